```python
import jax, jax.numpy as jnp
from jax import lax
import numpy as np

D_MODEL = 2048
BATCH = 4
SEQ = 8192
DEPTH = 4

EPS = 1e-6
N_BRANCH = 3
BRANCH_W = 1024

GLA_HEADS = 4
GLA_DK = 256
GLA_DV = 256
GLA_KW = GLA_HEADS * GLA_DK
GLA_VW = GLA_HEADS * GLA_DV
GLA_RANK = 16
GLA_TAU = 16.0
GLA_CHUNK = 64

NSA_HEADS = 16
NSA_GROUPS = 2
NSA_HPG = NSA_HEADS // NSA_GROUPS
NSA_DH = 64
NSA_QW = NSA_HEADS * NSA_DH
NSA_KVW = NSA_GROUPS * NSA_DH
NSA_CMP_LEN = 32
NSA_CMP_STRIDE = 16
NSA_SEL_BLOCK = 64
NSA_TOPN = 16
NSA_WINDOW = 512
NSA_QBLOCK = 128
BIG = 1e30

SSM_HEADS = 16
SSM_HEADDIM = 64
SSM_INNER = SSM_HEADS * SSM_HEADDIM
SSM_GROUPS = 4
SSM_HPG = SSM_HEADS // SSM_GROUPS
SSM_STATE = 128
SSM_CONV = 4
SSM_CONV_CH = SSM_INNER + 2 * SSM_GROUPS * SSM_STATE
SSM_CHUNK = 64

D_FF = ((8 * D_MODEL // 3 + 255) // 256) * 256

IN_SIZES = (GLA_KW, GLA_KW, GLA_VW, GLA_VW, GLA_RANK,
            NSA_QW, 6 * NSA_KVW, 3 * NSA_HEADS,
            SSM_INNER, SSM_CONV_CH, SSM_HEADS,
            N_BRANCH * D_MODEL)
D_IN = sum(IN_SIZES)
SPLIT_POINTS = tuple(int(v) for v in np.cumsum(IN_SIZES)[:-1])

kernel_name = 'hybrid_gla_nsa_ssd_trunk'


def rms_norm(x, g):
    xf = x.astype(jnp.float32)
    y = xf * lax.rsqrt(jnp.mean(xf * xf, axis=-1, keepdims=True) + EPS)
    return (y * g).astype(x.dtype)


def masked_softmax(s, mask):
    s = jnp.where(mask, s.astype(jnp.float32), -BIG)
    p = jax.nn.softmax(s, axis=-1)
    return jnp.where(mask, p, 0.0)


def alibi_slopes(n):
    return jnp.asarray(2.0 ** (-8.0 * np.arange(1, n + 1) / n), jnp.float32)


def causal_depthwise_conv(x, w, b):
    K = w.shape[0]
    y = lax.conv_general_dilated(x, w[:, None, :], window_strides=(1,), padding=((K - 1, 0),),
                                 dimension_numbers=('NWC', 'WIO', 'NWC'),
                                 feature_group_count=x.shape[-1])
    return y + b


def gla_mixer(q, k, v, r, a_low, w_a2, b_a, norm_g):
    f32 = jnp.float32
    bsz, T, _ = q.shape
    C = GLA_CHUNK
    n = T // C
    log_alpha = jax.nn.log_sigmoid((a_low @ w_a2 + b_a).astype(f32)) / GLA_TAU

    def chunks(z, d):
        return z.reshape(bsz, n, C, GLA_HEADS, d).transpose(0, 3, 1, 2, 4).astype(f32)

    qc = chunks(q, GLA_DK) * (GLA_DK ** -0.5)
    kc = chunks(k, GLA_DK)
    vc = chunks(v, GLA_DV)
    bcum = jnp.cumsum(chunks(log_alpha, GLA_DK), axis=3)
    b_last = bcum[:, :, :, -1:, :]
    q_dec = qc * jnp.exp(bcum)
    k_inv = kc * jnp.exp(-bcum)
    k_end = kc * jnp.exp(b_last - bcum)
    causal = jnp.tril(jnp.ones((C, C), dtype=bool))
    att = jnp.where(causal, jnp.einsum('bhncd,bhnsd->bhncs', q_dec, k_inv), 0.0)
    o_intra = jnp.einsum('bhncs,bhnse->bhnce', att, vc)

    def step(S, inp):
        qd, ke, vv, dl = inp
        o = jnp.einsum('bhcd,bhde->bhce', qd, S)
        S = jnp.exp(dl)[..., None] * S + jnp.einsum('bhcd,bhce->bhde', ke, vv)
        return S, o

    S0 = jnp.zeros((bsz, GLA_HEADS, GLA_DK, GLA_DV), f32)
    xs = (q_dec.transpose(2, 0, 1, 3, 4), k_end.transpose(2, 0, 1, 3, 4),
          vc.transpose(2, 0, 1, 3, 4), b_last[:, :, :, 0, :].transpose(2, 0, 1, 3))
    _, o_inter = lax.scan(step, S0, xs)
    o = o_intra + o_inter.transpose(1, 2, 0, 3, 4)
    o = o.transpose(0, 2, 3, 1, 4).reshape(bsz, T, GLA_HEADS, GLA_DV)
    o = rms_norm(o, norm_g).reshape(bsz, T, GLA_VW) * jax.nn.silu(r.astype(f32))
    return o.astype(q.dtype)


def nsa_mixer(q, kv, gate_logits, cmp_pos, cmp_w1, cmp_w2):
    f32 = jnp.float32
    bsz, T, _ = q.shape
    G, Hg, dh = NSA_GROUPS, NSA_HPG, NSA_DH
    q = q.reshape(bsz, T, G, Hg, dh).astype(f32) * (dh ** -0.5)
    k_c, v_c, k_s, v_s, k_w, v_w = jnp.split(kv, 6, axis=-1)

    def heads(z):
        return z.reshape(bsz, T, G, dh).transpose(0, 2, 1, 3).astype(f32)

    n_cmp = (T - NSA_CMP_LEN) // NSA_CMP_STRIDE + 1
    cmp_start = np.arange(n_cmp) * NSA_CMP_STRIDE
    tok_idx = cmp_start[:, None] + np.arange(NSA_CMP_LEN)[None, :]
    cmp_end = jnp.asarray(cmp_start + NSA_CMP_LEN - 1, f32)

    def compress(z, j):
        blocks = heads(z)[:, :, tok_idx] + cmp_pos[j]
        flat = blocks.reshape(bsz, G, n_cmp, NSA_CMP_LEN * dh)
        return jax.nn.silu(flat @ cmp_w1[j]) @ cmp_w2[j]

    kc = compress(k_c, 0)
    vc = compress(v_c, 1)

    n_sel = T // NSA_SEL_BLOCK
    n_top = min(NSA_TOPN, n_sel)
    ks = heads(k_s).reshape(bsz, G, n_sel, NSA_SEL_BLOCK, dh)
    vs = heads(v_s).reshape(bsz, G, n_sel, NSA_SEL_BLOCK, dh)
    overlap = np.zeros((n_cmp, n_sel), np.float32)
    np.add.at(overlap, (np.repeat(np.arange(n_cmp), NSA_CMP_LEN), (tok_idx // NSA_SEL_BLOCK).ravel()),
              1.0 / NSA_CMP_LEN)
    overlap = jnp.asarray(overlap)
    blk = jnp.arange(n_sel)

    pad = ((0, 0), (0, 0), (NSA_WINDOW, 0), (0, 0))
    kw = jnp.pad(heads(k_w), pad)
    vw = jnp.pad(heads(v_w), pad)

    gates = jax.nn.sigmoid(gate_logits.astype(f32)).reshape(bsz, T, 3, G, Hg)
    QB = NSA_QBLOCK
    nq = T // QB
    q_blk = q.reshape(bsz, nq, QB, G, Hg, dh).transpose(1, 0, 3, 4, 2, 5)
    g_blk = gates.reshape(bsz, nq, QB, 3, G, Hg).transpose(1, 3, 0, 4, 5, 2)
    slopes = alibi_slopes(NSA_HEADS).reshape(1, G, Hg, 1, 1)
    b_ix = jnp.arange(bsz)[:, None, None, None]
    g_ix = jnp.arange(G)[None, :, None, None]

    def one_block(inp):
        ci, qb, gb = inp
        t = ci * QB + jnp.arange(QB)
        tf = t.astype(f32)
        dist_c = tf[:, None] - cmp_end[None, :]
        s_c = jnp.einsum('bghqd,bgnd->bghqn', qb, kc) - slopes * dist_c
        p_c = masked_softmax(s_c, dist_c >= 0)
        o_cmp = jnp.einsum('bghqn,bgnd->bghqd', p_c, vc)
        imp = jnp.einsum('bghqn,ns->bgqs', p_c, overlap)
        cur = t // NSA_SEL_BLOCK
        forced = (blk[None, :] == 0) | (blk[None, :] == cur[:, None]) | (blk[None, :] == cur[:, None] - 1)
        future = blk[None, :] * NSA_SEL_BLOCK > t[:, None]
        imp = jnp.where(forced, BIG, jnp.where(future, -BIG, imp))
        _, idx = lax.top_k(imp, n_top)
        k_sel = ks[b_ix, g_ix, idx].reshape(bsz, G, QB, n_top * NSA_SEL_BLOCK, dh)
        v_sel = vs[b_ix, g_ix, idx].reshape(bsz, G, QB, n_top * NSA_SEL_BLOCK, dh)
        pos = (idx[..., None] * NSA_SEL_BLOCK + jnp.arange(NSA_SEL_BLOCK)).reshape(bsz, G, QB, -1)
        d_s = t[None, None, :, None] - pos
        s_s = jnp.einsum('bghqd,bgqsd->bghqs', qb, k_sel) - slopes * d_s[:, :, None].astype(f32)
        p_s = masked_softmax(s_s, (d_s >= 0)[:, :, None])
        o_sel = jnp.einsum('bghqs,bgqsd->bghqd', p_s, v_sel)
        k_win = lax.dynamic_slice_in_dim(kw, ci * QB, QB + NSA_WINDOW, axis=2)
        v_win = lax.dynamic_slice_in_dim(vw, ci * QB, QB + NSA_WINDOW, axis=2)
        pos_w = ci * QB - NSA_WINDOW + jnp.arange(QB + NSA_WINDOW)
        d_w = t[:, None] - pos_w[None, :]
        mask_w = (d_w >= 0) & (d_w < NSA_WINDOW) & (pos_w >= 0)[None, :]
        s_w = jnp.einsum('bghqd,bgkd->bghqk', qb, k_win) - slopes * d_w.astype(f32)
        p_w = masked_softmax(s_w, mask_w)
        o_win = jnp.einsum('bghqk,bgkd->bghqd', p_w, v_win)
        return gb[0][..., None] * o_cmp + gb[1][..., None] * o_sel + gb[2][..., None] * o_win

    out = lax.map(one_block, (jnp.arange(nq), q_blk, g_blk))
    out = out.transpose(1, 0, 4, 2, 3, 5).reshape(bsz, T, NSA_QW)
    return out.astype(kv.dtype)


def ssd_mixer(z, xbc, dt_raw, conv_w, conv_b, dt_bias, a_log, d_skip, norm_g):
    f32 = jnp.float32
    bsz, T, _ = z.shape
    G, Hp, P, N, L = SSM_GROUPS, SSM_HPG, SSM_HEADDIM, SSM_STATE, SSM_CHUNK
    nc = T // L
    xbc = jax.nn.silu(causal_depthwise_conv(xbc, conv_w, conv_b).astype(f32))
    xs, bm, cm = jnp.split(xbc, [SSM_INNER, SSM_INNER + G * N], axis=-1)
    x = xs.reshape(bsz, nc, L, G, Hp, P)
    bc = bm.reshape(bsz, nc, L, G, N)
    cc = cm.reshape(bsz, nc, L, G, N)
    dt = jax.nn.softplus(dt_raw.astype(f32) + dt_bias.astype(f32)).reshape(bsz, nc, L, G, Hp)
    a = dt * (-jnp.exp(a_log.astype(f32))).reshape(G, Hp)
    cum = jnp.cumsum(a, axis=2)
    cum_last = cum[:, :, -1]
    xdt = x * dt[..., None]
    cum_t = jnp.moveaxis(cum, 2, -1)
    seg = cum_t[..., :, None] - cum_t[..., None, :]
    causal = jnp.tril(jnp.ones((L, L), dtype=bool))
    decay = jnp.exp(jnp.where(causal, seg, -jnp.inf))
    cb = jnp.einsum('bnlgs,bnmgs->bnglm', cc, bc)
    y_intra = jnp.einsum('bnghlm,bnmghp->bnlghp', cb[:, :, :, None] * decay, xdt)
    dec_in = jnp.exp(cum)
    dec_end = jnp.exp(cum_last[:, :, None] - cum)

    def step(S, inp):
        c_n, b_n, xdt_n, din_n, dend_n, dl_n = inp
        y = jnp.einsum('blgs,bghps->blghp', c_n, S) * din_n[..., None]
        S = jnp.exp(dl_n)[..., None, None] * S + jnp.einsum('blgh,blgs,blghp->bghps', dend_n, b_n, xdt_n)
        return S, y

    def lead(arr):
        return jnp.moveaxis(arr, 1, 0)

    S0 = jnp.zeros((bsz, G, Hp, P, N), f32)
    _, y_inter = lax.scan(step, S0, (lead(cc), lead(bc), lead(xdt), lead(dec_in), lead(dec_end), lead(cum_last)))
    y = y_intra + jnp.moveaxis(y_inter, 0, 1) + x * d_skip.astype(f32).reshape(G, Hp, 1)
    y = y.reshape(bsz, T, SSM_INNER) * jax.nn.silu(z.astype(f32))
    y = rms_norm(y.reshape(bsz, T, G, SSM_INNER // G), norm_g.reshape(G, SSM_INNER // G))
    return y.reshape(bsz, T, SSM_INNER).astype(z.dtype)


def setup_inputs(seed: int = 0) -> dict:
    key = jax.random.key(seed)
    k = jax.random.split(key, 24)
    f32 = jnp.float32
    L, D = DEPTH, D_MODEL

    def dense(kk, shape, fan_in):
        return jax.random.normal(kk, shape, f32) * (fan_in ** -0.5)

    def gain(kk, shape):
        return 1.0 + 0.02 * jax.random.normal(kk, shape, f32)

    def small(kk, shape):
        return 0.02 * jax.random.normal(kk, shape, f32)

    dt0 = jnp.exp(jax.random.uniform(k[9], (L, SSM_HEADS), f32, np.log(1e-3), np.log(1e-1)))
    return {
        'x': jax.random.normal(k[0], (BATCH, SEQ, D), f32),
        'w_in': dense(k[1], (L, D, D_IN), D),
        'gla_a2': dense(k[2], (L, GLA_RANK, GLA_KW), GLA_RANK),
        'gla_a_bias': small(k[3], (L, GLA_KW)),
        'gla_norm': gain(k[4], (L, GLA_DV)),
        'nsa_cmp_pos': small(k[5], (L, 2, NSA_CMP_LEN, NSA_DH)),
        'nsa_cmp_w1': dense(k[6], (L, 2, NSA_CMP_LEN * NSA_DH, NSA_DH), NSA_CMP_LEN * NSA_DH),
        'nsa_cmp_w2': dense(k[7], (L, 2, NSA_DH, NSA_DH), NSA_DH),
        'ssm_conv_w': dense(k[8], (L, SSM_CONV, SSM_CONV_CH), SSM_CONV),
        'ssm_conv_b': small(k[10], (L, SSM_CONV_CH)),
        'ssm_dt_bias': dt0 + jnp.log(-jnp.expm1(-dt0)),
        'ssm_a_log': jnp.log(jax.random.uniform(k[11], (L, SSM_HEADS), f32, 1.0, 16.0)),
        'ssm_d': 1.0 + 0.1 * jax.random.normal(k[12], (L, SSM_HEADS), f32),
        'ssm_norm': gain(k[13], (L, SSM_INNER)),
        'w_branch': dense(k[14], (L, N_BRANCH, BRANCH_W, D), BRANCH_W),
        'w_out': dense(k[15], (L, D, D), D),
        'norm_pre_mix': gain(k[16], (L, D)),
        'norm_post_mix': gain(k[17], (L, D)),
        'norm_pre_ffn': gain(k[18], (L, D)),
        'norm_post_ffn': gain(k[19], (L, D)),
        'w_ffn_gate': dense(k[20], (L, D, D_FF), D),
        'w_ffn_up': dense(k[21], (L, D, D_FF), D),
        'w_ffn_down': dense(k[22], (L, D_FF, D), D_FF),
    }


def reference(x, w_in, gla_a2, gla_a_bias, gla_norm, nsa_cmp_pos, nsa_cmp_w1, nsa_cmp_w2,
              ssm_conv_w, ssm_conv_b, ssm_dt_bias, ssm_a_log, ssm_d, ssm_norm,
              w_branch, w_out, norm_pre_mix, norm_post_mix, norm_pre_ffn, norm_post_ffn,
              w_ffn_gate, w_ffn_up, w_ffn_down):
    bsz, T, D = x.shape
    for l in range(DEPTH):
        h = rms_norm(x, norm_pre_mix[l])
        proj = h @ w_in[l]
        (g_q, g_k, g_v, g_r, g_a, n_q, n_kv, n_g,
         s_z, s_xbc, s_dt, m_g) = jnp.split(proj, SPLIT_POINTS, axis=-1)
        y_gla = gla_mixer(g_q, g_k, g_v, g_r, g_a, gla_a2[l], gla_a_bias[l], gla_norm[l])
        y_nsa = nsa_mixer(n_q, n_kv, n_g, nsa_cmp_pos[l], nsa_cmp_w1[l], nsa_cmp_w2[l])
        y_ssm = ssd_mixer(s_z, s_xbc, s_dt, ssm_conv_w[l], ssm_conv_b[l], ssm_dt_bias[l],
                          ssm_a_log[l], ssm_d[l], ssm_norm[l])
        gate = jax.nn.sigmoid(m_g.astype(jnp.float32)).reshape(bsz, T, N_BRANCH, D)
        merged = (gate[:, :, 0] * (y_gla @ w_branch[l, 0])
                  + gate[:, :, 1] * (y_nsa @ w_branch[l, 1])
                  + gate[:, :, 2] * (y_ssm @ w_branch[l, 2])).astype(x.dtype)
        x = x + rms_norm(merged @ w_out[l], norm_post_mix[l])
        h = rms_norm(x, norm_pre_ffn[l])
        f = (jax.nn.silu(h @ w_ffn_gate[l]) * (h @ w_ffn_up[l])) @ w_ffn_down[l]
        x = x + rms_norm(f, norm_post_ffn[l])
    return x
```

```python
import functools

import numpy as np
import jax
import jax.numpy as jnp
from jax import lax
from jax.experimental import pallas as pl
from jax.experimental.pallas import tpu as pltpu

F32 = jnp.float32
BF16 = jnp.bfloat16

D_MODEL = 2048
EPS = 1e-6
N_BRANCH = 3
BRANCH_W = 1024
GLA_HEADS, GLA_DK, GLA_DV, GLA_RANK, GLA_TAU, GLA_CHUNK = 4, 256, 256, 16, 16.0, 64
GLA_KW = GLA_HEADS * GLA_DK
GLA_VW = GLA_HEADS * GLA_DV
NSA_HEADS, NSA_GROUPS, NSA_DH = 16, 2, 64
NSA_HPG = NSA_HEADS // NSA_GROUPS
NSA_QW = NSA_HEADS * NSA_DH
NSA_KVW = NSA_GROUPS * NSA_DH
NSA_CMP_LEN, NSA_CMP_STRIDE, NSA_SEL_BLOCK, NSA_TOPN, NSA_WINDOW = 32, 16, 64, 16, 512
BIG = 1e30
SSM_HEADS, SSM_HEADDIM, SSM_GROUPS, SSM_STATE, SSM_CONV, SSM_CHUNK = 16, 64, 4, 128, 4, 64
SSM_INNER = SSM_HEADS * SSM_HEADDIM
SSM_HPG = SSM_HEADS // SSM_GROUPS
SSM_CONV_CH = SSM_INNER + 2 * SSM_GROUPS * SSM_STATE
D_FF = ((8 * D_MODEL // 3 + 255) // 256) * 256
IN_SIZES = (GLA_KW, GLA_KW, GLA_VW, GLA_VW, GLA_RANK, NSA_QW, 6 * NSA_KVW, 3 * NSA_HEADS,
            SSM_INNER, SSM_CONV_CH, SSM_HEADS, N_BRANCH * D_MODEL)
SPLIT_POINTS = tuple(int(v) for v in np.cumsum(IN_SIZES)[:-1])

LANES = 128
VMEM_LIMIT = 56 * 1024 * 1024

OFF_MG = 0
OFF_GQ = OFF_MG + N_BRANCH * D_MODEL
OFF_GK = OFF_GQ + GLA_KW
OFF_GV = OFF_GK + GLA_KW
OFF_GR = OFF_GV + GLA_VW
OFF_NQ = OFF_GR + GLA_VW
OFF_SZ = OFF_NQ + NSA_QW
OFF_SX = OFF_SZ + SSM_INNER
OFF_SB = OFF_SX + SSM_INNER
OFF_SC = OFF_SB + SSM_GROUPS * SSM_STATE
OFF_NKV = OFF_SC + SSM_GROUPS * SSM_STATE
OFF_SMALL = OFF_NKV + 6 * NSA_KVW
D_PK = OFF_SMALL + 2 * LANES
SM_GA = 0
SM_NG = 16
SM_DT = 64


def _cparams(sem):
    return pltpu.CompilerParams(dimension_semantics=sem, vmem_limit_bytes=VMEM_LIMIT)


def _split3(x):
    hi = x.astype(BF16)
    r1 = x - hi.astype(F32)
    mid = r1.astype(BF16)
    lo = (r1 - mid.astype(F32)).astype(BF16)
    return hi, mid, lo


def _dot(a, b):
    return jnp.dot(a, b, preferred_element_type=F32)


def _dot_nt(a, b):
    return lax.dot_general(a, b, (((1,), (1,)), ((), ())), preferred_element_type=F32)


def _dot01_left(m01, x):
    hi, mid, lo = _split3(x)
    return _dot(m01, hi) + _dot(m01, mid) + _dot(m01, lo)


def _dot01_right(x, m01):
    hi, mid, lo = _split3(x)
    return _dot(hi, m01) + _dot(mid, m01) + _dot(lo, m01)


def _silu(x):
    return x / (1.0 + jnp.exp(-x))


def _rmsnorm_kernel(x_ref, g_ref, o_ref):
    x = x_ref[...]
    y = x * lax.rsqrt(jnp.mean(x * x, axis=-1, keepdims=True) + EPS)
    o_ref[...] = (y * g_ref[...]).astype(o_ref.dtype)


def rmsnorm_bf16(x, g, tm=1024):
    m, d = x.shape
    tm = min(tm, m)
    return pl.pallas_call(
        _rmsnorm_kernel,
        grid=(m // tm,),
        in_specs=[pl.BlockSpec((tm, d), lambda i: (i, 0)), pl.BlockSpec((1, d), lambda i: (0, 0))],
        out_specs=pl.BlockSpec((tm, d), lambda i: (i, 0)),
        out_shape=jax.ShapeDtypeStruct((m, d), BF16),
        compiler_params=_cparams(("parallel",)),
        name="rmsnorm",
    )(x, g.reshape(1, d))


def _mm_kernel(a_ref, w_ref, o_ref):
    o_ref[...] = _dot(a_ref[...], w_ref[...]).astype(o_ref.dtype)


def matmul(a, w, out_dtype=F32, tm=1024, tn=1024):
    m, k = a.shape
    n = w.shape[1]
    tm = min(tm, m)
    tn = min(tn, n)
    return pl.pallas_call(
        _mm_kernel,
        grid=(m // tm, n // tn),
        in_specs=[pl.BlockSpec((tm, k), lambda i, j: (i, 0)), pl.BlockSpec((k, tn), lambda i, j: (0, j))],
        out_specs=pl.BlockSpec((tm, tn), lambda i, j: (i, j)),
        out_shape=jax.ShapeDtypeStruct((m, n), out_dtype),
        compiler_params=_cparams(("parallel", "parallel")),
        name="in_proj",
    )(a, w)


def _merge_kernel(yg_ref, yn_ref, ys_ref, wg_ref, wn_ref, ws_ref, g0_ref, g1_ref, g2_ref, o_ref):
    def gate(ref):
        return 1.0 / (1.0 + jnp.exp(-ref[...]))

    acc = gate(g0_ref) * _dot(yg_ref[...], wg_ref[...])
    acc += gate(g1_ref) * _dot(yn_ref[...], wn_ref[...])
    acc += gate(g2_ref) * _dot(ys_ref[...], ws_ref[...])
    o_ref[...] = acc.astype(o_ref.dtype)


def merge_branches(y_gla, y_nsa, y_ssm, w_branch, proj, tm=1024, tn=512):
    m = y_gla.shape[0]
    d = w_branch.shape[-1]
    tm = min(tm, m)
    nj = d // tn
    ys = pl.BlockSpec((tm, BRANCH_W), lambda i, j: (i, 0))

    def wspec(b):
        return pl.BlockSpec((None, BRANCH_W, tn), lambda i, j, b=b: (b, 0, j))

    def gspec(b):
        return pl.BlockSpec((tm, tn), lambda i, j, b=b: (i, (OFF_MG + b * D_MODEL) // tn + j))

    return pl.pallas_call(
        _merge_kernel,
        grid=(m // tm, nj),
        in_specs=[ys, ys, ys, wspec(0), wspec(1), wspec(2), gspec(0), gspec(1), gspec(2)],
        out_specs=pl.BlockSpec((tm, tn), lambda i, j: (i, j)),
        out_shape=jax.ShapeDtypeStruct((m, d), BF16),
        compiler_params=_cparams(("parallel", "parallel")),
        name="merge",
    )(y_gla, y_nsa, y_ssm, w_branch, w_branch, w_branch, proj, proj, proj)


def _proj_norm_res_kernel(a_ref, w_ref, x_ref, g_ref, o_ref, acc_ref):
    k = pl.program_id(1)

    @pl.when(k == 0)
    def _():
        acc_ref[...] = jnp.zeros_like(acc_ref)

    acc_ref[...] += _dot(a_ref[...], w_ref[...])

    @pl.when(k == pl.num_programs(1) - 1)
    def _():
        f = acc_ref[...]
        y = f * lax.rsqrt(jnp.mean(f * f, axis=-1, keepdims=True) + EPS)
        o_ref[...] = x_ref[...] + y * g_ref[...]


def proj_norm_residual(a, w, x, g, tm=1024, tk=512):
    m, kk = a.shape
    d = w.shape[1]
    tm = min(tm, m)
    return pl.pallas_call(
        _proj_norm_res_kernel,
        grid=(m // tm, kk // tk),
        in_specs=[pl.BlockSpec((tm, tk), lambda i, k: (i, k)),
                  pl.BlockSpec((tk, d), lambda i, k: (k, 0)),
                  pl.BlockSpec((tm, d), lambda i, k: (i, 0)),
                  pl.BlockSpec((1, d), lambda i, k: (0, 0))],
        out_specs=pl.BlockSpec((tm, d), lambda i, k: (i, 0)),
        out_shape=jax.ShapeDtypeStruct((m, d), F32),
        scratch_shapes=[pltpu.VMEM((tm, d), F32)],
        compiler_params=_cparams(("parallel", "arbitrary")),
        name="proj_norm_res",
    )(a, w, x, g.reshape(1, d))


def _ffn_up_kernel(h_ref, wg_ref, wu_ref, o_ref):
    h = h_ref[...]
    a = _dot(h, wg_ref[...])
    u = _dot(h, wu_ref[...])
    o_ref[...] = (_silu(a) * u).astype(o_ref.dtype)


def ffn_up(h, wg, wu, tm=1024, tn=512):
    m, k = h.shape
    n = wg.shape[1]
    tm = min(tm, m)
    return pl.pallas_call(
        _ffn_up_kernel,
        grid=(m // tm, n // tn),
        in_specs=[pl.BlockSpec((tm, k), lambda i, j: (i, 0)),
                  pl.BlockSpec((k, tn), lambda i, j: (0, j)),
                  pl.BlockSpec((k, tn), lambda i, j: (0, j))],
        out_specs=pl.BlockSpec((tm, tn), lambda i, j: (i, j)),
        out_shape=jax.ShapeDtypeStruct((m, n), BF16),
        compiler_params=_cparams(("parallel", "parallel")),
        name="ffn_up",
    )(h, wg, wu)


def _gla_kernel(q_ref, k_ref, v_ref, r_ref, sm_ref, wa_ref, ba_ref, ng_ref, tri_ref, o_ref,
                st_ref, la_ref, *, nchunk):
    C = GLA_CHUNK

    @pl.when(pl.program_id(2) == 0)
    def _():
        st_ref[...] = jnp.zeros_like(st_ref)

    pre = _dot(sm_ref[...].astype(BF16), wa_ref[...]) + ba_ref[...]
    la_ref[...] = (jnp.minimum(pre, 0.0) - jnp.log1p(jnp.exp(-jnp.abs(pre)))) * (1.0 / GLA_TAU)
    tri = tri_ref[...]
    causal = lax.broadcasted_iota(jnp.int32, (C, C), 0) >= lax.broadcasted_iota(jnp.int32, (C, C), 1)

    def chunk(c, carry):
        rows = pl.ds(pl.multiple_of(c * C, C), C)
        bcum = _dot01_left(tri, la_ref[rows, :])
        b_last = bcum[C - 1:C, :]
        q = q_ref[rows, :] * (GLA_DK ** -0.5)
        k = k_ref[rows, :]
        v = v_ref[rows, :]
        q_dec = (q * jnp.exp(bcum)).astype(BF16)
        k_inv = (k * jnp.exp(-bcum)).astype(BF16)
        k_end = (k * jnp.exp(b_last - bcum)).astype(BF16)
        vb = v.astype(BF16)
        att = jnp.where(causal, _dot_nt(q_dec, k_inv), 0.0)
        st = st_ref[...]
        o = _dot(att.astype(BF16), vb) + _dot_nt(q_dec, st.astype(BF16))
        st_ref[...] = jnp.exp(b_last) * st + _dot(v.T.astype(BF16), k_end)
        y = o * lax.rsqrt(jnp.mean(o * o, axis=-1, keepdims=True) + EPS) * ng_ref[...]
        o_ref[rows, :] = (y * _silu(r_ref[rows, :])).astype(o_ref.dtype)
        return carry

    lax.fori_loop(0, nchunk, chunk, 0)


def gla_mixer(proj3, wa_pad, ba, norm_g, tb=512):
    bsz, T, _ = proj3.shape
    tb = min(tb, T)
    nchunk = tb // GLA_CHUNK
    W = GLA_DK
    tri = jnp.asarray(np.tril(np.ones((GLA_CHUNK, GLA_CHUNK), np.float32)), BF16)

    def col(off):
        return pl.BlockSpec((None, tb, W), lambda b, h, t, o=off // W: (b, t, o + h))

    return pl.pallas_call(
        functools.partial(_gla_kernel, nchunk=nchunk),
        grid=(bsz, GLA_HEADS, T // tb),
        in_specs=[col(OFF_GQ), col(OFF_GK), col(OFF_GV), col(OFF_GR),
                  pl.BlockSpec((None, tb, LANES), lambda b, h, t: (b, t, OFF_SMALL // LANES)),
                  pl.BlockSpec((LANES, W), lambda b, h, t: (0, h)),
                  pl.BlockSpec((1, W), lambda b, h, t: (0, h)),
                  pl.BlockSpec((1, GLA_DV), lambda b, h, t: (0, 0)),
                  pl.BlockSpec((GLA_CHUNK, GLA_CHUNK), lambda b, h, t: (0, 0))],
        out_specs=pl.BlockSpec((None, tb, GLA_DV), lambda b, h, t: (b, t, h)),
        out_shape=jax.ShapeDtypeStruct((bsz, T, GLA_VW), BF16),
        scratch_shapes=[pltpu.VMEM((GLA_DV, GLA_DK), F32), pltpu.VMEM((tb, GLA_DK), F32)],
        compiler_params=_cparams(("parallel", "parallel", "arbitrary")),
        name="gla",
    )(proj3, proj3, proj3, proj3, proj3, wa_pad, ba.reshape(1, GLA_KW), norm_g.reshape(1, GLA_DV), tri)


def _ssd_kernel(z_ref, x_ref, bm_ref, cm_ref, sm_ref, cwx_ref, cwb_ref, cwc_ref, cbx_ref, cbb_ref, cbc_ref,
                dtb_ref, alog_ref, dskip_ref, ng_ref, tri_ref, exp_ref, o_ref,
                st_ref, extx_ref, extb_ref, extc_ref, xa_ref, ba_ref, ca_ref, dt_ref, a_ref, *, tb, nchunk):
    L = SSM_CHUNK
    GW = SSM_HPG * SSM_HEADDIM
    NS = SSM_STATE
    first = pl.program_id(1) == 0

    @pl.when(first)
    def _():
        st_ref[...] = jnp.zeros_like(st_ref)

    def conv_silu(src_ref, ext_ref, w_ref, b_ref, dst_ref):
        @pl.when(first)
        def _():
            ext_ref[0:8, :] = jnp.zeros((8, ext_ref.shape[1]), F32)

        @pl.when(jnp.logical_not(first))
        def _():
            ext_ref[0:8, :] = ext_ref[tb:tb + 8, :]

        ext_ref[8:8 + tb, :] = src_ref[...]
        acc = b_ref[...] + w_ref[SSM_CONV - 1:SSM_CONV, :] * ext_ref[8:8 + tb, :]
        for j in range(1, SSM_CONV):
            acc = acc + w_ref[SSM_CONV - 1 - j:SSM_CONV - j, :] * ext_ref[8 - j:8 - j + tb, :]
        dst_ref[...] = _silu(acc)

    conv_silu(x_ref, extx_ref, cwx_ref, cbx_ref, xa_ref)
    conv_silu(bm_ref, extb_ref, cwb_ref, cbb_ref, ba_ref)
    conv_silu(cm_ref, extc_ref, cwc_ref, cbc_ref, ca_ref)

    v = sm_ref[...] + dtb_ref[...]
    dt = jnp.maximum(v, 0.0) + jnp.log1p(jnp.exp(-jnp.abs(v)))
    dt_ref[...] = dt
    a_ref[...] = dt * (-jnp.exp(alog_ref[...]))

    tri = tri_ref[...]
    row = lax.broadcasted_iota(jnp.int32, (L, GW), 0)
    lane = lax.broadcasted_iota(jnp.int32, (L, GW), 1)
    lane_in = jnp.bitwise_and(lane, SSM_HEADDIM - 1)
    eye_t = (lane_in == row).astype(F32)
    tril_t = lane_in <= row
    bd_mask = (lax.broadcasted_iota(jnp.int32, (GW, GW), 0) // L
               == lax.broadcasted_iota(jnp.int32, (GW, GW), 1) // SSM_HEADDIM).astype(F32)

    def chunk(c, carry):
        rows = pl.ds(pl.multiple_of(c * L, L), L)
        cum128 = _dot01_left(tri, a_ref[rows, :])
        dt_c = dt_ref[rows, :]
        for g in range(SSM_GROUPS):
            cs = slice(g * GW, (g + 1) * GW)
            e_g = exp_ref[:, cs]
            cum_e = _dot01_right(cum128, e_g)
            dt_e = _dot01_right(dt_c, e_g)
            cum_last = cum_e[L - 1:L, :]
            r_row = jnp.sum(cum_e * eye_t, axis=0, keepdims=True)
            decay = jnp.exp(jnp.where(tril_t, cum_e - r_row, -jnp.inf))
            x_g = xa_ref[rows, cs]
            xdt = x_g * dt_e
            b_g = ba_ref[rows, g * NS:(g + 1) * NS]
            c_g = ca_ref[rows, g * NS:(g + 1) * NS].astype(BF16)
            bb = b_g.astype(BF16)
            cb_t = _dot_nt(c_g, jnp.concatenate([bb] * SSM_HPG, axis=0))
            xdt_bd = (jnp.concatenate([xdt] * SSM_HPG, axis=0) * bd_mask).astype(BF16)
            y = _dot((cb_t * decay).astype(BF16), xdt_bd)
            st = st_ref[g]
            y = y + _dot(c_g, st.astype(BF16)) * jnp.exp(cum_e)
            dend = jnp.exp(cum_last - cum_e)
            st_ref[g] = jnp.exp(cum_last) * st + _dot(b_g.T.astype(BF16), (dend * xdt).astype(BF16))
            y = y + x_g * dskip_ref[:, cs]
            y = y * _silu(z_ref[rows, cs])
            y = y * lax.rsqrt(jnp.mean(y * y, axis=-1, keepdims=True) + EPS) * ng_ref[:, cs]
            o_ref[rows, cs] = y.astype(o_ref.dtype)
        return carry

    lax.fori_loop(0, nchunk, chunk, 0)


def _ssd_consts():
    tri = jnp.asarray(np.tril(np.ones((SSM_CHUNK, SSM_CHUNK), np.float32)), BF16)
    e = np.zeros((LANES, SSM_INNER), np.float32)
    for h in range(SSM_HEADS):
        e[SM_DT + h, h * SSM_HEADDIM:(h + 1) * SSM_HEADDIM] = 1.0
    return tri, jnp.asarray(e, BF16)


def _small_row(v, off):
    return jnp.zeros((1, LANES), F32).at[0, off:off + v.shape[0]].set(v.astype(F32))


def ssd_mixer(proj3, conv_w, conv_b, dt_bias, a_log, d_skip, norm_g, tb=256):
    bsz, T, _ = proj3.shape
    tb = min(tb, T)
    nchunk = tb // SSM_CHUNK
    GN = SSM_GROUPS * SSM_STATE
    tri, expand = _ssd_consts()
    cwx, cwb, cwc = conv_w[:, :SSM_INNER], conv_w[:, SSM_INNER:SSM_INNER + GN], conv_w[:, SSM_INNER + GN:]
    cb2 = conv_b.reshape(1, SSM_CONV_CH)
    cbx, cbb, cbc = cb2[:, :SSM_INNER], cb2[:, SSM_INNER:SSM_INNER + GN], cb2[:, SSM_INNER + GN:]
    dtb = _small_row(dt_bias, SM_DT)
    alog = _small_row(a_log, SM_DT)
    dskip = jnp.repeat(d_skip.astype(F32), SSM_HEADDIM).reshape(1, SSM_INNER)

    def col(off, w):
        return pl.BlockSpec((None, tb, w), lambda b, t, o=off // w: (b, t, o))

    def full(shape):
        return pl.BlockSpec(shape, lambda b, t: (0,) * len(shape))

    return pl.pallas_call(
        functools.partial(_ssd_kernel, tb=tb, nchunk=nchunk),
        grid=(bsz, T // tb),
        in_specs=[col(OFF_SZ, SSM_INNER), col(OFF_SX, SSM_INNER), col(OFF_SB, GN), col(OFF_SC, GN),
                  col(OFF_SMALL, LANES),
                  full((SSM_CONV, SSM_INNER)), full((SSM_CONV, GN)), full((SSM_CONV, GN)),
                  full((1, SSM_INNER)), full((1, GN)), full((1, GN)),
                  full((1, LANES)), full((1, LANES)), full((1, SSM_INNER)), full((1, SSM_INNER)),
                  full((SSM_CHUNK, SSM_CHUNK)), full((LANES, SSM_INNER))],
        out_specs=pl.BlockSpec((None, tb, SSM_INNER), lambda b, t: (b, t, 0)),
        out_shape=jax.ShapeDtypeStruct((bsz, T, SSM_INNER), BF16),
        scratch_shapes=[pltpu.VMEM((SSM_GROUPS, SSM_STATE, SSM_HPG * SSM_HEADDIM), F32),
                        pltpu.VMEM((tb + 8, SSM_INNER), F32), pltpu.VMEM((tb + 8, GN), F32),
                        pltpu.VMEM((tb + 8, GN), F32),
                        pltpu.VMEM((tb, SSM_INNER), F32), pltpu.VMEM((tb, GN), F32), pltpu.VMEM((tb, GN), F32),
                        pltpu.VMEM((tb, LANES), F32), pltpu.VMEM((tb, LANES), F32)],
        compiler_params=_cparams(("parallel", "arbitrary")),
        name="ssd",
    )(proj3, proj3, proj3, proj3, proj3, cwx, cwb, cwc, cbx, cbb, cbc, dtb, alog, dskip,
      norm_g.reshape(1, SSM_INNER), tri, expand)


NSA_SLOPES = tuple(float(np.float32(2.0 ** (-8.0 * (i + 1) / NSA_HEADS))) for i in range(NSA_HEADS))
NSA_SCALE = NSA_DH ** -0.5
KV_W = 2 * NSA_KVW
NEG_HUGE = -3.0e38


def _head_cols(hh):
    return slice(hh * NSA_DH, (hh + 1) * NSA_DH)


def _nsa_compress_kernel(k_ref, v_ref, pos_ref, bd1_ref, bd2_ref, o_ref, *, n16):
    S = NSA_CMP_STRIDE
    top = jnp.zeros((n16, KV_W), F32)
    bot = jnp.zeros((n16, KV_W), F32)
    for l in range(S):
        rows = pl.ds(l, n16, stride=S)
        x = jnp.concatenate([k_ref[rows, :], v_ref[rows, :]], axis=-1)
        top += _dot((x + pos_ref[l:l + 1, :]).astype(BF16), bd1_ref[l])
        bot += _dot((x + pos_ref[S + l:S + l + 1, :]).astype(BF16), bd1_ref[S + l])
    pre = top + pltpu.roll(bot, n16 - 1, axis=0)
    out = _dot(_silu(pre).astype(BF16), bd2_ref[...])
    row = lax.broadcasted_iota(jnp.int32, (n16, KV_W), 0)
    o_ref[...] = jnp.where(row < n16 - 1, out, 0.0)


def nsa_compress(proj3, cmp_pos, cmp_w1, cmp_w2):
    bsz, T, _ = proj3.shape
    n16 = T // NSA_CMP_STRIDE
    sel = np.array([0, 0, 1, 1])
    eye = jnp.eye(4, dtype=F32)
    w1r = cmp_w1.reshape(2, NSA_CMP_LEN, NSA_DH, NSA_DH)[sel]
    bd1 = jnp.einsum('ab,alde->ladbe', eye, w1r).reshape(NSA_CMP_LEN, KV_W, KV_W).astype(BF16)
    bd2 = jnp.einsum('ab,ade->adbe', eye, cmp_w2[sel]).reshape(KV_W, KV_W).astype(BF16)
    pos = jnp.concatenate([cmp_pos[0], cmp_pos[0], cmp_pos[1], cmp_pos[1]], axis=-1)
    return pl.pallas_call(
        functools.partial(_nsa_compress_kernel, n16=n16),
        grid=(bsz,),
        in_specs=[pl.BlockSpec((None, T, NSA_KVW), lambda b: (b, 0, OFF_NKV // NSA_KVW)),
                  pl.BlockSpec((None, T, NSA_KVW), lambda b: (b, 0, OFF_NKV // NSA_KVW + 1)),
                  pl.BlockSpec((NSA_CMP_LEN, KV_W), lambda b: (0, 0)),
                  pl.BlockSpec((NSA_CMP_LEN, KV_W, KV_W), lambda b: (0, 0, 0)),
                  pl.BlockSpec((KV_W, KV_W), lambda b: (0, 0))],
        out_specs=pl.BlockSpec((None, n16, KV_W), lambda b: (b, 0, 0)),
        out_shape=jax.ShapeDtypeStruct((bsz, n16, KV_W), F32),
        compiler_params=_cparams(("parallel",)),
        name="nsa_compress",
    )(proj3, proj3, pos, bd1, bd2)


def _nsa_cmp_topk_kernel(q_ref, kcv_ref, sm_ref, ov_ref, o_ref, sel_ref, *, tq, n16, n_sel, n_top):
    t0 = pl.program_id(1) * tq
    tpos = t0 + lax.broadcasted_iota(jnp.int32, (tq, 1), 0)
    cmp_end = lax.broadcasted_iota(jnp.int32, (1, n16), 1) * NSA_CMP_STRIDE + (NSA_CMP_LEN - 1)
    dist = (tpos - cmp_end).astype(F32)
    valid = dist >= 0.0
    gates = 1.0 / (1.0 + jnp.exp(-sm_ref[...]))
    blk = lax.broadcasted_iota(jnp.int32, (1, n_sel), 1)
    blk_f = blk.astype(F32)
    cur = tpos // NSA_SEL_BLOCK
    forced = (blk == 0) | (blk == cur) | (blk == cur - 1)
    future = blk * NSA_SEL_BLOCK > tpos
    for g in range(NSA_GROUPS):
        kc = kcv_ref[:, g * NSA_DH:(g + 1) * NSA_DH].astype(BF16)
        vc = kcv_ref[:, NSA_KVW + g * NSA_DH:NSA_KVW + (g + 1) * NSA_DH].astype(BF16)
        psum = jnp.zeros((tq, n16), F32)
        for h in range(NSA_HPG):
            hh = g * NSA_HPG + h
            q = (q_ref[:, _head_cols(hh)] * NSA_SCALE).astype(BF16)
            s = jnp.where(valid, _dot_nt(q, kc) - NSA_SLOPES[hh] * dist, -BIG)
            e = jnp.where(valid, jnp.exp(s - jnp.max(s, axis=-1, keepdims=True)), 0.0)
            den = jnp.sum(e, axis=-1, keepdims=True)
            p = e * (1.0 / jnp.where(den > 0.0, den, 1.0))
            psum += p
            gate = gates[:, SM_NG + hh:SM_NG + hh + 1]
            o_ref[:, _head_cols(hh)] = gate * _dot(p.astype(BF16), vc)
        imp = _dot01_right(psum, ov_ref[...])
        work = jnp.where(forced, BIG, jnp.where(future, -BIG, imp))
        sel = jnp.zeros((tq, n_sel), F32)
        for _ in range(n_top):
            top = jnp.max(work, axis=-1, keepdims=True)
            idx = jnp.min(jnp.where(work == top, blk_f, float(n_sel)), axis=-1, keepdims=True)
            pick = blk_f == idx
            sel = jnp.where(pick, 1.0, sel)
            work = jnp.where(pick, NEG_HUGE, work)
        sel_ref[:, g * n_sel:(g + 1) * n_sel] = sel.astype(sel_ref.dtype)


def _nsa_overlap(n16, n_sel):
    n_cmp = n16 - 1
    tok = (np.arange(n_cmp) * NSA_CMP_STRIDE)[:, None] + np.arange(NSA_CMP_LEN)[None, :]
    ov = np.zeros((n16, n_sel), np.float32)
    np.add.at(ov, (np.repeat(np.arange(n_cmp), NSA_CMP_LEN), (tok // NSA_SEL_BLOCK).ravel()), 1.0 / NSA_CMP_LEN)
    return jnp.asarray(ov, BF16)


def nsa_cmp_topk(proj3, kcv, tq=256):
    bsz, T, _ = proj3.shape
    tq = min(tq, T)
    n16 = T // NSA_CMP_STRIDE
    n_sel = T // NSA_SEL_BLOCK
    n_top = min(NSA_TOPN, n_sel)
    return pl.pallas_call(
        functools.partial(_nsa_cmp_topk_kernel, tq=tq, n16=n16, n_sel=n_sel, n_top=n_top),
        grid=(bsz, T // tq),
        in_specs=[pl.BlockSpec((None, tq, NSA_QW), lambda b, i: (b, i, OFF_NQ // NSA_QW)),
                  pl.BlockSpec((None, n16, KV_W), lambda b, i: (b, 0, 0)),
                  pl.BlockSpec((None, tq, LANES), lambda b, i: (b, i, OFF_SMALL // LANES)),
                  pl.BlockSpec((n16, n_sel), lambda b, i: (0, 0))],
        out_specs=[pl.BlockSpec((None, tq, NSA_QW), lambda b, i: (b, i, 0)),
                   pl.BlockSpec((None, tq, NSA_GROUPS * n_sel), lambda b, i: (b, i, 0))],
        out_shape=[jax.ShapeDtypeStruct((bsz, T, NSA_QW), F32),
                   jax.ShapeDtypeStruct((bsz, T, NSA_GROUPS * n_sel), BF16)],
        compiler_params=_cparams(("parallel", "parallel")),
        name="nsa_cmp_topk",
    )(proj3, kcv, proj3, _nsa_overlap(n16, n_sel))


def _nsa_window_kernel(q_ref, kv0_ref, kv1_ref, kv2_ref, sm_ref, prev_ref, o_ref, *, tq):
    nb = NSA_WINDOW // tq + 1
    i = pl.program_id(1)
    tpos = i * tq + lax.broadcasted_iota(jnp.int32, (tq, 1), 0)
    kpos = (i - (nb - 1)) * tq + lax.broadcasted_iota(jnp.int32, (1, nb * tq), 1)
    d = tpos - kpos
    valid = (d >= 0) & (d < NSA_WINDOW) & (kpos >= 0)
    d_f = d.astype(F32)
    gates = 1.0 / (1.0 + jnp.exp(-sm_ref[...]))
    kv = jnp.concatenate([kv0_ref[...], kv1_ref[...], kv2_ref[...]], axis=0).astype(BF16)
    for g in range(NSA_GROUPS):
        kw = kv[:, g * NSA_DH:(g + 1) * NSA_DH]
        vw = kv[:, NSA_KVW + g * NSA_DH:NSA_KVW + (g + 1) * NSA_DH]
        for h in range(NSA_HPG):
            hh = g * NSA_HPG + h
            q = (q_ref[:, _head_cols(hh)] * NSA_SCALE).astype(BF16)
            s = jnp.where(valid, _dot_nt(q, kw) - NSA_SLOPES[hh] * d_f, -BIG)
            e = jnp.where(valid, jnp.exp(s - jnp.max(s, axis=-1, keepdims=True)), 0.0)
            p = e * (1.0 / jnp.sum(e, axis=-1, keepdims=True))
            gate = gates[:, SM_NG + 2 * NSA_HEADS + hh:SM_NG + 2 * NSA_HEADS + hh + 1]
            o_ref[:, _head_cols(hh)] = prev_ref[:, _head_cols(hh)] + gate * _dot(p.astype(BF16), vw)


def nsa_window(proj3, prev, tq=256):
    bsz, T, _ = proj3.shape
    tq = min(tq, T)
    assert NSA_WINDOW % tq == 0 and NSA_WINDOW // tq == 2
    cw = (OFF_NKV + 2 * KV_W) // KV_W

    def kvspec(back):
        return pl.BlockSpec((None, tq, KV_W), lambda b, i, back=back: (b, jnp.maximum(i - back, 0), cw))

    return pl.pallas_call(
        functools.partial(_nsa_window_kernel, tq=tq),
        grid=(bsz, T // tq),
        in_specs=[pl.BlockSpec((None, tq, NSA_QW), lambda b, i: (b, i, OFF_NQ // NSA_QW)),
                  kvspec(2), kvspec(1), kvspec(0),
                  pl.BlockSpec((None, tq, LANES), lambda b, i: (b, i, OFF_SMALL // LANES)),
                  pl.BlockSpec((None, tq, NSA_QW), lambda b, i: (b, i, 0))],
        out_specs=pl.BlockSpec((None, tq, NSA_QW), lambda b, i: (b, i, 0)),
        out_shape=jax.ShapeDtypeStruct((bsz, T, NSA_QW), F32),
        compiler_params=_cparams(("parallel", "parallel")),
        name="nsa_window",
    )(proj3, proj3, proj3, proj3, proj3, prev)


def _nsa_select_kernel(q_ref, kv_ref, sel_ref, e_ref, sm_ref, prev_ref, o_ref, m_ref, l_ref, acc_ref,
                       *, tq, tk, n_sel):
    i = pl.program_id(1)
    j = pl.program_id(2)

    @pl.when(j == 0)
    def _():
        m_ref[...] = jnp.full(m_ref.shape, -BIG, F32)
        l_ref[...] = jnp.zeros_like(l_ref)
        acc_ref[...] = jnp.zeros_like(acc_ref)

    @pl.when(j * tk <= i * tq + tq - 1)
    def _():
        tpos = i * tq + lax.broadcasted_iota(jnp.int32, (tq, 1), 0)
        kpos = j * tk + lax.broadcasted_iota(jnp.int32, (1, tk), 1)
        d = tpos - kpos
        causal = d >= 0
        d_f = d.astype(F32)
        kv = kv_ref[...].astype(BF16)
        for g in range(NSA_GROUPS):
            chosen = _dot(sel_ref[:, g * n_sel:(g + 1) * n_sel], e_ref[...])
            valid = (chosen > 0.5) & causal
            ks = kv[:, g * NSA_DH:(g + 1) * NSA_DH]
            vs = kv[:, NSA_KVW + g * NSA_DH:NSA_KVW + (g + 1) * NSA_DH]
            for h in range(NSA_HPG):
                hh = g * NSA_HPG + h
                q = (q_ref[:, _head_cols(hh)] * NSA_SCALE).astype(BF16)
                s = jnp.where(valid, _dot_nt(q, ks) - NSA_SLOPES[hh] * d_f, -BIG)
                m_old = m_ref[hh]
                m_new = jnp.maximum(m_old, jnp.max(s, axis=-1, keepdims=True))
                alpha = jnp.exp(m_old - m_new)
                p = jnp.where(valid, jnp.exp(s - m_new), 0.0)
                l_ref[hh] = alpha * l_ref[hh] + jnp.sum(p, axis=-1, keepdims=True)
                acc_ref[:, _head_cols(hh)] = alpha * acc_ref[:, _head_cols(hh)] + _dot(p.astype(BF16), vs)
                m_ref[hh] = m_new

    @pl.when(j == pl.num_programs(2) - 1)
    def _():
        gates = 1.0 / (1.0 + jnp.exp(-sm_ref[...]))
        for hh in range(NSA_HEADS):
            gate = gates[:, SM_NG + NSA_HEADS + hh:SM_NG + NSA_HEADS + hh + 1]
            o = acc_ref[:, _head_cols(hh)] * (1.0 / l_ref[hh])
            o_ref[:, _head_cols(hh)] = (prev_ref[:, _head_cols(hh)] + gate * o).astype(o_ref.dtype)


def nsa_select(proj3, sel, prev, tq=256, tk=512):
    bsz, T, _ = proj3.shape
    tq = min(tq, T)
    tk = min(tk, T)
    n_sel = T // NSA_SEL_BLOCK
    cs = (OFF_NKV + KV_W) // KV_W
    expand = jnp.asarray(np.arange(T)[None, :] // NSA_SEL_BLOCK == np.arange(n_sel)[:, None], BF16)

    def last_tile(i):
        return (i * tq + tq - 1) // tk

    return pl.pallas_call(
        functools.partial(_nsa_select_kernel, tq=tq, tk=tk, n_sel=n_sel),
        grid=(bsz, T // tq, T // tk),
        in_specs=[pl.BlockSpec((None, tq, NSA_QW), lambda b, i, j: (b, i, OFF_NQ // NSA_QW)),
                  pl.BlockSpec((None, tk, KV_W), lambda b, i, j: (b, jnp.minimum(j, last_tile(i)), cs)),
                  pl.BlockSpec((None, tq, NSA_GROUPS * n_sel), lambda b, i, j: (b, i, 0)),
                  pl.BlockSpec((n_sel, tk), lambda b, i, j: (0, jnp.minimum(j, last_tile(i)))),
                  pl.BlockSpec((None, tq, LANES), lambda b, i, j: (b, i, OFF_SMALL // LANES)),
                  pl.BlockSpec((None, tq, NSA_QW), lambda b, i, j: (b, i, 0))],
        out_specs=pl.BlockSpec((None, tq, NSA_QW), lambda b, i, j: (b, i, 0)),
        out_shape=jax.ShapeDtypeStruct((bsz, T, NSA_QW), BF16),
        scratch_shapes=[pltpu.VMEM((NSA_HEADS, tq, 1), F32), pltpu.VMEM((NSA_HEADS, tq, 1), F32),
                        pltpu.VMEM((tq, NSA_QW), F32)],
        compiler_params=_cparams(("parallel", "parallel", "arbitrary")),
        name="nsa_select",
    )(proj3, proj3, sel, expand, proj3, prev)


def nsa_mixer(proj3, cmp_pos, cmp_w1, cmp_w2):
    kcv = nsa_compress(proj3, cmp_pos, cmp_w1, cmp_w2)
    o_cmp, sel = nsa_cmp_topk(proj3, kcv)
    o_cw = nsa_window(proj3, o_cmp)
    return nsa_select(proj3, sel, o_cw)


def _pack_w_in(w_in):
    (g_q, g_k, g_v, g_r, g_a, n_q, n_kv, n_g, s_z, s_xbc, s_dt, m_g) = jnp.split(w_in, SPLIT_POINTS, axis=-1)
    pad = jnp.zeros(w_in.shape[:-1] + (D_PK - OFF_SMALL - GLA_RANK - 3 * NSA_HEADS - SSM_HEADS,), w_in.dtype)
    packed = jnp.concatenate([m_g, g_q, g_k, g_v, g_r, n_q, s_z, s_xbc, n_kv, g_a, n_g, s_dt, pad], axis=-1)
    return packed.astype(BF16)


def kernel(x, w_in, gla_a2, gla_a_bias, gla_norm, nsa_cmp_pos, nsa_cmp_w1, nsa_cmp_w2, ssm_conv_w, ssm_conv_b,
           ssm_dt_bias, ssm_a_log, ssm_d, ssm_norm, w_branch, w_out, norm_pre_mix, norm_post_mix, norm_pre_ffn,
           norm_post_ffn, w_ffn_gate, w_ffn_up, w_ffn_down):
    bsz, T, D = x.shape
    depth = w_in.shape[0]
    n = bsz * T
    w_in_pk = _pack_w_in(w_in)
    wa_pad = jnp.zeros((depth, LANES, GLA_KW), F32).at[:, SM_GA:SM_GA + GLA_RANK].set(gla_a2).astype(BF16)
    w_branch_b = w_branch.astype(BF16)
    w_out_b = w_out.astype(BF16)
    w_gate_b = w_ffn_gate.astype(BF16)
    w_up_b = w_ffn_up.astype(BF16)
    w_down_b = w_ffn_down.astype(BF16)
    xf = x.reshape(n, D)
    for l in range(depth):
        h = rmsnorm_bf16(xf, norm_pre_mix[l])
        proj = matmul(h, w_in_pk[l])
        proj3 = proj.reshape(bsz, T, D_PK)
        y_gla = gla_mixer(proj3, wa_pad[l], gla_a_bias[l], gla_norm[l])
        y_nsa = nsa_mixer(proj3, nsa_cmp_pos[l], nsa_cmp_w1[l], nsa_cmp_w2[l])
        y_ssm = ssd_mixer(proj3, ssm_conv_w[l], ssm_conv_b[l], ssm_dt_bias[l], ssm_a_log[l], ssm_d[l], ssm_norm[l])
        merged = merge_branches(y_gla.reshape(n, BRANCH_W), y_nsa.reshape(n, BRANCH_W), y_ssm.reshape(n, BRANCH_W),
                                w_branch_b[l], proj)
        xf = proj_norm_residual(merged, w_out_b[l], xf, norm_post_mix[l])
        h = rmsnorm_bf16(xf, norm_pre_ffn[l])
        act = ffn_up(h, w_gate_b[l], w_up_b[l])
        xf = proj_norm_residual(act, w_down_b[l], xf, norm_post_ffn[l])
    return xf.reshape(bsz, T, D)
```

```python
import functools

import numpy as np
import jax
import jax.numpy as jnp
from jax import lax
from jax.experimental import pallas as pl
from jax.experimental.pallas import tpu as pltpu

F32 = jnp.float32
BF16 = jnp.bfloat16

D_MODEL = 2048
EPS = 1e-6
N_BRANCH = 3
BRANCH_W = 1024
GLA_HEADS, GLA_DK, GLA_DV, GLA_RANK, GLA_TAU, GLA_CHUNK = 4, 256, 256, 16, 16.0, 64
GLA_KW = GLA_HEADS * GLA_DK
GLA_VW = GLA_HEADS * GLA_DV
NSA_HEADS, NSA_GROUPS, NSA_DH = 16, 2, 64
NSA_HPG = NSA_HEADS // NSA_GROUPS
NSA_QW = NSA_HEADS * NSA_DH
NSA_KVW = NSA_GROUPS * NSA_DH
NSA_CMP_LEN, NSA_CMP_STRIDE, NSA_SEL_BLOCK, NSA_TOPN, NSA_WINDOW = 32, 16, 64, 16, 512
BIG = 1e30
SSM_HEADS, SSM_HEADDIM, SSM_GROUPS, SSM_STATE, SSM_CONV, SSM_CHUNK = 16, 64, 4, 128, 4, 64
SSM_INNER = SSM_HEADS * SSM_HEADDIM
SSM_HPG = SSM_HEADS // SSM_GROUPS
SSM_CONV_CH = SSM_INNER + 2 * SSM_GROUPS * SSM_STATE
D_FF = ((8 * D_MODEL // 3 + 255) // 256) * 256
IN_SIZES = (GLA_KW, GLA_KW, GLA_VW, GLA_VW, GLA_RANK, NSA_QW, 6 * NSA_KVW, 3 * NSA_HEADS,
            SSM_INNER, SSM_CONV_CH, SSM_HEADS, N_BRANCH * D_MODEL)
SPLIT_POINTS = tuple(int(v) for v in np.cumsum(IN_SIZES)[:-1])

LANES = 128
VMEM_LIMIT = 56 * 1024 * 1024

OFF_MG = 0
OFF_GQ = OFF_MG + N_BRANCH * D_MODEL
OFF_GK = OFF_GQ + GLA_KW
OFF_GV = OFF_GK + GLA_KW
OFF_GR = OFF_GV + GLA_VW
OFF_NQ = OFF_GR + GLA_VW
OFF_SZ = OFF_NQ + NSA_QW
OFF_SX = OFF_SZ + SSM_INNER
OFF_SB = OFF_SX + SSM_INNER
OFF_SC = OFF_SB + SSM_GROUPS * SSM_STATE
OFF_NKV = OFF_SC + SSM_GROUPS * SSM_STATE
OFF_SMALL = OFF_NKV + 6 * NSA_KVW
D_PK = OFF_SMALL + 2 * LANES
SM_GA = 0
SM_NG = 16
SM_DT = 64


def _cparams(sem):
    return pltpu.CompilerParams(dimension_semantics=sem, vmem_limit_bytes=VMEM_LIMIT)


def _split3(x):
    hi = x.astype(BF16)
    r1 = x - hi.astype(F32)
    mid = r1.astype(BF16)
    lo = (r1 - mid.astype(F32)).astype(BF16)
    return hi, mid, lo


def _dot(a, b):
    return jnp.dot(a, b, preferred_element_type=F32)


def _dot_nt(a, b):
    return lax.dot_general(a, b, (((1,), (1,)), ((), ())), preferred_element_type=F32)


def _dot01_left(m01, x):
    hi, mid, lo = _split3(x)
    return _dot(m01, hi) + _dot(m01, mid) + _dot(m01, lo)


def _dot01_right(x, m01):
    hi, mid, lo = _split3(x)
    return _dot(hi, m01) + _dot(mid, m01) + _dot(lo, m01)


def _silu(x):
    return x / (1.0 + jnp.exp(-x))


def _rmsnorm_kernel(x_ref, g_ref, o_ref):
    x = x_ref[...]
    y = x * lax.rsqrt(jnp.mean(x * x, axis=-1, keepdims=True) + EPS)
    o_ref[...] = (y * g_ref[...]).astype(o_ref.dtype)


def rmsnorm_bf16(x, g, tm=1024):
    m, d = x.shape
    tm = min(tm, m)
    return pl.pallas_call(
        _rmsnorm_kernel,
        grid=(m // tm,),
        in_specs=[pl.BlockSpec((tm, d), lambda i: (i, 0)), pl.BlockSpec((1, d), lambda i: (0, 0))],
        out_specs=pl.BlockSpec((tm, d), lambda i: (i, 0)),
        out_shape=jax.ShapeDtypeStruct((m, d), BF16),
        compiler_params=_cparams(("parallel",)),
        name="rmsnorm",
    )(x, g.reshape(1, d))


def _mm_kernel(a_ref, w_ref, o_ref):
    o_ref[...] = _dot(a_ref[...], w_ref[...]).astype(o_ref.dtype)


def matmul(a, w, out_dtype=F32, tm=1024, tn=1024):
    m, k = a.shape
    n = w.shape[1]
    tm = min(tm, m)
    tn = min(tn, n)
    return pl.pallas_call(
        _mm_kernel,
        grid=(m // tm, n // tn),
        in_specs=[pl.BlockSpec((tm, k), lambda i, j: (i, 0)), pl.BlockSpec((k, tn), lambda i, j: (0, j))],
        out_specs=pl.BlockSpec((tm, tn), lambda i, j: (i, j)),
        out_shape=jax.ShapeDtypeStruct((m, n), out_dtype),
        compiler_params=_cparams(("parallel", "parallel")),
        name="in_proj",
    )(a, w)


def _merge_kernel(yg_ref, yn_ref, ys_ref, wg_ref, wn_ref, ws_ref, g0_ref, g1_ref, g2_ref, o_ref):
    def gate(ref):
        return 1.0 / (1.0 + jnp.exp(-ref[...]))

    acc = gate(g0_ref) * _dot(yg_ref[...], wg_ref[...])
    acc += gate(g1_ref) * _dot(yn_ref[...], wn_ref[...])
    acc += gate(g2_ref) * _dot(ys_ref[...], ws_ref[...])
    o_ref[...] = acc.astype(o_ref.dtype)


def merge_branches(y_gla, y_nsa, y_ssm, w_branch, proj, tm=1024, tn=512):
    m = y_gla.shape[0]
    d = w_branch.shape[-1]
    tm = min(tm, m)
    nj = d // tn
    ys = pl.BlockSpec((tm, BRANCH_W), lambda i, j: (i, 0))

    def wspec(b):
        return pl.BlockSpec((None, BRANCH_W, tn), lambda i, j, b=b: (b, 0, j))

    def gspec(b):
        return pl.BlockSpec((tm, tn), lambda i, j, b=b: (i, (OFF_MG + b * D_MODEL) // tn + j))

    return pl.pallas_call(
        _merge_kernel,
        grid=(m // tm, nj),
        in_specs=[ys, ys, ys, wspec(0), wspec(1), wspec(2), gspec(0), gspec(1), gspec(2)],
        out_specs=pl.BlockSpec((tm, tn), lambda i, j: (i, j)),
        out_shape=jax.ShapeDtypeStruct((m, d), BF16),
        compiler_params=_cparams(("parallel", "parallel")),
        name="merge",
    )(y_gla, y_nsa, y_ssm, w_branch, w_branch, w_branch, proj, proj, proj)


def _proj_norm_res_kernel(a_ref, w_ref, x_ref, g_ref, o_ref, acc_ref):
    k = pl.program_id(1)

    @pl.when(k == 0)
    def _():
        acc_ref[...] = jnp.zeros_like(acc_ref)

    acc_ref[...] += _dot(a_ref[...], w_ref[...])

    @pl.when(k == pl.num_programs(1) - 1)
    def _():
        f = acc_ref[...]
        y = f * lax.rsqrt(jnp.mean(f * f, axis=-1, keepdims=True) + EPS)
        o_ref[...] = x_ref[...] + y * g_ref[...]


def proj_norm_residual(a, w, x, g, tm=1024, tk=512):
    m, kk = a.shape
    d = w.shape[1]
    tm = min(tm, m)
    return pl.pallas_call(
        _proj_norm_res_kernel,
        grid=(m // tm, kk // tk),
        in_specs=[pl.BlockSpec((tm, tk), lambda i, k: (i, k)),
                  pl.BlockSpec((tk, d), lambda i, k: (k, 0)),
                  pl.BlockSpec((tm, d), lambda i, k: (i, 0)),
                  pl.BlockSpec((1, d), lambda i, k: (0, 0))],
        out_specs=pl.BlockSpec((tm, d), lambda i, k: (i, 0)),
        out_shape=jax.ShapeDtypeStruct((m, d), F32),
        scratch_shapes=[pltpu.VMEM((tm, d), F32)],
        compiler_params=_cparams(("parallel", "arbitrary")),
        name="proj_norm_res",
    )(a, w, x, g.reshape(1, d))


def _ffn_up_kernel(h_ref, wg_ref, wu_ref, o_ref):
    h = h_ref[...]
    a = _dot(h, wg_ref[...])
    u = _dot(h, wu_ref[...])
    o_ref[...] = (_silu(a) * u).astype(o_ref.dtype)


def ffn_up(h, wg, wu, tm=1024, tn=512):
    m, k = h.shape
    n = wg.shape[1]
    tm = min(tm, m)
    return pl.pallas_call(
        _ffn_up_kernel,
        grid=(m // tm, n // tn),
        in_specs=[pl.BlockSpec((tm, k), lambda i, j: (i, 0)),
                  pl.BlockSpec((k, tn), lambda i, j: (0, j)),
                  pl.BlockSpec((k, tn), lambda i, j: (0, j))],
        out_specs=pl.BlockSpec((tm, tn), lambda i, j: (i, j)),
        out_shape=jax.ShapeDtypeStruct((m, n), BF16),
        compiler_params=_cparams(("parallel", "parallel")),
        name="ffn_up",
    )(h, wg, wu)


def _gla_kernel(q_ref, k_ref, v_ref, r_ref, sm_ref, wa_ref, ba_ref, ng_ref, tri_ref, o_ref,
                st_ref, la_ref, *, nchunk):
    C = GLA_CHUNK

    @pl.when(pl.program_id(2) == 0)
    def _():
        st_ref[...] = jnp.zeros_like(st_ref)

    pre = _dot(sm_ref[...].astype(BF16), wa_ref[...]) + ba_ref[...]
    la_ref[...] = (jnp.minimum(pre, 0.0) - jnp.log1p(jnp.exp(-jnp.abs(pre)))) * (1.0 / GLA_TAU)
    tri = tri_ref[...]
    causal = lax.broadcasted_iota(jnp.int32, (C, C), 0) >= lax.broadcasted_iota(jnp.int32, (C, C), 1)

    def chunk(c, carry):
        rows = pl.ds(pl.multiple_of(c * C, C), C)
        bcum = _dot01_left(tri, la_ref[rows, :])
        b_last = bcum[C - 1:C, :]
        q = q_ref[rows, :] * (GLA_DK ** -0.5)
        k = k_ref[rows, :]
        v = v_ref[rows, :]
        q_dec = (q * jnp.exp(bcum)).astype(BF16)
        k_inv = (k * jnp.exp(-bcum)).astype(BF16)
        k_end = (k * jnp.exp(b_last - bcum)).astype(BF16)
        vb = v.astype(BF16)
        att = jnp.where(causal, _dot_nt(q_dec, k_inv), 0.0)
        st = st_ref[...]
        o = _dot(att.astype(BF16), vb) + _dot_nt(q_dec, st.astype(BF16))
        st_ref[...] = jnp.exp(b_last) * st + _dot(v.T.astype(BF16), k_end)
        y = o * lax.rsqrt(jnp.mean(o * o, axis=-1, keepdims=True) + EPS) * ng_ref[...]
        o_ref[rows, :] = (y * _silu(r_ref[rows, :])).astype(o_ref.dtype)
        return carry

    lax.fori_loop(0, nchunk, chunk, 0)


def gla_mixer(proj3, wa_pad, ba, norm_g, tb=512):
    bsz, T, _ = proj3.shape
    tb = min(tb, T)
    nchunk = tb // GLA_CHUNK
    W = GLA_DK
    tri = jnp.asarray(np.tril(np.ones((GLA_CHUNK, GLA_CHUNK), np.float32)), BF16)

    def col(off):
        return pl.BlockSpec((None, tb, W), lambda b, h, t, o=off // W: (b, t, o + h))

    return pl.pallas_call(
        functools.partial(_gla_kernel, nchunk=nchunk),
        grid=(bsz, GLA_HEADS, T // tb),
        in_specs=[col(OFF_GQ), col(OFF_GK), col(OFF_GV), col(OFF_GR),
                  pl.BlockSpec((None, tb, LANES), lambda b, h, t: (b, t, OFF_SMALL // LANES)),
                  pl.BlockSpec((LANES, W), lambda b, h, t: (0, h)),
                  pl.BlockSpec((1, W), lambda b, h, t: (0, h)),
                  pl.BlockSpec((1, GLA_DV), lambda b, h, t: (0, 0)),
                  pl.BlockSpec((GLA_CHUNK, GLA_CHUNK), lambda b, h, t: (0, 0))],
        out_specs=pl.BlockSpec((None, tb, GLA_DV), lambda b, h, t: (b, t, h)),
        out_shape=jax.ShapeDtypeStruct((bsz, T, GLA_VW), BF16),
        scratch_shapes=[pltpu.VMEM((GLA_DV, GLA_DK), F32), pltpu.VMEM((tb, GLA_DK), F32)],
        compiler_params=_cparams(("parallel", "parallel", "arbitrary")),
        name="gla",
    )(proj3, proj3, proj3, proj3, proj3, wa_pad, ba.reshape(1, GLA_KW), norm_g.reshape(1, GLA_DV), tri)


def _ssd_kernel(z_ref, x_ref, bm_ref, cm_ref, sm_ref, cwx_ref, cwb_ref, cwc_ref, cbx_ref, cbb_ref, cbc_ref,
                dtb_ref, alog_ref, dskip_ref, ng_ref, tri_ref, exp_ref, o_ref,
                st_ref, extx_ref, extb_ref, extc_ref, xa_ref, ba_ref, ca_ref, dt_ref, a_ref, *, tb, nchunk):
    L = SSM_CHUNK
    GW = SSM_HPG * SSM_HEADDIM
    NS = SSM_STATE
    first = pl.program_id(1) == 0

    @pl.when(first)
    def _():
        st_ref[...] = jnp.zeros_like(st_ref)

    def conv_silu(src_ref, ext_ref, w_ref, b_ref, dst_ref):
        @pl.when(first)
        def _():
            ext_ref[0:8, :] = jnp.zeros((8, ext_ref.shape[1]), F32)

        @pl.when(jnp.logical_not(first))
        def _():
            ext_ref[0:8, :] = ext_ref[tb:tb + 8, :]

        ext_ref[8:8 + tb, :] = src_ref[...]
        acc = b_ref[...] + w_ref[SSM_CONV - 1:SSM_CONV, :] * ext_ref[8:8 + tb, :]
        for j in range(1, SSM_CONV):
            acc = acc + w_ref[SSM_CONV - 1 - j:SSM_CONV - j, :] * ext_ref[8 - j:8 - j + tb, :]
        dst_ref[...] = _silu(acc)

    conv_silu(x_ref, extx_ref, cwx_ref, cbx_ref, xa_ref)
    conv_silu(bm_ref, extb_ref, cwb_ref, cbb_ref, ba_ref)
    conv_silu(cm_ref, extc_ref, cwc_ref, cbc_ref, ca_ref)

    v = sm_ref[...] + dtb_ref[...]
    dt = jnp.maximum(v, 0.0) + jnp.log1p(jnp.exp(-jnp.abs(v)))
    dt_ref[...] = dt
    a_ref[...] = dt * (-jnp.exp(alog_ref[...]))

    tri = tri_ref[...]
    row = lax.broadcasted_iota(jnp.int32, (L, GW), 0)
    lane = lax.broadcasted_iota(jnp.int32, (L, GW), 1)
    lane_in = jnp.bitwise_and(lane, SSM_HEADDIM - 1)
    eye_t = (lane_in == row).astype(F32)
    tril_t = lane_in <= row
    bd_mask = (lax.broadcasted_iota(jnp.int32, (GW, GW), 0) // L
               == lax.broadcasted_iota(jnp.int32, (GW, GW), 1) // SSM_HEADDIM).astype(F32)

    def chunk(c, carry):
        rows = pl.ds(pl.multiple_of(c * L, L), L)
        cum128 = _dot01_left(tri, a_ref[rows, :])
        dt_c = dt_ref[rows, :]
        for g in range(SSM_GROUPS):
            cs = slice(g * GW, (g + 1) * GW)
            e_g = exp_ref[:, cs]
            cum_e = _dot01_right(cum128, e_g)
            dt_e = _dot01_right(dt_c, e_g)
            cum_last = cum_e[L - 1:L, :]
            r_row = jnp.sum(cum_e * eye_t, axis=0, keepdims=True)
            decay = jnp.exp(jnp.where(tril_t, cum_e - r_row, -jnp.inf))
            x_g = xa_ref[rows, cs]
            xdt = x_g * dt_e
            b_g = ba_ref[rows, g * NS:(g + 1) * NS]
            c_g = ca_ref[rows, g * NS:(g + 1) * NS].astype(BF16)
            bb = b_g.astype(BF16)
            cb_t = _dot_nt(c_g, jnp.concatenate([bb] * SSM_HPG, axis=0))
            xdt_bd = (jnp.concatenate([xdt] * SSM_HPG, axis=0) * bd_mask).astype(BF16)
            y = _dot((cb_t * decay).astype(BF16), xdt_bd)
            st = st_ref[g]
            y = y + _dot(c_g, st.astype(BF16)) * jnp.exp(cum_e)
            dend = jnp.exp(cum_last - cum_e)
            st_ref[g] = jnp.exp(cum_last) * st + _dot(b_g.T.astype(BF16), (dend * xdt).astype(BF16))
            y = y + x_g * dskip_ref[:, cs]
            y = y * _silu(z_ref[rows, cs])
            y = y * lax.rsqrt(jnp.mean(y * y, axis=-1, keepdims=True) + EPS) * ng_ref[:, cs]
            o_ref[rows, cs] = y.astype(o_ref.dtype)
        return carry

    lax.fori_loop(0, nchunk, chunk, 0)


def _ssd_consts():
    tri = jnp.asarray(np.tril(np.ones((SSM_CHUNK, SSM_CHUNK), np.float32)), BF16)
    e = np.zeros((LANES, SSM_INNER), np.float32)
    for h in range(SSM_HEADS):
        e[SM_DT + h, h * SSM_HEADDIM:(h + 1) * SSM_HEADDIM] = 1.0
    return tri, jnp.asarray(e, BF16)


def _small_row(v, off):
    return jnp.zeros((1, LANES), F32).at[0, off:off + v.shape[0]].set(v.astype(F32))


def ssd_mixer(proj3, conv_w, conv_b, dt_bias, a_log, d_skip, norm_g, tb=256):
    bsz, T, _ = proj3.shape
    tb = min(tb, T)
    nchunk = tb // SSM_CHUNK
    GN = SSM_GROUPS * SSM_STATE
    tri, expand = _ssd_consts()
    cwx, cwb, cwc = conv_w[:, :SSM_INNER], conv_w[:, SSM_INNER:SSM_INNER + GN], conv_w[:, SSM_INNER + GN:]
    cb2 = conv_b.reshape(1, SSM_CONV_CH)
    cbx, cbb, cbc = cb2[:, :SSM_INNER], cb2[:, SSM_INNER:SSM_INNER + GN], cb2[:, SSM_INNER + GN:]
    dtb = _small_row(dt_bias, SM_DT)
    alog = _small_row(a_log, SM_DT)
    dskip = jnp.repeat(d_skip.astype(F32), SSM_HEADDIM).reshape(1, SSM_INNER)

    def col(off, w):
        return pl.BlockSpec((None, tb, w), lambda b, t, o=off // w: (b, t, o))

    def full(shape):
        return pl.BlockSpec(shape, lambda b, t: (0,) * len(shape))

    return pl.pallas_call(
        functools.partial(_ssd_kernel, tb=tb, nchunk=nchunk),
        grid=(bsz, T // tb),
        in_specs=[col(OFF_SZ, SSM_INNER), col(OFF_SX, SSM_INNER), col(OFF_SB, GN), col(OFF_SC, GN),
                  col(OFF_SMALL, LANES),
                  full((SSM_CONV, SSM_INNER)), full((SSM_CONV, GN)), full((SSM_CONV, GN)),
                  full((1, SSM_INNER)), full((1, GN)), full((1, GN)),
                  full((1, LANES)), full((1, LANES)), full((1, SSM_INNER)), full((1, SSM_INNER)),
                  full((SSM_CHUNK, SSM_CHUNK)), full((LANES, SSM_INNER))],
        out_specs=pl.BlockSpec((None, tb, SSM_INNER), lambda b, t: (b, t, 0)),
        out_shape=jax.ShapeDtypeStruct((bsz, T, SSM_INNER), BF16),
        scratch_shapes=[pltpu.VMEM((SSM_GROUPS, SSM_STATE, SSM_HPG * SSM_HEADDIM), F32),
                        pltpu.VMEM((tb + 8, SSM_INNER), F32), pltpu.VMEM((tb + 8, GN), F32),
                        pltpu.VMEM((tb + 8, GN), F32),
                        pltpu.VMEM((tb, SSM_INNER), F32), pltpu.VMEM((tb, GN), F32), pltpu.VMEM((tb, GN), F32),
                        pltpu.VMEM((tb, LANES), F32), pltpu.VMEM((tb, LANES), F32)],
        compiler_params=_cparams(("parallel", "arbitrary")),
        name="ssd",
    )(proj3, proj3, proj3, proj3, proj3, cwx, cwb, cwc, cbx, cbb, cbc, dtb, alog, dskip,
      norm_g.reshape(1, SSM_INNER), tri, expand)


NSA_SLOPES = tuple(float(np.float32(2.0 ** (-8.0 * (i + 1) / NSA_HEADS))) for i in range(NSA_HEADS))
NSA_SCALE = NSA_DH ** -0.5
KV_W = 2 * NSA_KVW
NEG_HUGE = -3.0e38


def _head_cols(hh):
    return slice(hh * NSA_DH, (hh + 1) * NSA_DH)


def _nsa_compress_kernel(k_ref, v_ref, pos_ref, bd1_ref, bd2_ref, o_ref, *, n16):
    S = NSA_CMP_STRIDE
    top = jnp.zeros((n16, KV_W), F32)
    bot = jnp.zeros((n16, KV_W), F32)
    for l in range(S):
        rows = pl.ds(l, n16, stride=S)
        x = jnp.concatenate([k_ref[rows, :], v_ref[rows, :]], axis=-1)
        top += _dot((x + pos_ref[l:l + 1, :]).astype(BF16), bd1_ref[l])
        bot += _dot((x + pos_ref[S + l:S + l + 1, :]).astype(BF16), bd1_ref[S + l])
    pre = top + pltpu.roll(bot, n16 - 1, axis=0)
    out = _dot(_silu(pre).astype(BF16), bd2_ref[...])
    row = lax.broadcasted_iota(jnp.int32, (n16, KV_W), 0)
    o_ref[...] = jnp.where(row < n16 - 1, out, 0.0)


def nsa_compress(proj3, cmp_pos, cmp_w1, cmp_w2):
    bsz, T, _ = proj3.shape
    n16 = T // NSA_CMP_STRIDE
    sel = np.array([0, 0, 1, 1])
    eye = jnp.eye(4, dtype=F32)
    w1r = cmp_w1.reshape(2, NSA_CMP_LEN, NSA_DH, NSA_DH)[sel]
    bd1 = jnp.einsum('ab,alde->ladbe', eye, w1r).reshape(NSA_CMP_LEN, KV_W, KV_W).astype(BF16)
    bd2 = jnp.einsum('ab,ade->adbe', eye, cmp_w2[sel]).reshape(KV_W, KV_W).astype(BF16)
    pos = jnp.concatenate([cmp_pos[0], cmp_pos[0], cmp_pos[1], cmp_pos[1]], axis=-1)
    return pl.pallas_call(
        functools.partial(_nsa_compress_kernel, n16=n16),
        grid=(bsz,),
        in_specs=[pl.BlockSpec((None, T, NSA_KVW), lambda b: (b, 0, OFF_NKV // NSA_KVW)),
                  pl.BlockSpec((None, T, NSA_KVW), lambda b: (b, 0, OFF_NKV // NSA_KVW + 1)),
                  pl.BlockSpec((NSA_CMP_LEN, KV_W), lambda b: (0, 0)),
                  pl.BlockSpec((NSA_CMP_LEN, KV_W, KV_W), lambda b: (0, 0, 0)),
                  pl.BlockSpec((KV_W, KV_W), lambda b: (0, 0))],
        out_specs=pl.BlockSpec((None, n16, KV_W), lambda b: (b, 0, 0)),
        out_shape=jax.ShapeDtypeStruct((bsz, n16, KV_W), F32),
        compiler_params=_cparams(("parallel",)),
        name="nsa_compress",
    )(proj3, proj3, pos, bd1, bd2)


def _nsa_cmp_topk_kernel(q_ref, kcv_ref, sm_ref, ov_ref, o_ref, sel_ref, act_ref, *, tq, n16, n_sel, n_top):
    t0 = pl.program_id(1) * tq
    tpos = t0 + lax.broadcasted_iota(jnp.int32, (tq, 1), 0)
    cmp_end = lax.broadcasted_iota(jnp.int32, (1, n16), 1) * NSA_CMP_STRIDE + (NSA_CMP_LEN - 1)
    dist = (tpos - cmp_end).astype(F32)
    valid = dist >= 0.0
    gates = 1.0 / (1.0 + jnp.exp(-sm_ref[...]))
    blk = lax.broadcasted_iota(jnp.int32, (1, n_sel), 1)
    blk_f = blk.astype(F32)
    cur = tpos // NSA_SEL_BLOCK
    forced = (blk == 0) | (blk == cur) | (blk == cur - 1)
    future = blk * NSA_SEL_BLOCK > tpos
    for g in range(NSA_GROUPS):
        kc = kcv_ref[:, g * NSA_DH:(g + 1) * NSA_DH].astype(BF16)
        vc = kcv_ref[:, NSA_KVW + g * NSA_DH:NSA_KVW + (g + 1) * NSA_DH].astype(BF16)
        psum = jnp.zeros((tq, n16), F32)
        for h in range(NSA_HPG):
            hh = g * NSA_HPG + h
            q = (q_ref[:, _head_cols(hh)] * NSA_SCALE).astype(BF16)
            s = jnp.where(valid, _dot_nt(q, kc) - NSA_SLOPES[hh] * dist, -BIG)
            e = jnp.where(valid, jnp.exp(s - jnp.max(s, axis=-1, keepdims=True)), 0.0)
            den = jnp.sum(e, axis=-1, keepdims=True)
            p = e * (1.0 / jnp.where(den > 0.0, den, 1.0))
            psum += p
            gate = gates[:, SM_NG + hh:SM_NG + hh + 1]
            o_ref[:, _head_cols(hh)] = gate * _dot(p.astype(BF16), vc)
        imp = _dot01_right(psum, ov_ref[...])
        work = jnp.where(forced, BIG, jnp.where(future, -BIG, imp))
        sel = jnp.zeros((tq, n_sel), F32)
        for _ in range(n_top):
            top = jnp.max(work, axis=-1, keepdims=True)
            idx = jnp.min(jnp.where(work == top, blk_f, float(n_sel)), axis=-1, keepdims=True)
            pick = blk_f == idx
            sel = jnp.where(pick, 1.0, sel)
            work = jnp.where(pick, NEG_HUGE, work)
        sel_ref[:, g * n_sel:(g + 1) * n_sel] = sel.astype(sel_ref.dtype)
        union = jnp.max(sel, axis=0, keepdims=True)
        act_ref[:, g * n_sel:(g + 1) * n_sel] = jnp.where(blk * NSA_SEL_BLOCK < t0, union, 0.0)


def _nsa_overlap(n16, n_sel):
    n_cmp = n16 - 1
    tok = (np.arange(n_cmp) * NSA_CMP_STRIDE)[:, None] + np.arange(NSA_CMP_LEN)[None, :]
    ov = np.zeros((n16, n_sel), np.float32)
    np.add.at(ov, (np.repeat(np.arange(n_cmp), NSA_CMP_LEN), (tok // NSA_SEL_BLOCK).ravel()), 1.0 / NSA_CMP_LEN)
    return jnp.asarray(ov, BF16)


def nsa_cmp_topk(proj3, kcv, tq=256):
    bsz, T, _ = proj3.shape
    tq = min(tq, T)
    n16 = T // NSA_CMP_STRIDE
    n_sel = T // NSA_SEL_BLOCK
    n_top = min(NSA_TOPN, n_sel)
    return pl.pallas_call(
        functools.partial(_nsa_cmp_topk_kernel, tq=tq, n16=n16, n_sel=n_sel, n_top=n_top),
        grid=(bsz, T // tq),
        in_specs=[pl.BlockSpec((None, tq, NSA_QW), lambda b, i: (b, i, OFF_NQ // NSA_QW)),
                  pl.BlockSpec((None, n16, KV_W), lambda b, i: (b, 0, 0)),
                  pl.BlockSpec((None, tq, LANES), lambda b, i: (b, i, OFF_SMALL // LANES)),
                  pl.BlockSpec((n16, n_sel), lambda b, i: (0, 0))],
        out_specs=[pl.BlockSpec((None, tq, NSA_QW), lambda b, i: (b, i, 0)),
                   pl.BlockSpec((None, tq, NSA_GROUPS * n_sel), lambda b, i: (b, i, 0)),
                   pl.BlockSpec((None, None, 1, NSA_GROUPS * n_sel), lambda b, i: (b, i, 0, 0))],
        out_shape=[jax.ShapeDtypeStruct((bsz, T, NSA_QW), F32),
                   jax.ShapeDtypeStruct((bsz, T, NSA_GROUPS * n_sel), BF16),
                   jax.ShapeDtypeStruct((bsz, T // tq, 1, NSA_GROUPS * n_sel), F32)],
        compiler_params=_cparams(("parallel", "parallel")),
        name="nsa_cmp_topk",
    )(proj3, kcv, proj3, _nsa_overlap(n16, n_sel))


def _nsa_window_kernel(q_ref, kv0_ref, kv1_ref, kv2_ref, sm_ref, prev_ref, o_ref, *, tq):
    nb = NSA_WINDOW // tq + 1
    i = pl.program_id(1)
    tpos = i * tq + lax.broadcasted_iota(jnp.int32, (tq, 1), 0)
    kpos = (i - (nb - 1)) * tq + lax.broadcasted_iota(jnp.int32, (1, nb * tq), 1)
    d = tpos - kpos
    valid = (d >= 0) & (d < NSA_WINDOW) & (kpos >= 0)
    d_f = d.astype(F32)
    gates = 1.0 / (1.0 + jnp.exp(-sm_ref[...]))
    kv = jnp.concatenate([kv0_ref[...], kv1_ref[...], kv2_ref[...]], axis=0).astype(BF16)
    for g in range(NSA_GROUPS):
        kw = kv[:, g * NSA_DH:(g + 1) * NSA_DH]
        vw = kv[:, NSA_KVW + g * NSA_DH:NSA_KVW + (g + 1) * NSA_DH]
        for h in range(NSA_HPG):
            hh = g * NSA_HPG + h
            q = (q_ref[:, _head_cols(hh)] * NSA_SCALE).astype(BF16)
            s = jnp.where(valid, _dot_nt(q, kw) - NSA_SLOPES[hh] * d_f, -BIG)
            e = jnp.where(valid, jnp.exp(s - jnp.max(s, axis=-1, keepdims=True)), 0.0)
            p = e * (1.0 / jnp.sum(e, axis=-1, keepdims=True))
            gate = gates[:, SM_NG + 2 * NSA_HEADS + hh:SM_NG + 2 * NSA_HEADS + hh + 1]
            o_ref[:, _head_cols(hh)] = prev_ref[:, _head_cols(hh)] + gate * _dot(p.astype(BF16), vw)


def nsa_window(proj3, prev, tq=256):
    bsz, T, _ = proj3.shape
    tq = min(tq, T)
    assert NSA_WINDOW % tq == 0 and NSA_WINDOW // tq == 2
    cw = (OFF_NKV + 2 * KV_W) // KV_W

    def kvspec(back):
        return pl.BlockSpec((None, tq, KV_W), lambda b, i, back=back: (b, jnp.maximum(i - back, 0), cw))

    return pl.pallas_call(
        functools.partial(_nsa_window_kernel, tq=tq),
        grid=(bsz, T // tq),
        in_specs=[pl.BlockSpec((None, tq, NSA_QW), lambda b, i: (b, i, OFF_NQ // NSA_QW)),
                  kvspec(2), kvspec(1), kvspec(0),
                  pl.BlockSpec((None, tq, LANES), lambda b, i: (b, i, OFF_SMALL // LANES)),
                  pl.BlockSpec((None, tq, NSA_QW), lambda b, i: (b, i, 0))],
        out_specs=pl.BlockSpec((None, tq, NSA_QW), lambda b, i: (b, i, 0)),
        out_shape=jax.ShapeDtypeStruct((bsz, T, NSA_QW), F32),
        compiler_params=_cparams(("parallel", "parallel")),
        name="nsa_window",
    )(proj3, proj3, proj3, proj3, proj3, prev)


MASK_BIG = 1e30
SEL_STEP = 8
POS_HI = 128
EXTRA_W = NSA_DH


def _bf16_pieces(x):
    x = np.float32(x)
    out = []
    for _ in range(3):
        p = np.float32(np.asarray(x, dtype=jnp.bfloat16))
        out.append(p)
        x = np.float32(x - p)
    return out


def _nsa_select_consts(T, pad):
    n_sel = T // NSA_SEL_BLOCK
    pos = np.arange(T + pad)
    real = pos < T
    kc = np.zeros((T + pad, n_sel + EXTRA_W), np.float32)
    kc[pos[real], pos[real] // NSA_SEL_BLOCK] = 1.0
    for c in range(3):
        kc[real, n_sel + c] = (pos[real] // POS_HI) * POS_HI
        kc[real, n_sel + 3 + c] = pos[real] % POS_HI
    kc[~real, n_sel + 6] = 1.0
    vc = np.zeros((T + pad, NSA_DH), np.float32)
    vc[real, 0] = 1.0
    qc = np.zeros((NSA_HEADS, EXTRA_W), np.float32)
    for hh in range(NSA_HEADS):
        qc[hh, 0:3] = qc[hh, 3:6] = _bf16_pieces(NSA_SLOPES[hh])
        qc[hh, 6] = -MASK_BIG
    return jnp.asarray(kc, BF16), jnp.asarray(vc, BF16), jnp.asarray(qc, BF16)


def _nsa_kprep_kernel(kv_ref, kc_ref, vc_ref, ka_ref, va_ref):
    real = pl.program_id(1) < pl.num_programs(1) - 1
    kv = (kv_ref[...] * jnp.where(real, 1.0, 0.0)).astype(BF16)
    for g in range(NSA_GROUPS):
        ka_ref[g] = jnp.concatenate([kv[:, g * NSA_DH:(g + 1) * NSA_DH], kc_ref[...]], axis=-1)
        va_ref[g] = jnp.concatenate([kv[:, NSA_KVW + g * NSA_DH:NSA_KVW + (g + 1) * NSA_DH], vc_ref[...]], axis=-1)


def nsa_kprep(proj3, kc, vc, tb):
    bsz, T, _ = proj3.shape
    nt = T // tb
    wk = NSA_DH + kc.shape[1]
    cs = (OFF_NKV + KV_W) // KV_W
    return pl.pallas_call(
        _nsa_kprep_kernel,
        grid=(bsz, nt + 1),
        in_specs=[pl.BlockSpec((None, tb, KV_W), lambda b, t: (b, jnp.minimum(t, nt - 1), cs)),
                  pl.BlockSpec((tb, kc.shape[1]), lambda b, t: (t, 0)),
                  pl.BlockSpec((tb, NSA_DH), lambda b, t: (t, 0))],
        out_specs=[pl.BlockSpec((None, NSA_GROUPS, tb, wk), lambda b, t: (b, 0, t, 0)),
                   pl.BlockSpec((None, NSA_GROUPS, tb, 2 * NSA_DH), lambda b, t: (b, 0, t, 0))],
        out_shape=[jax.ShapeDtypeStruct((bsz, NSA_GROUPS, T + tb, wk), BF16),
                   jax.ShapeDtypeStruct((bsz, NSA_GROUPS, T + tb, 2 * NSA_DH), BF16)],
        compiler_params=_cparams(("parallel", "parallel")),
        name="nsa_kprep",
    )(proj3, kc, vc)


def _nsa_select_kernel(ids_ref, cnt_ref, q_ref, ka_ref, va_ref, sel_ref, qc_ref, sm_ref, prev_ref, o_ref,
                       qa_ref, kt_ref, vt_ref, m_ref, acc_ref, *, tq, n_sel, lmax):
    b = pl.program_id(0)
    i = pl.program_id(1)
    B = NSA_SEL_BLOCK
    tpos = i * tq + lax.broadcasted_iota(jnp.int32, (tq, 1), 0)
    kpos = i * tq + lax.broadcasted_iota(jnp.int32, (1, tq), 1)
    causal_bias = jnp.where(kpos <= tpos, 0.0, -BIG)
    causal_bias = jnp.concatenate([causal_bias] * NSA_HPG, axis=0)
    gates = 1.0 / (1.0 + jnp.exp(-sm_ref[...]))
    own = pl.ds(pl.multiple_of(i * tq, tq), tq)
    for g in range(NSA_GROUPS):
        mcols = ((sel_ref[:, g * n_sel:(g + 1) * n_sel].astype(F32) - 1.0) * MASK_BIG).astype(BF16)
        for h in range(NSA_HPG):
            hh = g * NSA_HPG + h
            qh = (q_ref[:, _head_cols(hh)] * NSA_SCALE).astype(BF16)
            qx = jnp.broadcast_to(qc_ref[hh:hh + 1, :], (tq, EXTRA_W))
            qa_ref[h * tq:(h + 1) * tq, :] = jnp.concatenate([qh, mcols, qx], axis=-1)
        qa = qa_ref[...]
        s = _dot_nt(qa, ka_ref[g, own, :]) + causal_bias
        m0 = jnp.max(s, axis=-1, keepdims=True)
        m_ref[...] = m0
        acc_ref[...] = _dot(jnp.exp(s - m0).astype(BF16), va_ref[g, own, :])
        slot = (b * pl.num_programs(1) + i) * NSA_GROUPS + g

        def step(st, carry):
            for u in range(SEL_STEP):
                r0 = pl.multiple_of(ids_ref[slot * lmax + st * SEL_STEP + u] * B, B)
                kt_ref[u * B:(u + 1) * B, :] = ka_ref[g, pl.ds(r0, B), :]
                vt_ref[u * B:(u + 1) * B, :] = va_ref[g, pl.ds(r0, B), :]
            s = _dot_nt(qa, kt_ref[...])
            m_old = m_ref[...]
            m_new = jnp.maximum(m_old, jnp.max(s, axis=-1, keepdims=True))
            p = jnp.exp(s - m_new).astype(BF16)
            acc_ref[...] = jnp.exp(m_old - m_new) * acc_ref[...] + _dot(p, vt_ref[...])
            m_ref[...] = m_new
            return carry

        lax.fori_loop(0, (cnt_ref[slot] + SEL_STEP - 1) // SEL_STEP, step, 0)
        acc = acc_ref[...]
        o = acc[:, :NSA_DH] * (1.0 / acc[:, NSA_DH:NSA_DH + 1])
        for h in range(NSA_HPG):
            hh = g * NSA_HPG + h
            gate = gates[:, SM_NG + NSA_HEADS + hh:SM_NG + NSA_HEADS + hh + 1]
            o_ref[:, _head_cols(hh)] = (prev_ref[:, _head_cols(hh)] + gate * o[h * tq:(h + 1) * tq, :]
                                        ).astype(o_ref.dtype)


def nsa_select(proj3, sel, act, prev, tq=256):
    bsz, T, _ = proj3.shape
    tq = min(tq, T)
    nt = T // tq
    n_sel = T // NSA_SEL_BLOCK
    lmax = -(-n_sel // SEL_STEP) * SEL_STEP
    kc, vc, qc = _nsa_select_consts(T, tq)
    kaug, vaug = nsa_kprep(proj3, kc, vc, tq)
    wk = kaug.shape[-1]
    blk = jnp.arange(n_sel, dtype=jnp.int32)
    on = act.reshape(bsz, nt, NSA_GROUPS, n_sel) > 0.0
    order = jnp.sort(jnp.where(on, blk, n_sel + blk), axis=-1)
    ids = jnp.where(order < n_sel, order, n_sel)
    ids = jnp.pad(ids, ((0, 0), (0, 0), (0, 0), (0, lmax - n_sel)), constant_values=n_sel).reshape(-1)
    cnt = jnp.sum(on, axis=-1, dtype=jnp.int32).reshape(-1)
    rows = NSA_HPG * tq
    grid_spec = pltpu.PrefetchScalarGridSpec(
        num_scalar_prefetch=2,
        grid=(bsz, nt),
        in_specs=[pl.BlockSpec((None, tq, NSA_QW), lambda b, i, *_: (b, i, OFF_NQ // NSA_QW)),
                  pl.BlockSpec((None, NSA_GROUPS, T + tq, wk), lambda b, i, *_: (b, 0, 0, 0)),
                  pl.BlockSpec((None, NSA_GROUPS, T + tq, 2 * NSA_DH), lambda b, i, *_: (b, 0, 0, 0)),
                  pl.BlockSpec((None, tq, NSA_GROUPS * n_sel), lambda b, i, *_: (b, i, 0)),
                  pl.BlockSpec((NSA_HEADS, EXTRA_W), lambda b, i, *_: (0, 0)),
                  pl.BlockSpec((None, tq, LANES), lambda b, i, *_: (b, i, OFF_SMALL // LANES)),
                  pl.BlockSpec((None, tq, NSA_QW), lambda b, i, *_: (b, i, 0))],
        out_specs=pl.BlockSpec((None, tq, NSA_QW), lambda b, i, *_: (b, i, 0)),
        scratch_shapes=[pltpu.VMEM((rows, wk), BF16),
                        pltpu.VMEM((SEL_STEP * NSA_SEL_BLOCK, wk), BF16),
                        pltpu.VMEM((SEL_STEP * NSA_SEL_BLOCK, 2 * NSA_DH), BF16),
                        pltpu.VMEM((rows, 1), F32), pltpu.VMEM((rows, 2 * NSA_DH), F32)])
    return pl.pallas_call(
        functools.partial(_nsa_select_kernel, tq=tq, n_sel=n_sel, lmax=lmax),
        grid_spec=grid_spec,
        out_shape=jax.ShapeDtypeStruct((bsz, T, NSA_QW), BF16),
        compiler_params=_cparams(("parallel", "arbitrary")),
        name="nsa_select",
    )(ids, cnt, proj3, kaug, vaug, sel, qc, proj3, prev)


def nsa_mixer(proj3, cmp_pos, cmp_w1, cmp_w2):
    kcv = nsa_compress(proj3, cmp_pos, cmp_w1, cmp_w2)
    o_cmp, sel, act = nsa_cmp_topk(proj3, kcv)
    o_cw = nsa_window(proj3, o_cmp)
    return nsa_select(proj3, sel, act, o_cw)


def _pack_w_in(w_in):
    (g_q, g_k, g_v, g_r, g_a, n_q, n_kv, n_g, s_z, s_xbc, s_dt, m_g) = jnp.split(w_in, SPLIT_POINTS, axis=-1)
    pad = jnp.zeros(w_in.shape[:-1] + (D_PK - OFF_SMALL - GLA_RANK - 3 * NSA_HEADS - SSM_HEADS,), w_in.dtype)
    packed = jnp.concatenate([m_g, g_q, g_k, g_v, g_r, n_q, s_z, s_xbc, n_kv, g_a, n_g, s_dt, pad], axis=-1)
    return packed.astype(BF16)


def kernel(x, w_in, gla_a2, gla_a_bias, gla_norm, nsa_cmp_pos, nsa_cmp_w1, nsa_cmp_w2, ssm_conv_w, ssm_conv_b,
           ssm_dt_bias, ssm_a_log, ssm_d, ssm_norm, w_branch, w_out, norm_pre_mix, norm_post_mix, norm_pre_ffn,
           norm_post_ffn, w_ffn_gate, w_ffn_up, w_ffn_down):
    bsz, T, D = x.shape
    depth = w_in.shape[0]
    n = bsz * T
    w_in_pk = _pack_w_in(w_in)
    wa_pad = jnp.zeros((depth, LANES, GLA_KW), F32).at[:, SM_GA:SM_GA + GLA_RANK].set(gla_a2).astype(BF16)
    w_branch_b = w_branch.astype(BF16)
    w_out_b = w_out.astype(BF16)
    w_gate_b = w_ffn_gate.astype(BF16)
    w_up_b = w_ffn_up.astype(BF16)
    w_down_b = w_ffn_down.astype(BF16)
    xf = x.reshape(n, D)
    for l in range(depth):
        h = rmsnorm_bf16(xf, norm_pre_mix[l])
        proj = matmul(h, w_in_pk[l])
        proj3 = proj.reshape(bsz, T, D_PK)
        y_gla = gla_mixer(proj3, wa_pad[l], gla_a_bias[l], gla_norm[l])
        y_nsa = nsa_mixer(proj3, nsa_cmp_pos[l], nsa_cmp_w1[l], nsa_cmp_w2[l])
        y_ssm = ssd_mixer(proj3, ssm_conv_w[l], ssm_conv_b[l], ssm_dt_bias[l], ssm_a_log[l], ssm_d[l], ssm_norm[l])
        merged = merge_branches(y_gla.reshape(n, BRANCH_W), y_nsa.reshape(n, BRANCH_W), y_ssm.reshape(n, BRANCH_W),
                                w_branch_b[l], proj)
        xf = proj_norm_residual(merged, w_out_b[l], xf, norm_post_mix[l])
        h = rmsnorm_bf16(xf, norm_pre_ffn[l])
        act = ffn_up(h, w_gate_b[l], w_up_b[l])
        xf = proj_norm_residual(act, w_down_b[l], xf, norm_post_ffn[l])
    return xf.reshape(bsz, T, D)
```

```python
import functools

import numpy as np
import jax
import jax.numpy as jnp
from jax import lax
from jax.experimental import pallas as pl
from jax.experimental.pallas import tpu as pltpu

F32 = jnp.float32
BF16 = jnp.bfloat16

D_MODEL = 2048
EPS = 1e-6
N_BRANCH = 3
BRANCH_W = 1024
GLA_HEADS, GLA_DK, GLA_DV, GLA_RANK, GLA_TAU, GLA_CHUNK = 4, 256, 256, 16, 16.0, 64
GLA_KW = GLA_HEADS * GLA_DK
GLA_VW = GLA_HEADS * GLA_DV
NSA_HEADS, NSA_GROUPS, NSA_DH = 16, 2, 64
NSA_HPG = NSA_HEADS // NSA_GROUPS
NSA_QW = NSA_HEADS * NSA_DH
NSA_KVW = NSA_GROUPS * NSA_DH
NSA_CMP_LEN, NSA_CMP_STRIDE, NSA_SEL_BLOCK, NSA_TOPN, NSA_WINDOW = 32, 16, 64, 16, 512
BIG = 1e30
SSM_HEADS, SSM_HEADDIM, SSM_GROUPS, SSM_STATE, SSM_CONV, SSM_CHUNK = 16, 64, 4, 128, 4, 64
SSM_INNER = SSM_HEADS * SSM_HEADDIM
SSM_HPG = SSM_HEADS // SSM_GROUPS
SSM_CONV_CH = SSM_INNER + 2 * SSM_GROUPS * SSM_STATE
D_FF = ((8 * D_MODEL // 3 + 255) // 256) * 256
IN_SIZES = (GLA_KW, GLA_KW, GLA_VW, GLA_VW, GLA_RANK, NSA_QW, 6 * NSA_KVW, 3 * NSA_HEADS,
            SSM_INNER, SSM_CONV_CH, SSM_HEADS, N_BRANCH * D_MODEL)
SPLIT_POINTS = tuple(int(v) for v in np.cumsum(IN_SIZES)[:-1])

LANES = 128
VMEM_LIMIT = 56 * 1024 * 1024

OFF_MG = 0
OFF_GQ = OFF_MG + N_BRANCH * D_MODEL
OFF_GK = OFF_GQ + GLA_KW
OFF_GV = OFF_GK + GLA_KW
OFF_GR = OFF_GV + GLA_VW
OFF_NQ = OFF_GR + GLA_VW
OFF_SZ = OFF_NQ + NSA_QW
OFF_SX = OFF_SZ + SSM_INNER
OFF_SB = OFF_SX + SSM_INNER
OFF_SC = OFF_SB + SSM_GROUPS * SSM_STATE
OFF_NKV = OFF_SC + SSM_GROUPS * SSM_STATE
OFF_SMALL = OFF_NKV + 6 * NSA_KVW
D_PK = OFF_SMALL + 2 * LANES
SM_GA = 0
SM_NG = 16
SM_DT = 64


def _cparams(sem):
    return pltpu.CompilerParams(dimension_semantics=sem, vmem_limit_bytes=VMEM_LIMIT)


def _split3(x):
    hi = x.astype(BF16)
    r1 = x - hi.astype(F32)
    mid = r1.astype(BF16)
    lo = (r1 - mid.astype(F32)).astype(BF16)
    return hi, mid, lo


def _dot(a, b):
    return jnp.dot(a, b, preferred_element_type=F32)


def _dot_nt(a, b):
    return lax.dot_general(a, b, (((1,), (1,)), ((), ())), preferred_element_type=F32)


def _dot01_left(m01, x):
    hi, mid, lo = _split3(x)
    return _dot(m01, hi) + _dot(m01, mid) + _dot(m01, lo)


def _dot01_right(x, m01):
    hi, mid, lo = _split3(x)
    return _dot(hi, m01) + _dot(mid, m01) + _dot(lo, m01)


def _silu(x):
    return x / (1.0 + jnp.exp(-x))


def _norm_rows(x_ref, g_ref, h_ref):
    @pl.when(pl.program_id(1) == 0)
    def _():
        x = x_ref[...]
        y = x * lax.rsqrt(jnp.mean(x * x, axis=-1, keepdims=True) + EPS)
        h_ref[...] = (y * g_ref[...]).astype(h_ref.dtype)


def _norm_mm_kernel(x_ref, g_ref, w_ref, o_ref, h_ref):
    _norm_rows(x_ref, g_ref, h_ref)
    o_ref[...] = _dot(h_ref[...], w_ref[...]).astype(o_ref.dtype)


def norm_matmul(x, g, w, out_dtype=F32, tm=1024, tn=1024):
    m, k = x.shape
    n = w.shape[1]
    tm = min(tm, m)
    tn = min(tn, n)
    return pl.pallas_call(
        _norm_mm_kernel,
        grid=(m // tm, n // tn),
        in_specs=[pl.BlockSpec((tm, k), lambda i, j: (i, 0)), pl.BlockSpec((1, k), lambda i, j: (0, 0)),
                  pl.BlockSpec((k, tn), lambda i, j: (0, j))],
        out_specs=pl.BlockSpec((tm, tn), lambda i, j: (i, j)),
        out_shape=jax.ShapeDtypeStruct((m, n), out_dtype),
        scratch_shapes=[pltpu.VMEM((tm, k), BF16)],
        compiler_params=_cparams(("parallel", "arbitrary")),
        name="in_proj",
    )(x, g.reshape(1, k), w)


def _merge_kernel(yg_ref, yn_ref, ys_ref, wg_ref, wn_ref, ws_ref, g0_ref, g1_ref, g2_ref, o_ref):
    def gate(ref):
        return 1.0 / (1.0 + jnp.exp(-ref[...]))

    acc = gate(g0_ref) * _dot(yg_ref[...], wg_ref[...])
    acc += gate(g1_ref) * _dot(yn_ref[...], wn_ref[...])
    acc += gate(g2_ref) * _dot(ys_ref[...], ws_ref[...])
    o_ref[...] = acc.astype(o_ref.dtype)


def merge_branches(y_gla, y_nsa, y_ssm, w_branch, proj, tm=1024, tn=512):
    m = y_gla.shape[0]
    d = w_branch.shape[-1]
    tm = min(tm, m)
    nj = d // tn
    ys = pl.BlockSpec((tm, BRANCH_W), lambda i, j: (i, 0))

    def wspec(b):
        return pl.BlockSpec((None, BRANCH_W, tn), lambda i, j, b=b: (b, 0, j))

    def gspec(b):
        return pl.BlockSpec((tm, tn), lambda i, j, b=b: (i, (OFF_MG + b * D_MODEL) // tn + j))

    return pl.pallas_call(
        _merge_kernel,
        grid=(m // tm, nj),
        in_specs=[ys, ys, ys, wspec(0), wspec(1), wspec(2), gspec(0), gspec(1), gspec(2)],
        out_specs=pl.BlockSpec((tm, tn), lambda i, j: (i, j)),
        out_shape=jax.ShapeDtypeStruct((m, d), BF16),
        compiler_params=_cparams(("parallel", "parallel")),
        name="merge",
    )(y_gla, y_nsa, y_ssm, w_branch, w_branch, w_branch, proj, proj, proj)


def _proj_norm_res_kernel(a_ref, w_ref, x_ref, g_ref, o_ref, acc_ref):
    k = pl.program_id(1)

    @pl.when(k == 0)
    def _():
        acc_ref[...] = jnp.zeros_like(acc_ref)

    acc_ref[...] += _dot(a_ref[...], w_ref[...])

    @pl.when(k == pl.num_programs(1) - 1)
    def _():
        f = acc_ref[...]
        y = f * lax.rsqrt(jnp.mean(f * f, axis=-1, keepdims=True) + EPS)
        o_ref[...] = x_ref[...] + y * g_ref[...]


def proj_norm_residual(a, w, x, g, tm=1024, tk=512):
    m, kk = a.shape
    d = w.shape[1]
    tm = min(tm, m)
    return pl.pallas_call(
        _proj_norm_res_kernel,
        grid=(m // tm, kk // tk),
        in_specs=[pl.BlockSpec((tm, tk), lambda i, k: (i, k)),
                  pl.BlockSpec((tk, d), lambda i, k: (k, 0)),
                  pl.BlockSpec((tm, d), lambda i, k: (i, 0)),
                  pl.BlockSpec((1, d), lambda i, k: (0, 0))],
        out_specs=pl.BlockSpec((tm, d), lambda i, k: (i, 0)),
        out_shape=jax.ShapeDtypeStruct((m, d), F32),
        scratch_shapes=[pltpu.VMEM((tm, d), F32)],
        compiler_params=_cparams(("parallel", "arbitrary")),
        name="proj_norm_res",
    )(a, w, x, g.reshape(1, d))


def _ffn_up_kernel(x_ref, g_ref, wg_ref, wu_ref, o_ref, h_ref):
    _norm_rows(x_ref, g_ref, h_ref)
    h = h_ref[...]
    a = _dot(h, wg_ref[...])
    u = _dot(h, wu_ref[...])
    o_ref[...] = (_silu(a) * u).astype(o_ref.dtype)


def ffn_up(x, g, wg, wu, tm=1024, tn=512):
    m, k = x.shape
    n = wg.shape[1]
    tm = min(tm, m)
    return pl.pallas_call(
        _ffn_up_kernel,
        grid=(m // tm, n // tn),
        in_specs=[pl.BlockSpec((tm, k), lambda i, j: (i, 0)),
                  pl.BlockSpec((1, k), lambda i, j: (0, 0)),
                  pl.BlockSpec((k, tn), lambda i, j: (0, j)),
                  pl.BlockSpec((k, tn), lambda i, j: (0, j))],
        out_specs=pl.BlockSpec((tm, tn), lambda i, j: (i, j)),
        out_shape=jax.ShapeDtypeStruct((m, n), BF16),
        scratch_shapes=[pltpu.VMEM((tm, k), BF16)],
        compiler_params=_cparams(("parallel", "arbitrary")),
        name="ffn_up",
    )(x, g.reshape(1, k), wg, wu)


def _gla_kernel(q_ref, k_ref, v_ref, r_ref, sm_ref, wa_ref, ba_ref, ng_ref, tri_ref, o_ref,
                st_ref, la_ref, *, nchunk):
    C = GLA_CHUNK

    @pl.when(pl.program_id(1) == 0)
    def _():
        st_ref[...] = jnp.zeros_like(st_ref)

    pre = _dot(sm_ref[...].astype(BF16), wa_ref[...]) + ba_ref[...]
    la_ref[...] = (jnp.minimum(pre, 0.0) - jnp.log1p(jnp.exp(-jnp.abs(pre)))) * (1.0 / GLA_TAU)
    tri = tri_ref[...]
    causal = lax.broadcasted_iota(jnp.int32, (C, C), 0) >= lax.broadcasted_iota(jnp.int32, (C, C), 1)

    def chunk(c, carry):
        rows = pl.ds(pl.multiple_of(c * C, C), C)
        for h in range(GLA_HEADS):
            ck = slice(h * GLA_DK, (h + 1) * GLA_DK)
            cv = slice(h * GLA_DV, (h + 1) * GLA_DV)
            bcum = _dot01_left(tri, la_ref[rows, ck])
            b_last = bcum[C - 1:C, :]
            q = q_ref[rows, ck] * (GLA_DK ** -0.5)
            k = k_ref[rows, ck]
            v = v_ref[rows, cv]
            q_dec = (q * jnp.exp(bcum)).astype(BF16)
            k_inv = (k * jnp.exp(-bcum)).astype(BF16)
            k_end = (k * jnp.exp(b_last - bcum)).astype(BF16)
            att = jnp.where(causal, _dot_nt(q_dec, k_inv), 0.0)
            st = st_ref[h]
            o = _dot(att.astype(BF16), v.astype(BF16)) + _dot_nt(q_dec, st.astype(BF16))
            st_ref[h] = jnp.exp(b_last) * st + _dot(v.T.astype(BF16), k_end)
            y = o * lax.rsqrt(jnp.mean(o * o, axis=-1, keepdims=True) + EPS) * ng_ref[...]
            o_ref[rows, cv] = (y * _silu(r_ref[rows, cv])).astype(o_ref.dtype)
        return carry

    lax.fori_loop(0, nchunk, chunk, 0)


def gla_mixer(proj3, wa_pad, ba, norm_g, tb=256):
    bsz, T, _ = proj3.shape
    tb = min(tb, T)
    nchunk = tb // GLA_CHUNK
    tri = jnp.asarray(np.tril(np.ones((GLA_CHUNK, GLA_CHUNK), np.float32)), BF16)

    def col(off, w):
        return pl.BlockSpec((None, tb, w), lambda b, t, o=off // w: (b, t, o))

    def full(shape):
        return pl.BlockSpec(shape, lambda b, t: (0,) * len(shape))

    return pl.pallas_call(
        functools.partial(_gla_kernel, nchunk=nchunk),
        grid=(bsz, T // tb),
        in_specs=[col(OFF_GQ, GLA_KW), col(OFF_GK, GLA_KW), col(OFF_GV, GLA_VW), col(OFF_GR, GLA_VW),
                  col(OFF_SMALL, LANES),
                  full((LANES, GLA_KW)), full((1, GLA_KW)), full((1, GLA_DV)), full((GLA_CHUNK, GLA_CHUNK))],
        out_specs=pl.BlockSpec((None, tb, GLA_VW), lambda b, t: (b, t, 0)),
        out_shape=jax.ShapeDtypeStruct((bsz, T, GLA_VW), BF16),
        scratch_shapes=[pltpu.VMEM((GLA_HEADS, GLA_DV, GLA_DK), F32), pltpu.VMEM((tb, GLA_KW), F32)],
        compiler_params=_cparams(("parallel", "arbitrary")),
        name="gla",
    )(proj3, proj3, proj3, proj3, proj3, wa_pad, ba.reshape(1, GLA_KW), norm_g.reshape(1, GLA_DV), tri)


def _ssd_kernel(z_ref, x_ref, bm_ref, cm_ref, sm_ref, cwx_ref, cwb_ref, cwc_ref, cbx_ref, cbb_ref, cbc_ref,
                dtb_ref, alog_ref, dskip_ref, ng_ref, tri_ref, exp_ref, o_ref,
                st_ref, extx_ref, extb_ref, extc_ref, xa_ref, ba_ref, ca_ref, dt_ref, a_ref, *, tb, nchunk):
    L = SSM_CHUNK
    GW = SSM_HPG * SSM_HEADDIM
    NS = SSM_STATE
    first = pl.program_id(1) == 0

    @pl.when(first)
    def _():
        st_ref[...] = jnp.zeros_like(st_ref)

    def conv_silu(src_ref, ext_ref, w_ref, b_ref, dst_ref):
        @pl.when(first)
        def _():
            ext_ref[0:8, :] = jnp.zeros((8, ext_ref.shape[1]), F32)

        @pl.when(jnp.logical_not(first))
        def _():
            ext_ref[0:8, :] = ext_ref[tb:tb + 8, :]

        ext_ref[8:8 + tb, :] = src_ref[...]
        acc = b_ref[...] + w_ref[SSM_CONV - 1:SSM_CONV, :] * ext_ref[8:8 + tb, :]
        for j in range(1, SSM_CONV):
            acc = acc + w_ref[SSM_CONV - 1 - j:SSM_CONV - j, :] * ext_ref[8 - j:8 - j + tb, :]
        dst_ref[...] = _silu(acc)

    conv_silu(x_ref, extx_ref, cwx_ref, cbx_ref, xa_ref)
    conv_silu(bm_ref, extb_ref, cwb_ref, cbb_ref, ba_ref)
    conv_silu(cm_ref, extc_ref, cwc_ref, cbc_ref, ca_ref)

    v = sm_ref[...] + dtb_ref[...]
    dt = jnp.maximum(v, 0.0) + jnp.log1p(jnp.exp(-jnp.abs(v)))
    dt_ref[...] = dt
    a_ref[...] = dt * (-jnp.exp(alog_ref[...]))

    tri = tri_ref[...]
    row = lax.broadcasted_iota(jnp.int32, (L, GW), 0)
    lane = lax.broadcasted_iota(jnp.int32, (L, GW), 1)
    lane_in = jnp.bitwise_and(lane, SSM_HEADDIM - 1)
    eye_t = (lane_in == row).astype(F32)
    tril_t = lane_in <= row
    bd_mask = (lax.broadcasted_iota(jnp.int32, (GW, GW), 0) // L
               == lax.broadcasted_iota(jnp.int32, (GW, GW), 1) // SSM_HEADDIM).astype(F32)

    def chunk(c, carry):
        rows = pl.ds(pl.multiple_of(c * L, L), L)
        cum128 = _dot01_left(tri, a_ref[rows, :])
        dt_c = dt_ref[rows, :]
        for g in range(SSM_GROUPS):
            cs = slice(g * GW, (g + 1) * GW)
            e_g = exp_ref[:, cs]
            cum_e = _dot01_right(cum128, e_g)
            dt_e = _dot01_right(dt_c, e_g)
            cum_last = cum_e[L - 1:L, :]
            r_row = jnp.sum(cum_e * eye_t, axis=0, keepdims=True)
            decay = jnp.exp(jnp.where(tril_t, cum_e - r_row, -jnp.inf))
            x_g = xa_ref[rows, cs]
            xdt = x_g * dt_e
            b_g = ba_ref[rows, g * NS:(g + 1) * NS]
            c_g = ca_ref[rows, g * NS:(g + 1) * NS].astype(BF16)
            bb = b_g.astype(BF16)
            cb_t = _dot_nt(c_g, jnp.concatenate([bb] * SSM_HPG, axis=0))
            xdt_bd = (jnp.concatenate([xdt] * SSM_HPG, axis=0) * bd_mask).astype(BF16)
            y = _dot((cb_t * decay).astype(BF16), xdt_bd)
            st = st_ref[g]
            y = y + _dot(c_g, st.astype(BF16)) * jnp.exp(cum_e)
            dend = jnp.exp(cum_last - cum_e)
            st_ref[g] = jnp.exp(cum_last) * st + _dot(b_g.T.astype(BF16), (dend * xdt).astype(BF16))
            y = y + x_g * dskip_ref[:, cs]
            y = y * _silu(z_ref[rows, cs])
            y = y * lax.rsqrt(jnp.mean(y * y, axis=-1, keepdims=True) + EPS) * ng_ref[:, cs]
            o_ref[rows, cs] = y.astype(o_ref.dtype)
        return carry

    lax.fori_loop(0, nchunk, chunk, 0)


def _ssd_consts():
    tri = jnp.asarray(np.tril(np.ones((SSM_CHUNK, SSM_CHUNK), np.float32)), BF16)
    e = np.zeros((LANES, SSM_INNER), np.float32)
    for h in range(SSM_HEADS):
        e[SM_DT + h, h * SSM_HEADDIM:(h + 1) * SSM_HEADDIM] = 1.0
    return tri, jnp.asarray(e, BF16)


def _small_row(v, off):
    return jnp.zeros((1, LANES), F32).at[0, off:off + v.shape[0]].set(v.astype(F32))


def ssd_mixer(proj3, conv_w, conv_b, dt_bias, a_log, d_skip, norm_g, tb=256):
    bsz, T, _ = proj3.shape
    tb = min(tb, T)
    nchunk = tb // SSM_CHUNK
    GN = SSM_GROUPS * SSM_STATE
    tri, expand = _ssd_consts()
    cwx, cwb, cwc = conv_w[:, :SSM_INNER], conv_w[:, SSM_INNER:SSM_INNER + GN], conv_w[:, SSM_INNER + GN:]
    cb2 = conv_b.reshape(1, SSM_CONV_CH)
    cbx, cbb, cbc = cb2[:, :SSM_INNER], cb2[:, SSM_INNER:SSM_INNER + GN], cb2[:, SSM_INNER + GN:]
    dtb = _small_row(dt_bias, SM_DT)
    alog = _small_row(a_log, SM_DT)
    dskip = jnp.repeat(d_skip.astype(F32), SSM_HEADDIM).reshape(1, SSM_INNER)

    def col(off, w):
        return pl.BlockSpec((None, tb, w), lambda b, t, o=off // w: (b, t, o))

    def full(shape):
        return pl.BlockSpec(shape, lambda b, t: (0,) * len(shape))

    return pl.pallas_call(
        functools.partial(_ssd_kernel, tb=tb, nchunk=nchunk),
        grid=(bsz, T // tb),
        in_specs=[col(OFF_SZ, SSM_INNER), col(OFF_SX, SSM_INNER), col(OFF_SB, GN), col(OFF_SC, GN),
                  col(OFF_SMALL, LANES),
                  full((SSM_CONV, SSM_INNER)), full((SSM_CONV, GN)), full((SSM_CONV, GN)),
                  full((1, SSM_INNER)), full((1, GN)), full((1, GN)),
                  full((1, LANES)), full((1, LANES)), full((1, SSM_INNER)), full((1, SSM_INNER)),
                  full((SSM_CHUNK, SSM_CHUNK)), full((LANES, SSM_INNER))],
        out_specs=pl.BlockSpec((None, tb, SSM_INNER), lambda b, t: (b, t, 0)),
        out_shape=jax.ShapeDtypeStruct((bsz, T, SSM_INNER), BF16),
        scratch_shapes=[pltpu.VMEM((SSM_GROUPS, SSM_STATE, SSM_HPG * SSM_HEADDIM), F32),
                        pltpu.VMEM((tb + 8, SSM_INNER), F32), pltpu.VMEM((tb + 8, GN), F32),
                        pltpu.VMEM((tb + 8, GN), F32),
                        pltpu.VMEM((tb, SSM_INNER), F32), pltpu.VMEM((tb, GN), F32), pltpu.VMEM((tb, GN), F32),
                        pltpu.VMEM((tb, LANES), F32), pltpu.VMEM((tb, LANES), F32)],
        compiler_params=_cparams(("parallel", "arbitrary")),
        name="ssd",
    )(proj3, proj3, proj3, proj3, proj3, cwx, cwb, cwc, cbx, cbb, cbc, dtb, alog, dskip,
      norm_g.reshape(1, SSM_INNER), tri, expand)


NSA_SLOPES = tuple(float(np.float32(2.0 ** (-8.0 * (i + 1) / NSA_HEADS))) for i in range(NSA_HEADS))
NSA_SCALE = NSA_DH ** -0.5
KV_W = 2 * NSA_KVW
NEG_HUGE = -3.0e38


def _head_cols(hh):
    return slice(hh * NSA_DH, (hh + 1) * NSA_DH)


def _nsa_compress_kernel(k_ref, v_ref, pos_ref, bd1_ref, bd2_ref, o_ref, *, n16):
    S = NSA_CMP_STRIDE
    top = jnp.zeros((n16, KV_W), F32)
    bot = jnp.zeros((n16, KV_W), F32)
    for l in range(S):
        rows = pl.ds(l, n16, stride=S)
        x = jnp.concatenate([k_ref[rows, :], v_ref[rows, :]], axis=-1)
        top += _dot((x + pos_ref[l:l + 1, :]).astype(BF16), bd1_ref[l])
        bot += _dot((x + pos_ref[S + l:S + l + 1, :]).astype(BF16), bd1_ref[S + l])
    pre = top + pltpu.roll(bot, n16 - 1, axis=0)
    out = _dot(_silu(pre).astype(BF16), bd2_ref[...])
    row = lax.broadcasted_iota(jnp.int32, (n16, KV_W), 0)
    o_ref[...] = jnp.where(row < n16 - 1, out, 0.0)


def nsa_compress(proj3, cmp_pos, cmp_w1, cmp_w2):
    bsz, T, _ = proj3.shape
    n16 = T // NSA_CMP_STRIDE
    sel = np.array([0, 0, 1, 1])
    eye = jnp.eye(4, dtype=F32)
    w1r = cmp_w1.reshape(2, NSA_CMP_LEN, NSA_DH, NSA_DH)[sel]
    bd1 = jnp.einsum('ab,alde->ladbe', eye, w1r).reshape(NSA_CMP_LEN, KV_W, KV_W).astype(BF16)
    bd2 = jnp.einsum('ab,ade->adbe', eye, cmp_w2[sel]).reshape(KV_W, KV_W).astype(BF16)
    pos = jnp.concatenate([cmp_pos[0], cmp_pos[0], cmp_pos[1], cmp_pos[1]], axis=-1)
    return pl.pallas_call(
        functools.partial(_nsa_compress_kernel, n16=n16),
        grid=(bsz,),
        in_specs=[pl.BlockSpec((None, T, NSA_KVW), lambda b: (b, 0, OFF_NKV // NSA_KVW)),
                  pl.BlockSpec((None, T, NSA_KVW), lambda b: (b, 0, OFF_NKV // NSA_KVW + 1)),
                  pl.BlockSpec((NSA_CMP_LEN, KV_W), lambda b: (0, 0)),
                  pl.BlockSpec((NSA_CMP_LEN, KV_W, KV_W), lambda b: (0, 0, 0)),
                  pl.BlockSpec((KV_W, KV_W), lambda b: (0, 0))],
        out_specs=pl.BlockSpec((None, n16, KV_W), lambda b: (b, 0, 0)),
        out_shape=jax.ShapeDtypeStruct((bsz, n16, KV_W), F32),
        compiler_params=_cparams(("parallel",)),
        name="nsa_compress",
    )(proj3, proj3, pos, bd1, bd2)


def _stack_queries(q_ref, qc_ref, qa_ref, g, tq, mid=None):
    for h in range(NSA_HPG):
        hh = g * NSA_HPG + h
        parts = [(q_ref[:, _head_cols(hh)] * NSA_SCALE).astype(BF16)]
        if mid is not None:
            parts.append(mid)
        parts.append(jnp.broadcast_to(qc_ref[hh:hh + 1, :], (tq, EXTRA_W)))
        qa_ref[h * tq:(h + 1) * tq, :] = jnp.concatenate(parts, axis=-1)
    return qa_ref[...]


def _nsa_cmp_topk_kernel(q_ref, kcv_ref, cpos_ref, qc_ref, sm_ref, ov_ref, o_ref, sel_ref, act_ref, qa_ref,
                         *, tq, n16, n_sel, n_top):
    t0 = pl.program_id(1) * tq
    tpos = t0 + lax.broadcasted_iota(jnp.int32, (tq, 1), 0)
    cmp_end = lax.broadcasted_iota(jnp.int32, (1, n16), 1) * NSA_CMP_STRIDE + (NSA_CMP_LEN - 1)
    bias = jnp.where(cmp_end <= tpos, 0.0, -BIG)
    any_valid = (tpos >= NSA_CMP_LEN - 1).astype(F32)
    gates = 1.0 / (1.0 + jnp.exp(-sm_ref[...]))
    blk = lax.broadcasted_iota(jnp.int32, (1, n_sel), 1)
    blk_f = blk.astype(F32)
    cur = tpos // NSA_SEL_BLOCK
    forced = (blk == 0) | (blk == cur) | (blk == cur - 1)
    future = blk * NSA_SEL_BLOCK > tpos
    for g in range(NSA_GROUPS):
        kc = jnp.concatenate([kcv_ref[:, g * NSA_DH:(g + 1) * NSA_DH].astype(BF16), cpos_ref[...]], axis=-1)
        vc = kcv_ref[:, NSA_KVW + g * NSA_DH:NSA_KVW + (g + 1) * NSA_DH].astype(BF16)
        _stack_queries(q_ref, qc_ref, qa_ref, g, tq)
        psum = None
        for h in range(NSA_HPG):
            hh = g * NSA_HPG + h
            s = _dot_nt(qa_ref[h * tq:(h + 1) * tq, :], kc) + bias
            e = jnp.exp(s - jnp.max(s, axis=-1, keepdims=True))
            p = e * (any_valid / jnp.sum(e, axis=-1, keepdims=True))
            psum = p if psum is None else psum + p
            o_ref[:, _head_cols(hh)] = gates[:, SM_NG + hh:SM_NG + hh + 1] * _dot(p.astype(BF16), vc)
        imp = _dot01_right(psum, ov_ref[...])
        work = jnp.where(forced, BIG, jnp.where(future, -BIG, imp))
        sel = jnp.zeros((tq, n_sel), F32)
        for _ in range(n_top):
            top = jnp.max(work, axis=-1, keepdims=True)
            idx = jnp.min(jnp.where(work == top, blk_f, float(n_sel)), axis=-1, keepdims=True)
            pick = blk_f == idx
            sel = jnp.where(pick, 1.0, sel)
            work = jnp.where(pick, NEG_HUGE, work)
        sel_ref[:, g * n_sel:(g + 1) * n_sel] = sel.astype(sel_ref.dtype)
        union = jnp.max(sel, axis=0, keepdims=True)
        act_ref[:, g * n_sel:(g + 1) * n_sel] = jnp.where(blk * NSA_SEL_BLOCK < t0, union, 0.0)


def _nsa_overlap(n16, n_sel):
    n_cmp = n16 - 1
    tok = (np.arange(n_cmp) * NSA_CMP_STRIDE)[:, None] + np.arange(NSA_CMP_LEN)[None, :]
    ov = np.zeros((n16, n_sel), np.float32)
    np.add.at(ov, (np.repeat(np.arange(n_cmp), NSA_CMP_LEN), (tok // NSA_SEL_BLOCK).ravel()), 1.0 / NSA_CMP_LEN)
    return jnp.asarray(ov, BF16)


def nsa_cmp_topk(proj3, kcv, tq=256):
    bsz, T, _ = proj3.shape
    tq = min(tq, T)
    n16 = T // NSA_CMP_STRIDE
    n_sel = T // NSA_SEL_BLOCK
    n_top = min(NSA_TOPN, n_sel)
    cmp_end = np.arange(n16) * NSA_CMP_STRIDE + NSA_CMP_LEN - 1
    cpos = jnp.asarray(_pos_cols(cmp_end, np.ones(n16, bool)), BF16)
    return pl.pallas_call(
        functools.partial(_nsa_cmp_topk_kernel, tq=tq, n16=n16, n_sel=n_sel, n_top=n_top),
        grid=(bsz, T // tq),
        in_specs=[pl.BlockSpec((None, tq, NSA_QW), lambda b, i: (b, i, OFF_NQ // NSA_QW)),
                  pl.BlockSpec((None, n16, KV_W), lambda b, i: (b, 0, 0)),
                  pl.BlockSpec((n16, EXTRA_W), lambda b, i: (0, 0)),
                  pl.BlockSpec((NSA_HEADS, EXTRA_W), lambda b, i: (0, 0)),
                  pl.BlockSpec((None, tq, LANES), lambda b, i: (b, i, OFF_SMALL // LANES)),
                  pl.BlockSpec((n16, n_sel), lambda b, i: (0, 0))],
        out_specs=[pl.BlockSpec((None, tq, NSA_QW), lambda b, i: (b, i, 0)),
                   pl.BlockSpec((None, tq, NSA_GROUPS * n_sel), lambda b, i: (b, i, 0)),
                   pl.BlockSpec((None, None, 1, NSA_GROUPS * n_sel), lambda b, i: (b, i, 0, 0))],
        out_shape=[jax.ShapeDtypeStruct((bsz, T, NSA_QW), F32),
                   jax.ShapeDtypeStruct((bsz, T, NSA_GROUPS * n_sel), BF16),
                   jax.ShapeDtypeStruct((bsz, T // tq, 1, NSA_GROUPS * n_sel), F32)],
        scratch_shapes=[pltpu.VMEM((NSA_HPG * tq, NSA_DH + EXTRA_W), BF16)],
        compiler_params=_cparams(("parallel", "parallel")),
        name="nsa_cmp_topk",
    )(proj3, kcv, cpos, _nsa_query_consts(), proj3, _nsa_overlap(n16, n_sel))


def _nsa_window_kernel(q_ref, k0_ref, k1_ref, k2_ref, v0_ref, v1_ref, v2_ref, qc_ref, sm_ref, prev_ref, o_ref,
                       qa_ref, *, tq):
    nb = NSA_WINDOW // tq + 1
    i = pl.program_id(1)
    tpos = i * tq + lax.broadcasted_iota(jnp.int32, (tq, 1), 0)
    kpos = (i - (nb - 1)) * tq + lax.broadcasted_iota(jnp.int32, (1, nb * tq), 1)
    d = tpos - kpos
    valid = (d >= 0) & (d < NSA_WINDOW) & (kpos >= 0)
    hpc = NSA_HPG // WIN_CHAINS
    bias = jnp.concatenate([jnp.where(valid, 0.0, -BIG)] * hpc, axis=0)
    gates = 1.0 / (1.0 + jnp.exp(-sm_ref[...]))
    for g in range(NSA_GROUPS):
        kw = jnp.concatenate([k0_ref[g], k1_ref[g], k2_ref[g]], axis=0)
        vw = jnp.concatenate([v0_ref[g], v1_ref[g], v2_ref[g]], axis=0)
        _stack_queries(q_ref, qc_ref, qa_ref, g, tq)
        for c in range(WIN_CHAINS):
            s = _dot_nt(qa_ref[c * hpc * tq:(c + 1) * hpc * tq, :], kw) + bias
            e = jnp.exp(s - jnp.max(s, axis=-1, keepdims=True))
            acc = _dot(e.astype(BF16), vw)
            o = acc[:, :NSA_DH] * (1.0 / acc[:, NSA_DH:NSA_DH + 1])
            for h in range(hpc):
                hh = g * NSA_HPG + c * hpc + h
                gate = gates[:, SM_NG + 2 * NSA_HEADS + hh:SM_NG + 2 * NSA_HEADS + hh + 1]
                o_ref[:, _head_cols(hh)] = prev_ref[:, _head_cols(hh)] + gate * o[h * tq:(h + 1) * tq]


def nsa_window(proj3, prev, tq=256):
    bsz, T, _ = proj3.shape
    tq = min(tq, T)
    assert NSA_WINDOW % tq == 0 and NSA_WINDOW // tq == 2
    kc, vc = _nsa_key_consts(T, tq, onehot=False)
    kaug, vaug = nsa_kprep(proj3, kc, vc, tq, (OFF_NKV + 2 * KV_W) // KV_W)
    wk = kaug.shape[-1]

    def kvspec(back, w):
        return pl.BlockSpec((None, NSA_GROUPS, tq, w), lambda b, i, back=back: (b, 0, jnp.maximum(i - back, 0), 0))

    return pl.pallas_call(
        functools.partial(_nsa_window_kernel, tq=tq),
        grid=(bsz, T // tq),
        in_specs=[pl.BlockSpec((None, tq, NSA_QW), lambda b, i: (b, i, OFF_NQ // NSA_QW)),
                  kvspec(2, wk), kvspec(1, wk), kvspec(0, wk),
                  kvspec(2, 2 * NSA_DH), kvspec(1, 2 * NSA_DH), kvspec(0, 2 * NSA_DH),
                  pl.BlockSpec((NSA_HEADS, EXTRA_W), lambda b, i: (0, 0)),
                  pl.BlockSpec((None, tq, LANES), lambda b, i: (b, i, OFF_SMALL // LANES)),
                  pl.BlockSpec((None, tq, NSA_QW), lambda b, i: (b, i, 0))],
        out_specs=pl.BlockSpec((None, tq, NSA_QW), lambda b, i: (b, i, 0)),
        out_shape=jax.ShapeDtypeStruct((bsz, T, NSA_QW), F32),
        scratch_shapes=[pltpu.VMEM((NSA_HPG * tq, wk), BF16)],
        compiler_params=_cparams(("parallel", "parallel")),
        name="nsa_window",
    )(proj3, kaug, kaug, kaug, vaug, vaug, vaug, _nsa_query_consts(), proj3, prev)


MASK_BIG = 1e30
SEL_STEP = 8
NSA_CHAINS = 4
WIN_CHAINS = 2
POS_HI = 128
EXTRA_W = NSA_DH


def _bf16_pieces(x):
    x = np.float32(x)
    out = []
    for _ in range(3):
        p = np.float32(np.asarray(x, dtype=jnp.bfloat16))
        out.append(p)
        x = np.float32(x - p)
    return out


def _pos_cols(pos, real):
    c = np.zeros((len(pos), EXTRA_W), np.float32)
    for j in range(3):
        c[real, j] = (pos[real] // POS_HI) * POS_HI
        c[real, 3 + j] = pos[real] % POS_HI
    c[~real, 6] = 1.0
    return c


def _nsa_query_consts():
    qc = np.zeros((NSA_HEADS, EXTRA_W), np.float32)
    for hh in range(NSA_HEADS):
        qc[hh, 0:3] = qc[hh, 3:6] = _bf16_pieces(NSA_SLOPES[hh])
        qc[hh, 6] = -MASK_BIG
    return jnp.asarray(qc, BF16)


def _nsa_key_consts(T, pad, onehot):
    n_sel = T // NSA_SEL_BLOCK
    pos = np.arange(T + pad)
    real = pos < T
    kc = _pos_cols(pos, real)
    if onehot:
        oh = np.zeros((T + pad, n_sel), np.float32)
        oh[pos[real], pos[real] // NSA_SEL_BLOCK] = 1.0
        kc = np.concatenate([oh, kc], axis=1)
    vc = np.zeros((T + pad, NSA_DH), np.float32)
    vc[real, 0] = 1.0
    return jnp.asarray(kc, BF16), jnp.asarray(vc, BF16)


def _nsa_kprep_kernel(kv_ref, kc_ref, vc_ref, ka_ref, va_ref):
    real = pl.program_id(1) < pl.num_programs(1) - 1
    kv = (kv_ref[...] * jnp.where(real, 1.0, 0.0)).astype(BF16)
    for g in range(NSA_GROUPS):
        ka_ref[g] = jnp.concatenate([kv[:, g * NSA_DH:(g + 1) * NSA_DH], kc_ref[...]], axis=-1)
        va_ref[g] = jnp.concatenate([kv[:, NSA_KVW + g * NSA_DH:NSA_KVW + (g + 1) * NSA_DH], vc_ref[...]], axis=-1)


def nsa_kprep(proj3, kc, vc, tb, cs):
    bsz, T, _ = proj3.shape
    nt = T // tb
    wk = NSA_DH + kc.shape[1]
    return pl.pallas_call(
        _nsa_kprep_kernel,
        grid=(bsz, nt + 1),
        in_specs=[pl.BlockSpec((None, tb, KV_W), lambda b, t: (b, jnp.minimum(t, nt - 1), cs)),
                  pl.BlockSpec((tb, kc.shape[1]), lambda b, t: (t, 0)),
                  pl.BlockSpec((tb, NSA_DH), lambda b, t: (t, 0))],
        out_specs=[pl.BlockSpec((None, NSA_GROUPS, tb, wk), lambda b, t: (b, 0, t, 0)),
                   pl.BlockSpec((None, NSA_GROUPS, tb, 2 * NSA_DH), lambda b, t: (b, 0, t, 0))],
        out_shape=[jax.ShapeDtypeStruct((bsz, NSA_GROUPS, T + tb, wk), BF16),
                   jax.ShapeDtypeStruct((bsz, NSA_GROUPS, T + tb, 2 * NSA_DH), BF16)],
        compiler_params=_cparams(("parallel", "parallel")),
        name="nsa_kprep",
    )(proj3, kc, vc)


def _nsa_select_kernel(ids_ref, cnt_ref, q_ref, ka_ref, va_ref, sel_ref, qc_ref, sm_ref, prev_ref, o_ref,
                       qa_ref, kt_ref, vt_ref, m_ref, acc_ref, *, tq, n_sel, lmax):
    b = pl.program_id(0)
    i = pl.program_id(1)
    B = NSA_SEL_BLOCK
    tpos = i * tq + lax.broadcasted_iota(jnp.int32, (tq, 1), 0)
    kpos = i * tq + lax.broadcasted_iota(jnp.int32, (1, tq), 1)
    causal_bias = jnp.where(kpos <= tpos, 0.0, -BIG)
    hpc = NSA_HPG // NSA_CHAINS
    chains = [slice(c * hpc * tq, (c + 1) * hpc * tq) for c in range(NSA_CHAINS)]
    causal_bias = jnp.concatenate([causal_bias] * hpc, axis=0)
    gates = 1.0 / (1.0 + jnp.exp(-sm_ref[...]))
    own = pl.ds(pl.multiple_of(i * tq, tq), tq)
    for g in range(NSA_GROUPS):
        mcols = ((sel_ref[:, g * n_sel:(g + 1) * n_sel].astype(F32) - 1.0) * MASK_BIG).astype(BF16)
        _stack_queries(q_ref, qc_ref, qa_ref, g, tq, mid=mcols)
        for rs in chains:
            s = _dot_nt(qa_ref[rs, :], ka_ref[g, own, :]) + causal_bias
            m0 = jnp.max(s, axis=-1, keepdims=True)
            m_ref[rs, :] = m0
            acc_ref[rs, :] = _dot(jnp.exp(s - m0).astype(BF16), va_ref[g, own, :])
        slot = (b * pl.num_programs(1) + i) * NSA_GROUPS + g

        def step(st, carry):
            for u in range(SEL_STEP):
                r0 = pl.multiple_of(ids_ref[slot * lmax + st * SEL_STEP + u] * B, B)
                kt_ref[u * B:(u + 1) * B, :] = ka_ref[g, pl.ds(r0, B), :]
                vt_ref[u * B:(u + 1) * B, :] = va_ref[g, pl.ds(r0, B), :]
            for rs in chains:
                s = _dot_nt(qa_ref[rs, :], kt_ref[...])
                m_old = m_ref[rs, :]
                m_new = jnp.maximum(m_old, jnp.max(s, axis=-1, keepdims=True))
                p = jnp.exp(s - m_new).astype(BF16)
                acc_ref[rs, :] = jnp.exp(m_old - m_new) * acc_ref[rs, :] + _dot(p, vt_ref[...])
                m_ref[rs, :] = m_new
            return carry

        lax.fori_loop(0, (cnt_ref[slot] + SEL_STEP - 1) // SEL_STEP, step, 0)
        acc = acc_ref[...]
        o = acc[:, :NSA_DH] * (1.0 / acc[:, NSA_DH:NSA_DH + 1])
        for h in range(NSA_HPG):
            hh = g * NSA_HPG + h
            gate = gates[:, SM_NG + NSA_HEADS + hh:SM_NG + NSA_HEADS + hh + 1]
            o_ref[:, _head_cols(hh)] = (prev_ref[:, _head_cols(hh)] + gate * o[h * tq:(h + 1) * tq, :]
                                        ).astype(o_ref.dtype)


def nsa_select(proj3, sel, act, prev, tq=256):
    bsz, T, _ = proj3.shape
    tq = min(tq, T)
    nt = T // tq
    n_sel = T // NSA_SEL_BLOCK
    lmax = -(-n_sel // SEL_STEP) * SEL_STEP
    kc, vc = _nsa_key_consts(T, tq, onehot=True)
    qc = _nsa_query_consts()
    kaug, vaug = nsa_kprep(proj3, kc, vc, tq, (OFF_NKV + KV_W) // KV_W)
    wk = kaug.shape[-1]
    on = (act.reshape(bsz, nt, NSA_GROUPS, n_sel) > 0.0).astype(jnp.int32)
    seen = jnp.cumsum(on, axis=-1)
    ids = jnp.sum(seen[..., None, :] <= jnp.arange(lmax, dtype=jnp.int32)[:, None], axis=-1, dtype=jnp.int32)
    ids = ids.reshape(-1)
    cnt = seen[..., -1].reshape(-1)
    rows = NSA_HPG * tq
    grid_spec = pltpu.PrefetchScalarGridSpec(
        num_scalar_prefetch=2,
        grid=(bsz, nt),
        in_specs=[pl.BlockSpec((None, tq, NSA_QW), lambda b, i, *_: (b, i, OFF_NQ // NSA_QW)),
                  pl.BlockSpec((None, NSA_GROUPS, T + tq, wk), lambda b, i, *_: (b, 0, 0, 0)),
                  pl.BlockSpec((None, NSA_GROUPS, T + tq, 2 * NSA_DH), lambda b, i, *_: (b, 0, 0, 0)),
                  pl.BlockSpec((None, tq, NSA_GROUPS * n_sel), lambda b, i, *_: (b, i, 0)),
                  pl.BlockSpec((NSA_HEADS, EXTRA_W), lambda b, i, *_: (0, 0)),
                  pl.BlockSpec((None, tq, LANES), lambda b, i, *_: (b, i, OFF_SMALL // LANES)),
                  pl.BlockSpec((None, tq, NSA_QW), lambda b, i, *_: (b, i, 0))],
        out_specs=pl.BlockSpec((None, tq, NSA_QW), lambda b, i, *_: (b, i, 0)),
        scratch_shapes=[pltpu.VMEM((rows, wk), BF16),
                        pltpu.VMEM((SEL_STEP * NSA_SEL_BLOCK, wk), BF16),
                        pltpu.VMEM((SEL_STEP * NSA_SEL_BLOCK, 2 * NSA_DH), BF16),
                        pltpu.VMEM((rows, 1), F32), pltpu.VMEM((rows, 2 * NSA_DH), F32)])
    return pl.pallas_call(
        functools.partial(_nsa_select_kernel, tq=tq, n_sel=n_sel, lmax=lmax),
        grid_spec=grid_spec,
        out_shape=jax.ShapeDtypeStruct((bsz, T, NSA_QW), BF16),
        compiler_params=_cparams(("parallel", "arbitrary")),
        name="nsa_select",
    )(ids, cnt, proj3, kaug, vaug, sel, qc, proj3, prev)


def nsa_mixer(proj3, cmp_pos, cmp_w1, cmp_w2):
    kcv = nsa_compress(proj3, cmp_pos, cmp_w1, cmp_w2)
    o_cmp, sel, act = nsa_cmp_topk(proj3, kcv)
    o_cw = nsa_window(proj3, o_cmp)
    return nsa_select(proj3, sel, act, o_cw)


def _pack_w_in(w_in):
    w_in = w_in.astype(BF16)
    (g_q, g_k, g_v, g_r, g_a, n_q, n_kv, n_g, s_z, s_xbc, s_dt, m_g) = jnp.split(w_in, SPLIT_POINTS, axis=-1)
    pad = jnp.zeros(w_in.shape[:-1] + (D_PK - OFF_SMALL - GLA_RANK - 3 * NSA_HEADS - SSM_HEADS,), w_in.dtype)
    return jnp.concatenate([m_g, g_q, g_k, g_v, g_r, n_q, s_z, s_xbc, n_kv, g_a, n_g, s_dt, pad], axis=-1)


def kernel(x, w_in, gla_a2, gla_a_bias, gla_norm, nsa_cmp_pos, nsa_cmp_w1, nsa_cmp_w2, ssm_conv_w, ssm_conv_b,
           ssm_dt_bias, ssm_a_log, ssm_d, ssm_norm, w_branch, w_out, norm_pre_mix, norm_post_mix, norm_pre_ffn,
           norm_post_ffn, w_ffn_gate, w_ffn_up, w_ffn_down):
    bsz, T, D = x.shape
    depth = w_in.shape[0]
    n = bsz * T
    w_in_pk = _pack_w_in(w_in)
    wa_pad = jnp.zeros((depth, LANES, GLA_KW), F32).at[:, SM_GA:SM_GA + GLA_RANK].set(gla_a2).astype(BF16)
    w_branch_b = w_branch.astype(BF16)
    w_out_b = w_out.astype(BF16)
    w_gate_b = w_ffn_gate.astype(BF16)
    w_up_b = w_ffn_up.astype(BF16)
    w_down_b = w_ffn_down.astype(BF16)
    xf = x.reshape(n, D)
    for l in range(depth):
        proj = norm_matmul(xf, norm_pre_mix[l], w_in_pk[l])
        proj3 = proj.reshape(bsz, T, D_PK)
        y_gla = gla_mixer(proj3, wa_pad[l], gla_a_bias[l], gla_norm[l])
        y_nsa = nsa_mixer(proj3, nsa_cmp_pos[l], nsa_cmp_w1[l], nsa_cmp_w2[l])
        y_ssm = ssd_mixer(proj3, ssm_conv_w[l], ssm_conv_b[l], ssm_dt_bias[l], ssm_a_log[l], ssm_d[l], ssm_norm[l])
        merged = merge_branches(y_gla.reshape(n, BRANCH_W), y_nsa.reshape(n, BRANCH_W), y_ssm.reshape(n, BRANCH_W),
                                w_branch_b[l], proj)
        xf = proj_norm_residual(merged, w_out_b[l], xf, norm_post_mix[l])
        act = ffn_up(xf, norm_pre_ffn[l], w_gate_b[l], w_up_b[l])
        xf = proj_norm_residual(act, w_down_b[l], xf, norm_post_ffn[l])
    return xf.reshape(bsz, T, D)
```

```python
import functools

import numpy as np
import jax
import jax.numpy as jnp
from jax import lax
from jax.experimental import pallas as pl
from jax.experimental.pallas import tpu as pltpu

F32 = jnp.float32
BF16 = jnp.bfloat16

D_MODEL = 2048
EPS = 1e-6
N_BRANCH = 3
BRANCH_W = 1024
GLA_HEADS, GLA_DK, GLA_DV, GLA_RANK, GLA_TAU, GLA_CHUNK = 4, 256, 256, 16, 16.0, 64
GLA_KW = GLA_HEADS * GLA_DK
GLA_VW = GLA_HEADS * GLA_DV
NSA_HEADS, NSA_GROUPS, NSA_DH = 16, 2, 64
NSA_HPG = NSA_HEADS // NSA_GROUPS
NSA_QW = NSA_HEADS * NSA_DH
NSA_KVW = NSA_GROUPS * NSA_DH
NSA_CMP_LEN, NSA_CMP_STRIDE, NSA_SEL_BLOCK, NSA_TOPN, NSA_WINDOW = 32, 16, 64, 16, 512
BIG = 1e30
SSM_HEADS, SSM_HEADDIM, SSM_GROUPS, SSM_STATE, SSM_CONV, SSM_CHUNK = 16, 64, 4, 128, 4, 64
SSM_INNER = SSM_HEADS * SSM_HEADDIM
SSM_HPG = SSM_HEADS // SSM_GROUPS
SSM_CONV_CH = SSM_INNER + 2 * SSM_GROUPS * SSM_STATE
D_FF = ((8 * D_MODEL // 3 + 255) // 256) * 256
IN_SIZES = (GLA_KW, GLA_KW, GLA_VW, GLA_VW, GLA_RANK, NSA_QW, 6 * NSA_KVW, 3 * NSA_HEADS,
            SSM_INNER, SSM_CONV_CH, SSM_HEADS, N_BRANCH * D_MODEL)
SPLIT_POINTS = tuple(int(v) for v in np.cumsum(IN_SIZES)[:-1])

LANES = 128
VMEM_LIMIT = 56 * 1024 * 1024

OFF_MG = 0
OFF_GQ = OFF_MG + N_BRANCH * D_MODEL
OFF_GK = OFF_GQ + GLA_KW
OFF_GV = OFF_GK + GLA_KW
OFF_GR = OFF_GV + GLA_VW
OFF_NQ = OFF_GR + GLA_VW
OFF_SZ = OFF_NQ + NSA_QW
OFF_SX = OFF_SZ + SSM_INNER
OFF_SB = OFF_SX + SSM_INNER
OFF_SC = OFF_SB + SSM_GROUPS * SSM_STATE
OFF_NKV = OFF_SC + SSM_GROUPS * SSM_STATE
OFF_SMALL = OFF_NKV + 6 * NSA_KVW
D_PK = OFF_SMALL + 2 * LANES
SM_GA = 0
SM_NG = 16
SM_DT = 64


def _cparams(sem):
    return pltpu.CompilerParams(dimension_semantics=sem, vmem_limit_bytes=VMEM_LIMIT)


def _split3(x):
    hi = x.astype(BF16)
    r1 = x - hi.astype(F32)
    mid = r1.astype(BF16)
    lo = (r1 - mid.astype(F32)).astype(BF16)
    return hi, mid, lo


def _dot(a, b):
    return jnp.dot(a, b, preferred_element_type=F32)


def _dot_nt(a, b):
    return lax.dot_general(a, b, (((1,), (1,)), ((), ())), preferred_element_type=F32)


def _dot01_left(m01, x):
    hi, mid, lo = _split3(x)
    return _dot(m01, hi) + _dot(m01, mid) + _dot(m01, lo)


def _dot01_right(x, m01):
    hi, mid, lo = _split3(x)
    return _dot(hi, m01) + _dot(mid, m01) + _dot(lo, m01)


def _silu(x):
    return x / (1.0 + jnp.exp(-x))


def _norm_rows(x_ref, g_ref, h_ref):
    @pl.when(pl.program_id(1) == 0)
    def _():
        x = x_ref[...]
        y = x * lax.rsqrt(jnp.mean(x * x, axis=-1, keepdims=True) + EPS)
        h_ref[...] = (y * g_ref[...]).astype(h_ref.dtype)


def _norm_mm_kernel(x_ref, g_ref, w_ref, o_ref, h_ref):
    _norm_rows(x_ref, g_ref, h_ref)
    o_ref[...] = _dot(h_ref[...], w_ref[...]).astype(o_ref.dtype)


def norm_matmul(x, g, w, out_dtype=F32, tm=1024, tn=1024):
    m, k = x.shape
    n = w.shape[1]
    tm = min(tm, m)
    tn = min(tn, n)
    return pl.pallas_call(
        _norm_mm_kernel,
        grid=(m // tm, n // tn),
        in_specs=[pl.BlockSpec((tm, k), lambda i, j: (i, 0)), pl.BlockSpec((1, k), lambda i, j: (0, 0)),
                  pl.BlockSpec((k, tn), lambda i, j: (0, j))],
        out_specs=pl.BlockSpec((tm, tn), lambda i, j: (i, j)),
        out_shape=jax.ShapeDtypeStruct((m, n), out_dtype),
        scratch_shapes=[pltpu.VMEM((tm, k), BF16)],
        compiler_params=_cparams(("parallel", "arbitrary")),
        name="in_proj",
    )(x, g.reshape(1, k), w)


def _merge_kernel(yg_ref, yn_ref, ys_ref, wg_ref, wn_ref, ws_ref, g0_ref, g1_ref, g2_ref, o_ref):
    def gate(ref):
        return 1.0 / (1.0 + jnp.exp(-ref[...]))

    acc = gate(g0_ref) * _dot(yg_ref[...], wg_ref[...])
    acc += gate(g1_ref) * _dot(yn_ref[...], wn_ref[...])
    acc += gate(g2_ref) * _dot(ys_ref[...], ws_ref[...])
    o_ref[...] = acc.astype(o_ref.dtype)


def merge_branches(y_gla, y_nsa, y_ssm, w_branch, proj, tm=1024, tn=512):
    m = y_gla.shape[0]
    d = w_branch.shape[-1]
    tm = min(tm, m)
    nj = d // tn
    ys = pl.BlockSpec((tm, BRANCH_W), lambda i, j: (i, 0))

    def wspec(b):
        return pl.BlockSpec((None, BRANCH_W, tn), lambda i, j, b=b: (b, 0, j))

    def gspec(b):
        return pl.BlockSpec((tm, tn), lambda i, j, b=b: (i, (OFF_MG + b * D_MODEL) // tn + j))

    return pl.pallas_call(
        _merge_kernel,
        grid=(m // tm, nj),
        in_specs=[ys, ys, ys, wspec(0), wspec(1), wspec(2), gspec(0), gspec(1), gspec(2)],
        out_specs=pl.BlockSpec((tm, tn), lambda i, j: (i, j)),
        out_shape=jax.ShapeDtypeStruct((m, d), BF16),
        compiler_params=_cparams(("parallel", "parallel")),
        name="merge",
    )(y_gla, y_nsa, y_ssm, w_branch, w_branch, w_branch, proj, proj, proj)


def _proj_norm_res_kernel(a_ref, w_ref, x_ref, g_ref, o_ref, acc_ref):
    k = pl.program_id(1)

    @pl.when(k == 0)
    def _():
        acc_ref[...] = jnp.zeros_like(acc_ref)

    acc_ref[...] += _dot(a_ref[...], w_ref[...])

    @pl.when(k == pl.num_programs(1) - 1)
    def _():
        f = acc_ref[...]
        y = f * lax.rsqrt(jnp.mean(f * f, axis=-1, keepdims=True) + EPS)
        o_ref[...] = x_ref[...] + y * g_ref[...]


def proj_norm_residual(a, w, x, g, tm=1024, tk=512):
    m, kk = a.shape
    d = w.shape[1]
    tm = min(tm, m)
    return pl.pallas_call(
        _proj_norm_res_kernel,
        grid=(m // tm, kk // tk),
        in_specs=[pl.BlockSpec((tm, tk), lambda i, k: (i, k)),
                  pl.BlockSpec((tk, d), lambda i, k: (k, 0)),
                  pl.BlockSpec((tm, d), lambda i, k: (i, 0)),
                  pl.BlockSpec((1, d), lambda i, k: (0, 0))],
        out_specs=pl.BlockSpec((tm, d), lambda i, k: (i, 0)),
        out_shape=jax.ShapeDtypeStruct((m, d), F32),
        scratch_shapes=[pltpu.VMEM((tm, d), F32)],
        compiler_params=_cparams(("parallel", "arbitrary")),
        name="proj_norm_res",
    )(a, w, x, g.reshape(1, d))


def _ffn_up_kernel(x_ref, g_ref, wg_ref, wu_ref, o_ref, h_ref):
    _norm_rows(x_ref, g_ref, h_ref)
    h = h_ref[...]
    a = _dot(h, wg_ref[...])
    u = _dot(h, wu_ref[...])
    o_ref[...] = (_silu(a) * u).astype(o_ref.dtype)


def ffn_up(x, g, wg, wu, tm=1024, tn=512):
    m, k = x.shape
    n = wg.shape[1]
    tm = min(tm, m)
    return pl.pallas_call(
        _ffn_up_kernel,
        grid=(m // tm, n // tn),
        in_specs=[pl.BlockSpec((tm, k), lambda i, j: (i, 0)),
                  pl.BlockSpec((1, k), lambda i, j: (0, 0)),
                  pl.BlockSpec((k, tn), lambda i, j: (0, j)),
                  pl.BlockSpec((k, tn), lambda i, j: (0, j))],
        out_specs=pl.BlockSpec((tm, tn), lambda i, j: (i, j)),
        out_shape=jax.ShapeDtypeStruct((m, n), BF16),
        scratch_shapes=[pltpu.VMEM((tm, k), BF16)],
        compiler_params=_cparams(("parallel", "arbitrary")),
        name="ffn_up",
    )(x, g.reshape(1, k), wg, wu)


def _gla_kernel(q_ref, k_ref, v_ref, r_ref, sm_ref, wa_ref, ba_ref, ng_ref, tri_ref, o_ref,
                st_ref, la_ref, *, nchunk):
    C = GLA_CHUNK

    @pl.when(pl.program_id(1) == 0)
    def _():
        st_ref[...] = jnp.zeros_like(st_ref)

    pre = _dot(sm_ref[...].astype(BF16), wa_ref[...]) + ba_ref[...]
    la_ref[...] = (jnp.minimum(pre, 0.0) - jnp.log1p(jnp.exp(-jnp.abs(pre)))) * (1.0 / GLA_TAU)
    tri = tri_ref[...]
    causal = lax.broadcasted_iota(jnp.int32, (C, C), 0) >= lax.broadcasted_iota(jnp.int32, (C, C), 1)

    def chunk(c, carry):
        rows = pl.ds(pl.multiple_of(c * C, C), C)
        for h in range(GLA_HEADS):
            ck = slice(h * GLA_DK, (h + 1) * GLA_DK)
            cv = slice(h * GLA_DV, (h + 1) * GLA_DV)
            bcum = _dot01_left(tri, la_ref[rows, ck])
            b_last = bcum[C - 1:C, :]
            q = q_ref[rows, ck] * (GLA_DK ** -0.5)
            k = k_ref[rows, ck]
            v = v_ref[rows, cv]
            q_dec = (q * jnp.exp(bcum)).astype(BF16)
            k_inv = (k * jnp.exp(-bcum)).astype(BF16)
            k_end = (k * jnp.exp(b_last - bcum)).astype(BF16)
            att = jnp.where(causal, _dot_nt(q_dec, k_inv), 0.0)
            st = st_ref[h]
            o = _dot(att.astype(BF16), v.astype(BF16)) + _dot_nt(q_dec, st.astype(BF16))
            st_ref[h] = jnp.exp(b_last) * st + _dot(v.T.astype(BF16), k_end)
            y = o * lax.rsqrt(jnp.mean(o * o, axis=-1, keepdims=True) + EPS) * ng_ref[...]
            o_ref[rows, cv] = (y * _silu(r_ref[rows, cv])).astype(o_ref.dtype)
        return carry

    lax.fori_loop(0, nchunk, chunk, 0, unroll=2)


def gla_mixer(proj3, wa_pad, ba, norm_g, tb=256):
    bsz, T, _ = proj3.shape
    tb = min(tb, T)
    nchunk = tb // GLA_CHUNK
    tri = jnp.asarray(np.tril(np.ones((GLA_CHUNK, GLA_CHUNK), np.float32)), BF16)

    def col(off, w):
        return pl.BlockSpec((None, tb, w), lambda b, t, o=off // w: (b, t, o))

    def full(shape):
        return pl.BlockSpec(shape, lambda b, t: (0,) * len(shape))

    return pl.pallas_call(
        functools.partial(_gla_kernel, nchunk=nchunk),
        grid=(bsz, T // tb),
        in_specs=[col(OFF_GQ, GLA_KW), col(OFF_GK, GLA_KW), col(OFF_GV, GLA_VW), col(OFF_GR, GLA_VW),
                  col(OFF_SMALL, LANES),
                  full((LANES, GLA_KW)), full((1, GLA_KW)), full((1, GLA_DV)), full((GLA_CHUNK, GLA_CHUNK))],
        out_specs=pl.BlockSpec((None, tb, GLA_VW), lambda b, t: (b, t, 0)),
        out_shape=jax.ShapeDtypeStruct((bsz, T, GLA_VW), BF16),
        scratch_shapes=[pltpu.VMEM((GLA_HEADS, GLA_DV, GLA_DK), F32), pltpu.VMEM((tb, GLA_KW), F32)],
        compiler_params=_cparams(("parallel", "arbitrary")),
        name="gla",
    )(proj3, proj3, proj3, proj3, proj3, wa_pad, ba.reshape(1, GLA_KW), norm_g.reshape(1, GLA_DV), tri)


def _ssd_kernel(z_ref, x_ref, bm_ref, cm_ref, sm_ref, cwx_ref, cwb_ref, cwc_ref, cbx_ref, cbb_ref, cbc_ref,
                dtb_ref, alog_ref, dskip_ref, ng_ref, tri_ref, exp_ref, o_ref,
                st_ref, extx_ref, extb_ref, extc_ref, xa_ref, ba_ref, ca_ref, dt_ref, a_ref, *, tb, nchunk):
    L = SSM_CHUNK
    GW = SSM_HPG * SSM_HEADDIM
    NS = SSM_STATE
    first = pl.program_id(1) == 0

    @pl.when(first)
    def _():
        st_ref[...] = jnp.zeros_like(st_ref)

    def conv_silu(src_ref, ext_ref, w_ref, b_ref, dst_ref):
        @pl.when(first)
        def _():
            ext_ref[0:8, :] = jnp.zeros((8, ext_ref.shape[1]), F32)

        @pl.when(jnp.logical_not(first))
        def _():
            ext_ref[0:8, :] = ext_ref[tb:tb + 8, :]

        ext_ref[8:8 + tb, :] = src_ref[...]
        acc = b_ref[...] + w_ref[SSM_CONV - 1:SSM_CONV, :] * ext_ref[8:8 + tb, :]
        for j in range(1, SSM_CONV):
            acc = acc + w_ref[SSM_CONV - 1 - j:SSM_CONV - j, :] * ext_ref[8 - j:8 - j + tb, :]
        dst_ref[...] = _silu(acc)

    conv_silu(x_ref, extx_ref, cwx_ref, cbx_ref, xa_ref)
    conv_silu(bm_ref, extb_ref, cwb_ref, cbb_ref, ba_ref)
    conv_silu(cm_ref, extc_ref, cwc_ref, cbc_ref, ca_ref)

    v = sm_ref[...] + dtb_ref[...]
    dt = jnp.maximum(v, 0.0) + jnp.log1p(jnp.exp(-jnp.abs(v)))
    dt_ref[...] = dt
    a_ref[...] = dt * (-jnp.exp(alog_ref[...]))

    tri = tri_ref[...]
    row = lax.broadcasted_iota(jnp.int32, (L, GW), 0)
    lane = lax.broadcasted_iota(jnp.int32, (L, GW), 1)
    lane_in = jnp.bitwise_and(lane, SSM_HEADDIM - 1)
    eye_t = (lane_in == row).astype(F32)
    tril_t = lane_in <= row
    bd_mask = (lax.broadcasted_iota(jnp.int32, (GW, GW), 0) // L
               == lax.broadcasted_iota(jnp.int32, (GW, GW), 1) // SSM_HEADDIM).astype(F32)

    def chunk(c, carry):
        rows = pl.ds(pl.multiple_of(c * L, L), L)
        cum128 = _dot01_left(tri, a_ref[rows, :])
        dt_c = dt_ref[rows, :]
        for g in range(SSM_GROUPS):
            cs = slice(g * GW, (g + 1) * GW)
            e_g = exp_ref[:, cs]
            cum_e = _dot01_right(cum128, e_g)
            dt_e = _dot01_right(dt_c, e_g)
            cum_last = cum_e[L - 1:L, :]
            r_row = jnp.sum(cum_e * eye_t, axis=0, keepdims=True)
            decay = jnp.exp(jnp.where(tril_t, cum_e - r_row, -jnp.inf))
            x_g = xa_ref[rows, cs]
            xdt = x_g * dt_e
            b_g = ba_ref[rows, g * NS:(g + 1) * NS]
            c_g = ca_ref[rows, g * NS:(g + 1) * NS].astype(BF16)
            bb = b_g.astype(BF16)
            cb_t = _dot_nt(c_g, jnp.concatenate([bb] * SSM_HPG, axis=0))
            xdt_bd = (jnp.concatenate([xdt] * SSM_HPG, axis=0) * bd_mask).astype(BF16)
            y = _dot((cb_t * decay).astype(BF16), xdt_bd)
            st = st_ref[g]
            y = y + _dot(c_g, st.astype(BF16)) * jnp.exp(cum_e)
            dend = jnp.exp(cum_last - cum_e)
            st_ref[g] = jnp.exp(cum_last) * st + _dot(b_g.T.astype(BF16), (dend * xdt).astype(BF16))
            y = y + x_g * dskip_ref[:, cs]
            y = y * _silu(z_ref[rows, cs])
            y = y * lax.rsqrt(jnp.mean(y * y, axis=-1, keepdims=True) + EPS) * ng_ref[:, cs]
            o_ref[rows, cs] = y.astype(o_ref.dtype)
        return carry

    lax.fori_loop(0, nchunk, chunk, 0, unroll=2)


def _ssd_consts():
    tri = jnp.asarray(np.tril(np.ones((SSM_CHUNK, SSM_CHUNK), np.float32)), BF16)
    e = np.zeros((LANES, SSM_INNER), np.float32)
    for h in range(SSM_HEADS):
        e[SM_DT + h, h * SSM_HEADDIM:(h + 1) * SSM_HEADDIM] = 1.0
    return tri, jnp.asarray(e, BF16)


def _small_row(v, off):
    return jnp.zeros((1, LANES), F32).at[0, off:off + v.shape[0]].set(v.astype(F32))


def ssd_mixer(proj3, conv_w, conv_b, dt_bias, a_log, d_skip, norm_g, tb=256):
    bsz, T, _ = proj3.shape
    tb = min(tb, T)
    nchunk = tb // SSM_CHUNK
    GN = SSM_GROUPS * SSM_STATE
    tri, expand = _ssd_consts()
    cwx, cwb, cwc = conv_w[:, :SSM_INNER], conv_w[:, SSM_INNER:SSM_INNER + GN], conv_w[:, SSM_INNER + GN:]
    cb2 = conv_b.reshape(1, SSM_CONV_CH)
    cbx, cbb, cbc = cb2[:, :SSM_INNER], cb2[:, SSM_INNER:SSM_INNER + GN], cb2[:, SSM_INNER + GN:]
    dtb = _small_row(dt_bias, SM_DT)
    alog = _small_row(a_log, SM_DT)
    dskip = jnp.repeat(d_skip.astype(F32), SSM_HEADDIM).reshape(1, SSM_INNER)

    def col(off, w):
        return pl.BlockSpec((None, tb, w), lambda b, t, o=off // w: (b, t, o))

    def full(shape):
        return pl.BlockSpec(shape, lambda b, t: (0,) * len(shape))

    return pl.pallas_call(
        functools.partial(_ssd_kernel, tb=tb, nchunk=nchunk),
        grid=(bsz, T // tb),
        in_specs=[col(OFF_SZ, SSM_INNER), col(OFF_SX, SSM_INNER), col(OFF_SB, GN), col(OFF_SC, GN),
                  col(OFF_SMALL, LANES),
                  full((SSM_CONV, SSM_INNER)), full((SSM_CONV, GN)), full((SSM_CONV, GN)),
                  full((1, SSM_INNER)), full((1, GN)), full((1, GN)),
                  full((1, LANES)), full((1, LANES)), full((1, SSM_INNER)), full((1, SSM_INNER)),
                  full((SSM_CHUNK, SSM_CHUNK)), full((LANES, SSM_INNER))],
        out_specs=pl.BlockSpec((None, tb, SSM_INNER), lambda b, t: (b, t, 0)),
        out_shape=jax.ShapeDtypeStruct((bsz, T, SSM_INNER), BF16),
        scratch_shapes=[pltpu.VMEM((SSM_GROUPS, SSM_STATE, SSM_HPG * SSM_HEADDIM), F32),
                        pltpu.VMEM((tb + 8, SSM_INNER), F32), pltpu.VMEM((tb + 8, GN), F32),
                        pltpu.VMEM((tb + 8, GN), F32),
                        pltpu.VMEM((tb, SSM_INNER), F32), pltpu.VMEM((tb, GN), F32), pltpu.VMEM((tb, GN), F32),
                        pltpu.VMEM((tb, LANES), F32), pltpu.VMEM((tb, LANES), F32)],
        compiler_params=_cparams(("parallel", "arbitrary")),
        name="ssd",
    )(proj3, proj3, proj3, proj3, proj3, cwx, cwb, cwc, cbx, cbb, cbc, dtb, alog, dskip,
      norm_g.reshape(1, SSM_INNER), tri, expand)


NSA_SLOPES = tuple(float(np.float32(2.0 ** (-8.0 * (i + 1) / NSA_HEADS))) for i in range(NSA_HEADS))
NSA_SCALE = NSA_DH ** -0.5
KV_W = 2 * NSA_KVW
NEG_HUGE = -3.0e38


def _head_cols(hh):
    return slice(hh * NSA_DH, (hh + 1) * NSA_DH)


def _nsa_compress_kernel(k_ref, v_ref, pos_ref, bd1_ref, bd2_ref, o_ref, *, n16):
    S = NSA_CMP_STRIDE
    top = jnp.zeros((n16, KV_W), F32)
    bot = jnp.zeros((n16, KV_W), F32)
    for l in range(S):
        rows = pl.ds(l, n16, stride=S)
        x = jnp.concatenate([k_ref[rows, :], v_ref[rows, :]], axis=-1)
        top += _dot((x + pos_ref[l:l + 1, :]).astype(BF16), bd1_ref[l])
        bot += _dot((x + pos_ref[S + l:S + l + 1, :]).astype(BF16), bd1_ref[S + l])
    pre = top + pltpu.roll(bot, n16 - 1, axis=0)
    out = _dot(_silu(pre).astype(BF16), bd2_ref[...])
    row = lax.broadcasted_iota(jnp.int32, (n16, KV_W), 0)
    o_ref[...] = jnp.where(row < n16 - 1, out, 0.0)


def nsa_compress(proj3, cmp_pos, cmp_w1, cmp_w2):
    bsz, T, _ = proj3.shape
    n16 = T // NSA_CMP_STRIDE
    sel = np.array([0, 0, 1, 1])
    eye = jnp.eye(4, dtype=F32)
    w1r = cmp_w1.reshape(2, NSA_CMP_LEN, NSA_DH, NSA_DH)[sel]
    bd1 = jnp.einsum('ab,alde->ladbe', eye, w1r).reshape(NSA_CMP_LEN, KV_W, KV_W).astype(BF16)
    bd2 = jnp.einsum('ab,ade->adbe', eye, cmp_w2[sel]).reshape(KV_W, KV_W).astype(BF16)
    pos = jnp.concatenate([cmp_pos[0], cmp_pos[0], cmp_pos[1], cmp_pos[1]], axis=-1)
    return pl.pallas_call(
        functools.partial(_nsa_compress_kernel, n16=n16),
        grid=(bsz,),
        in_specs=[pl.BlockSpec((None, T, NSA_KVW), lambda b: (b, 0, OFF_NKV // NSA_KVW)),
                  pl.BlockSpec((None, T, NSA_KVW), lambda b: (b, 0, OFF_NKV // NSA_KVW + 1)),
                  pl.BlockSpec((NSA_CMP_LEN, KV_W), lambda b: (0, 0)),
                  pl.BlockSpec((NSA_CMP_LEN, KV_W, KV_W), lambda b: (0, 0, 0)),
                  pl.BlockSpec((KV_W, KV_W), lambda b: (0, 0))],
        out_specs=pl.BlockSpec((None, n16, KV_W), lambda b: (b, 0, 0)),
        out_shape=jax.ShapeDtypeStruct((bsz, n16, KV_W), F32),
        compiler_params=_cparams(("parallel",)),
        name="nsa_compress",
    )(proj3, proj3, pos, bd1, bd2)


def _stack_queries(q_ref, qc_ref, qa_ref, g, tq, mid=None):
    for h in range(NSA_HPG):
        hh = g * NSA_HPG + h
        parts = [(q_ref[:, _head_cols(hh)] * NSA_SCALE).astype(BF16)]
        if mid is not None:
            parts.append(mid)
        parts.append(jnp.broadcast_to(qc_ref[hh:hh + 1, :], (tq, EXTRA_W)))
        qa_ref[h * tq:(h + 1) * tq, :] = jnp.concatenate(parts, axis=-1)
    return qa_ref[...]


def _nsa_cmp_topk_kernel(q_ref, kcv_ref, cpos_ref, qc_ref, sm_ref, ov_ref, o_ref, sel_ref, act_ref, qa_ref, imp_ref,
                         *, tq, n16, n_sel, n_top, cw):
    t0 = pl.program_id(1) * tq
    tpos = t0 + lax.broadcasted_iota(jnp.int32, (1, tq), 1)
    any_valid = (tpos >= NSA_CMP_LEN - 1).astype(F32)
    gates = 1.0 / (1.0 + jnp.exp(-sm_ref[...]))
    blk = lax.broadcasted_iota(jnp.int32, (n_sel, 1), 0)
    blk_f = blk.astype(F32)
    cur = tpos // NSA_SEL_BLOCK
    forced = (blk == 0) | (blk == cur) | (blk == cur - 1)
    future = blk * NSA_SEL_BLOCK > tpos
    blk_row = lax.broadcasted_iota(jnp.int32, (1, n_sel), 1)

    def attend(nc):
        cmp_end = lax.broadcasted_iota(jnp.int32, (nc, 1), 0) * NSA_CMP_STRIDE + (NSA_CMP_LEN - 1)
        bias = jnp.where(cmp_end <= tpos, 0.0, -BIG)
        for g in range(NSA_GROUPS):
            kc = jnp.concatenate([kcv_ref[:nc, g * NSA_DH:(g + 1) * NSA_DH].astype(BF16), cpos_ref[:nc, :]], axis=-1)
            vc_t = kcv_ref[:nc, NSA_KVW + g * NSA_DH:NSA_KVW + (g + 1) * NSA_DH].T.astype(BF16)
            _stack_queries(q_ref, qc_ref, qa_ref, g, tq)
            psum = None
            for h in range(NSA_HPG):
                hh = g * NSA_HPG + h
                s = _dot_nt(kc, qa_ref[h * tq:(h + 1) * tq, :]) + bias
                e = jnp.exp(s - jnp.max(s, axis=0, keepdims=True))
                p = e * (any_valid / jnp.sum(e, axis=0, keepdims=True))
                psum = p if psum is None else psum + p
                o_ref[:, _head_cols(hh)] = gates[:, SM_NG + hh:SM_NG + hh + 1] * _dot(vc_t, p.astype(BF16)).T
            imp_ref[g] = _dot01_left(ov_ref[:, :nc], psum)

    need = (t0 + tq) // NSA_CMP_STRIDE
    nchunks = n16 // cw
    for k in range(1, nchunks + 1):
        lo = (k - 1) * cw
        cond = (need > lo) & (need <= k * cw) if k < nchunks else need > lo
        pl.when(cond)(functools.partial(attend, k * cw))

    for g in range(NSA_GROUPS):
        work = jnp.where(forced, BIG, jnp.where(future, -BIG, imp_ref[g]))
        sel_t = jnp.zeros((n_sel, tq), F32)
        for _ in range(n_top):
            top = jnp.max(work, axis=0, keepdims=True)
            idx = jnp.min(jnp.where(work == top, blk_f, float(n_sel)), axis=0, keepdims=True)
            pick = blk_f == idx
            sel_t = jnp.where(pick, 1.0, sel_t)
            work = jnp.where(pick, NEG_HUGE, work)
        sel = sel_t.T
        sel_ref[:, g * n_sel:(g + 1) * n_sel] = sel.astype(sel_ref.dtype)
        union = jnp.max(sel, axis=0, keepdims=True)
        act_ref[:, g * n_sel:(g + 1) * n_sel] = jnp.where(blk_row * NSA_SEL_BLOCK < t0, union, 0.0)


def _nsa_overlap(n16, n_sel):
    n_cmp = n16 - 1
    tok = (np.arange(n_cmp) * NSA_CMP_STRIDE)[:, None] + np.arange(NSA_CMP_LEN)[None, :]
    ov = np.zeros((n16, n_sel), np.float32)
    np.add.at(ov, (np.repeat(np.arange(n_cmp), NSA_CMP_LEN), (tok // NSA_SEL_BLOCK).ravel()), 1.0 / NSA_CMP_LEN)
    return jnp.asarray(ov.T, BF16)


def nsa_cmp_topk(proj3, kcv, tq=256):
    bsz, T, _ = proj3.shape
    tq = min(tq, T)
    n16 = T // NSA_CMP_STRIDE
    n_sel = T // NSA_SEL_BLOCK
    n_top = min(NSA_TOPN, n_sel)
    cmp_end = np.arange(n16) * NSA_CMP_STRIDE + NSA_CMP_LEN - 1
    cpos = jnp.asarray(_pos_cols(cmp_end, np.ones(n16, bool)), BF16)
    return pl.pallas_call(
        functools.partial(_nsa_cmp_topk_kernel, tq=tq, n16=n16, n_sel=n_sel, n_top=n_top, cw=min(LANES, n16)),
        grid=(bsz, T // tq),
        in_specs=[pl.BlockSpec((None, tq, NSA_QW), lambda b, i: (b, i, OFF_NQ // NSA_QW)),
                  pl.BlockSpec((None, n16, KV_W), lambda b, i: (b, 0, 0)),
                  pl.BlockSpec((n16, EXTRA_W), lambda b, i: (0, 0)),
                  pl.BlockSpec((NSA_HEADS, EXTRA_W), lambda b, i: (0, 0)),
                  pl.BlockSpec((None, tq, LANES), lambda b, i: (b, i, OFF_SMALL // LANES)),
                  pl.BlockSpec((n_sel, n16), lambda b, i: (0, 0))],
        out_specs=[pl.BlockSpec((None, tq, NSA_QW), lambda b, i: (b, i, 0)),
                   pl.BlockSpec((None, tq, NSA_GROUPS * n_sel), lambda b, i: (b, i, 0)),
                   pl.BlockSpec((None, None, 1, NSA_GROUPS * n_sel), lambda b, i: (b, i, 0, 0))],
        out_shape=[jax.ShapeDtypeStruct((bsz, T, NSA_QW), F32),
                   jax.ShapeDtypeStruct((bsz, T, NSA_GROUPS * n_sel), BF16),
                   jax.ShapeDtypeStruct((bsz, T // tq, 1, NSA_GROUPS * n_sel), F32)],
        scratch_shapes=[pltpu.VMEM((NSA_HPG * tq, NSA_DH + EXTRA_W), BF16),
                        pltpu.VMEM((NSA_GROUPS, n_sel, tq), F32)],
        compiler_params=_cparams(("parallel", "parallel")),
        name="nsa_cmp_topk",
    )(proj3, kcv, cpos, _nsa_query_consts(), proj3, _nsa_overlap(n16, n_sel))


def _nsa_window_kernel(q_ref, k0_ref, k1_ref, k2_ref, v0_ref, v1_ref, v2_ref, qc_ref, sm_ref, prev_ref, o_ref,
                       qa_ref, *, tq):
    nb = NSA_WINDOW // tq + 1
    i = pl.program_id(1)
    tpos = i * tq + lax.broadcasted_iota(jnp.int32, (1, tq), 1)
    kpos = (i - (nb - 1)) * tq + lax.broadcasted_iota(jnp.int32, (nb * tq, 1), 0)
    d = tpos - kpos
    valid = (d >= 0) & (d < NSA_WINDOW) & (kpos >= 0)
    hpc = NSA_HPG // WIN_CHAINS
    bias = jnp.concatenate([jnp.where(valid, 0.0, -BIG)] * hpc, axis=1)
    gates = 1.0 / (1.0 + jnp.exp(-sm_ref[...]))
    for g in range(NSA_GROUPS):
        kw = jnp.concatenate([k0_ref[g], k1_ref[g], k2_ref[g]], axis=0)
        vw = jnp.concatenate([v0_ref[g], v1_ref[g], v2_ref[g]], axis=0)
        vw_t = vw.astype(F32).T.astype(BF16)
        _stack_queries(q_ref, qc_ref, qa_ref, g, tq)
        for c in range(WIN_CHAINS):
            s = _dot_nt(kw, qa_ref[c * hpc * tq:(c + 1) * hpc * tq, :]) + bias
            e = jnp.exp(s - jnp.max(s, axis=0, keepdims=True))
            acc = _dot(vw_t, e.astype(BF16))
            o = (acc[:NSA_DH, :] * (1.0 / acc[NSA_DH:NSA_DH + 1, :])).T
            for h in range(hpc):
                hh = g * NSA_HPG + c * hpc + h
                gate = gates[:, SM_NG + 2 * NSA_HEADS + hh:SM_NG + 2 * NSA_HEADS + hh + 1]
                o_ref[:, _head_cols(hh)] = prev_ref[:, _head_cols(hh)] + gate * o[h * tq:(h + 1) * tq]


def nsa_window(proj3, prev, tq=256):
    bsz, T, _ = proj3.shape
    tq = min(tq, T)
    assert NSA_WINDOW % tq == 0 and NSA_WINDOW // tq == 2
    kc, vc = _nsa_key_consts(T, tq, onehot=False)
    kaug, vaug = nsa_kprep(proj3, kc, vc, tq, (OFF_NKV + 2 * KV_W) // KV_W)
    wk = kaug.shape[-1]

    def kvspec(back, w):
        return pl.BlockSpec((None, NSA_GROUPS, tq, w), lambda b, i, back=back: (b, 0, jnp.maximum(i - back, 0), 0))

    return pl.pallas_call(
        functools.partial(_nsa_window_kernel, tq=tq),
        grid=(bsz, T // tq),
        in_specs=[pl.BlockSpec((None, tq, NSA_QW), lambda b, i: (b, i, OFF_NQ // NSA_QW)),
                  kvspec(2, wk), kvspec(1, wk), kvspec(0, wk),
                  kvspec(2, 2 * NSA_DH), kvspec(1, 2 * NSA_DH), kvspec(0, 2 * NSA_DH),
                  pl.BlockSpec((NSA_HEADS, EXTRA_W), lambda b, i: (0, 0)),
                  pl.BlockSpec((None, tq, LANES), lambda b, i: (b, i, OFF_SMALL // LANES)),
                  pl.BlockSpec((None, tq, NSA_QW), lambda b, i: (b, i, 0))],
        out_specs=pl.BlockSpec((None, tq, NSA_QW), lambda b, i: (b, i, 0)),
        out_shape=jax.ShapeDtypeStruct((bsz, T, NSA_QW), F32),
        scratch_shapes=[pltpu.VMEM((NSA_HPG * tq, wk), BF16)],
        compiler_params=_cparams(("parallel", "parallel")),
        name="nsa_window",
    )(proj3, kaug, kaug, kaug, vaug, vaug, vaug, _nsa_query_consts(), proj3, prev)


MASK_BIG = 1e30
SEL_STEP = 8
NSA_CHAINS = 1
WIN_CHAINS = 2
POS_HI = 128
EXTRA_W = NSA_DH


def _bf16_pieces(x):
    x = np.float32(x)
    out = []
    for _ in range(3):
        p = np.float32(np.asarray(x, dtype=jnp.bfloat16))
        out.append(p)
        x = np.float32(x - p)
    return out


def _pos_cols(pos, real):
    c = np.zeros((len(pos), EXTRA_W), np.float32)
    for j in range(3):
        c[real, j] = (pos[real] // POS_HI) * POS_HI
        c[real, 3 + j] = pos[real] % POS_HI
    c[~real, 6] = 1.0
    return c


def _nsa_query_consts():
    qc = np.zeros((NSA_HEADS, EXTRA_W), np.float32)
    for hh in range(NSA_HEADS):
        qc[hh, 0:3] = qc[hh, 3:6] = _bf16_pieces(NSA_SLOPES[hh])
        qc[hh, 6] = -MASK_BIG
    return jnp.asarray(qc, BF16)


def _nsa_key_consts(T, pad, onehot):
    n_sel = T // NSA_SEL_BLOCK
    pos = np.arange(T + pad)
    real = pos < T
    kc = _pos_cols(pos, real)
    if onehot:
        oh = np.zeros((T + pad, n_sel), np.float32)
        oh[pos[real], pos[real] // NSA_SEL_BLOCK] = 1.0
        kc = np.concatenate([oh, kc], axis=1)
    vc = np.zeros((T + pad, NSA_DH), np.float32)
    vc[real, 0] = 1.0
    return jnp.asarray(kc, BF16), jnp.asarray(vc, BF16)


def _nsa_kprep_kernel(kv_ref, kc_ref, vc_ref, ka_ref, va_ref):
    real = pl.program_id(1) < pl.num_programs(1) - 1
    kv = (kv_ref[...] * jnp.where(real, 1.0, 0.0)).astype(BF16)
    for g in range(NSA_GROUPS):
        ka_ref[g] = jnp.concatenate([kv[:, g * NSA_DH:(g + 1) * NSA_DH], kc_ref[...]], axis=-1)
        va_ref[g] = jnp.concatenate([kv[:, NSA_KVW + g * NSA_DH:NSA_KVW + (g + 1) * NSA_DH], vc_ref[...]], axis=-1)


def nsa_kprep(proj3, kc, vc, tb, cs):
    bsz, T, _ = proj3.shape
    nt = T // tb
    wk = NSA_DH + kc.shape[1]
    return pl.pallas_call(
        _nsa_kprep_kernel,
        grid=(bsz, nt + 1),
        in_specs=[pl.BlockSpec((None, tb, KV_W), lambda b, t: (b, jnp.minimum(t, nt - 1), cs)),
                  pl.BlockSpec((tb, kc.shape[1]), lambda b, t: (t, 0)),
                  pl.BlockSpec((tb, NSA_DH), lambda b, t: (t, 0))],
        out_specs=[pl.BlockSpec((None, NSA_GROUPS, tb, wk), lambda b, t: (b, 0, t, 0)),
                   pl.BlockSpec((None, NSA_GROUPS, tb, 2 * NSA_DH), lambda b, t: (b, 0, t, 0))],
        out_shape=[jax.ShapeDtypeStruct((bsz, NSA_GROUPS, T + tb, wk), BF16),
                   jax.ShapeDtypeStruct((bsz, NSA_GROUPS, T + tb, 2 * NSA_DH), BF16)],
        compiler_params=_cparams(("parallel", "parallel")),
        name="nsa_kprep",
    )(proj3, kc, vc)


def _nsa_select_kernel(ids_ref, cnt_ref, q_ref, ka_ref, va_ref, sel_ref, qc_ref, sm_ref, prev_ref, o_ref,
                       qa_ref, kt_ref, vt_ref, m_ref, acc_ref, *, tq, n_sel, lmax):
    b = pl.program_id(0)
    i = pl.program_id(1)
    B = NSA_SEL_BLOCK
    kpos = i * tq + lax.broadcasted_iota(jnp.int32, (tq, 1), 0)
    tpos = i * tq + lax.broadcasted_iota(jnp.int32, (1, tq), 1)
    causal_bias = jnp.where(kpos <= tpos, 0.0, -BIG)
    hpc = NSA_HPG // NSA_CHAINS
    chains = [slice(c * hpc * tq, (c + 1) * hpc * tq) for c in range(NSA_CHAINS)]
    causal_bias = jnp.concatenate([causal_bias] * hpc, axis=1)
    gates = 1.0 / (1.0 + jnp.exp(-sm_ref[...]))
    own = pl.ds(pl.multiple_of(i * tq, tq), tq)
    for g in range(NSA_GROUPS):
        mcols = ((sel_ref[:, g * n_sel:(g + 1) * n_sel].astype(F32) - 1.0) * MASK_BIG).astype(BF16)
        _stack_queries(q_ref, qc_ref, qa_ref, g, tq, mid=mcols)
        v_own = va_ref[g, own, :].astype(F32).T.astype(BF16)
        for rs in chains:
            s = _dot_nt(ka_ref[g, own, :], qa_ref[rs, :]) + causal_bias
            m0 = jnp.max(s, axis=0, keepdims=True)
            m_ref[:, rs] = m0
            acc_ref[:, rs] = _dot(v_own, jnp.exp(s - m0).astype(BF16))
        slot = (b * pl.num_programs(1) + i) * NSA_GROUPS + g

        def step(st, carry):
            for u in range(SEL_STEP):
                r0 = pl.multiple_of(ids_ref[slot * lmax + st * SEL_STEP + u] * B, B)
                kt_ref[u * B:(u + 1) * B, :] = ka_ref[g, pl.ds(r0, B), :]
                vt_ref[u * B:(u + 1) * B, :] = va_ref[g, pl.ds(r0, B), :]
            v_t = vt_ref[...].astype(F32).T.astype(BF16)
            for rs in chains:
                s = _dot_nt(kt_ref[...], qa_ref[rs, :])
                m_old = m_ref[:, rs]
                m_new = jnp.maximum(m_old, jnp.max(s, axis=0, keepdims=True))
                p = jnp.exp(s - m_new).astype(BF16)
                acc_ref[:, rs] = jnp.exp(m_old - m_new) * acc_ref[:, rs] + _dot(v_t, p)
                m_ref[:, rs] = m_new
            return carry

        lax.fori_loop(0, (cnt_ref[slot] + SEL_STEP - 1) // SEL_STEP, step, 0)
        acc = acc_ref[...]
        o = (acc[:NSA_DH, :] * (1.0 / acc[NSA_DH:NSA_DH + 1, :])).T
        for h in range(NSA_HPG):
            hh = g * NSA_HPG + h
            gate = gates[:, SM_NG + NSA_HEADS + hh:SM_NG + NSA_HEADS + hh + 1]
            o_ref[:, _head_cols(hh)] = (prev_ref[:, _head_cols(hh)] + gate * o[h * tq:(h + 1) * tq, :]
                                        ).astype(o_ref.dtype)


def nsa_select(proj3, sel, act, prev, tq=256):
    bsz, T, _ = proj3.shape
    tq = min(tq, T)
    nt = T // tq
    n_sel = T // NSA_SEL_BLOCK
    lmax = -(-n_sel // SEL_STEP) * SEL_STEP
    kc, vc = _nsa_key_consts(T, tq, onehot=True)
    qc = _nsa_query_consts()
    kaug, vaug = nsa_kprep(proj3, kc, vc, tq, (OFF_NKV + KV_W) // KV_W)
    wk = kaug.shape[-1]
    on = (act.reshape(bsz, nt, NSA_GROUPS, n_sel) > 0.0).astype(jnp.int32)
    seen = jnp.cumsum(on, axis=-1)
    ids = jnp.sum(seen[..., None, :] <= jnp.arange(lmax, dtype=jnp.int32)[:, None], axis=-1, dtype=jnp.int32)
    ids = ids.reshape(-1)
    cnt = seen[..., -1].reshape(-1)
    rows = NSA_HPG * tq
    grid_spec = pltpu.PrefetchScalarGridSpec(
        num_scalar_prefetch=2,
        grid=(bsz, nt),
        in_specs=[pl.BlockSpec((None, tq, NSA_QW), lambda b, i, *_: (b, i, OFF_NQ // NSA_QW)),
                  pl.BlockSpec((None, NSA_GROUPS, T + tq, wk), lambda b, i, *_: (b, 0, 0, 0)),
                  pl.BlockSpec((None, NSA_GROUPS, T + tq, 2 * NSA_DH), lambda b, i, *_: (b, 0, 0, 0)),
                  pl.BlockSpec((None, tq, NSA_GROUPS * n_sel), lambda b, i, *_: (b, i, 0)),
                  pl.BlockSpec((NSA_HEADS, EXTRA_W), lambda b, i, *_: (0, 0)),
                  pl.BlockSpec((None, tq, LANES), lambda b, i, *_: (b, i, OFF_SMALL // LANES)),
                  pl.BlockSpec((None, tq, NSA_QW), lambda b, i, *_: (b, i, 0))],
        out_specs=pl.BlockSpec((None, tq, NSA_QW), lambda b, i, *_: (b, i, 0)),
        scratch_shapes=[pltpu.VMEM((rows, wk), BF16),
                        pltpu.VMEM((SEL_STEP * NSA_SEL_BLOCK, wk), BF16),
                        pltpu.VMEM((SEL_STEP * NSA_SEL_BLOCK, 2 * NSA_DH), BF16),
                        pltpu.VMEM((1, rows), F32), pltpu.VMEM((2 * NSA_DH, rows), F32)])
    return pl.pallas_call(
        functools.partial(_nsa_select_kernel, tq=tq, n_sel=n_sel, lmax=lmax),
        grid_spec=grid_spec,
        out_shape=jax.ShapeDtypeStruct((bsz, T, NSA_QW), BF16),
        compiler_params=_cparams(("parallel", "arbitrary")),
        name="nsa_select",
    )(ids, cnt, proj3, kaug, vaug, sel, qc, proj3, prev)


def nsa_mixer(proj3, cmp_pos, cmp_w1, cmp_w2):
    kcv = nsa_compress(proj3, cmp_pos, cmp_w1, cmp_w2)
    o_cmp, sel, act = nsa_cmp_topk(proj3, kcv)
    o_cw = nsa_window(proj3, o_cmp)
    return nsa_select(proj3, sel, act, o_cw)


def _pack_w_in(w_in):
    w_in = w_in.astype(BF16)
    (g_q, g_k, g_v, g_r, g_a, n_q, n_kv, n_g, s_z, s_xbc, s_dt, m_g) = jnp.split(w_in, SPLIT_POINTS, axis=-1)
    pad = jnp.zeros(w_in.shape[:-1] + (D_PK - OFF_SMALL - GLA_RANK - 3 * NSA_HEADS - SSM_HEADS,), w_in.dtype)
    return jnp.concatenate([m_g, g_q, g_k, g_v, g_r, n_q, s_z, s_xbc, n_kv, g_a, n_g, s_dt, pad], axis=-1)


def kernel(x, w_in, gla_a2, gla_a_bias, gla_norm, nsa_cmp_pos, nsa_cmp_w1, nsa_cmp_w2, ssm_conv_w, ssm_conv_b,
           ssm_dt_bias, ssm_a_log, ssm_d, ssm_norm, w_branch, w_out, norm_pre_mix, norm_post_mix, norm_pre_ffn,
           norm_post_ffn, w_ffn_gate, w_ffn_up, w_ffn_down):
    bsz, T, D = x.shape
    depth = w_in.shape[0]
    n = bsz * T
    w_in_pk = _pack_w_in(w_in)
    wa_pad = jnp.zeros((depth, LANES, GLA_KW), F32).at[:, SM_GA:SM_GA + GLA_RANK].set(gla_a2).astype(BF16)
    w_branch_b = w_branch.astype(BF16)
    w_out_b = w_out.astype(BF16)
    w_gate_b = w_ffn_gate.astype(BF16)
    w_up_b = w_ffn_up.astype(BF16)
    w_down_b = w_ffn_down.astype(BF16)
    xf = x.reshape(n, D)
    for l in range(depth):
        proj = norm_matmul(xf, norm_pre_mix[l], w_in_pk[l])
        proj3 = proj.reshape(bsz, T, D_PK)
        y_gla = gla_mixer(proj3, wa_pad[l], gla_a_bias[l], gla_norm[l])
        y_nsa = nsa_mixer(proj3, nsa_cmp_pos[l], nsa_cmp_w1[l], nsa_cmp_w2[l])
        y_ssm = ssd_mixer(proj3, ssm_conv_w[l], ssm_conv_b[l], ssm_dt_bias[l], ssm_a_log[l], ssm_d[l], ssm_norm[l])
        merged = merge_branches(y_gla.reshape(n, BRANCH_W), y_nsa.reshape(n, BRANCH_W), y_ssm.reshape(n, BRANCH_W),
                                w_branch_b[l], proj)
        xf = proj_norm_residual(merged, w_out_b[l], xf, norm_post_mix[l])
        act = ffn_up(xf, norm_pre_ffn[l], w_gate_b[l], w_up_b[l])
        xf = proj_norm_residual(act, w_down_b[l], xf, norm_post_ffn[l])
    return xf.reshape(bsz, T, D)
```

```python
import functools

import numpy as np
import jax
import jax.numpy as jnp
from jax import lax
from jax.experimental import pallas as pl
from jax.experimental.pallas import tpu as pltpu

F32 = jnp.float32
BF16 = jnp.bfloat16

D_MODEL = 2048
EPS = 1e-6
N_BRANCH = 3
BRANCH_W = 1024
GLA_HEADS, GLA_DK, GLA_DV, GLA_RANK, GLA_TAU, GLA_CHUNK = 4, 256, 256, 16, 16.0, 64
GLA_KW = GLA_HEADS * GLA_DK
GLA_VW = GLA_HEADS * GLA_DV
NSA_HEADS, NSA_GROUPS, NSA_DH = 16, 2, 64
NSA_HPG = NSA_HEADS // NSA_GROUPS
NSA_QW = NSA_HEADS * NSA_DH
NSA_KVW = NSA_GROUPS * NSA_DH
NSA_CMP_LEN, NSA_CMP_STRIDE, NSA_SEL_BLOCK, NSA_TOPN, NSA_WINDOW = 32, 16, 64, 16, 512
BIG = 1e30
SSM_HEADS, SSM_HEADDIM, SSM_GROUPS, SSM_STATE, SSM_CONV, SSM_CHUNK = 16, 64, 4, 128, 4, 64
SSM_INNER = SSM_HEADS * SSM_HEADDIM
SSM_HPG = SSM_HEADS // SSM_GROUPS
SSM_CONV_CH = SSM_INNER + 2 * SSM_GROUPS * SSM_STATE
D_FF = ((8 * D_MODEL // 3 + 255) // 256) * 256
IN_SIZES = (GLA_KW, GLA_KW, GLA_VW, GLA_VW, GLA_RANK, NSA_QW, 6 * NSA_KVW, 3 * NSA_HEADS,
            SSM_INNER, SSM_CONV_CH, SSM_HEADS, N_BRANCH * D_MODEL)
SPLIT_POINTS = tuple(int(v) for v in np.cumsum(IN_SIZES)[:-1])

LANES = 128
VMEM_LIMIT = 56 * 1024 * 1024

OFF_MG = 0
OFF_GQ = OFF_MG + N_BRANCH * D_MODEL
OFF_GK = OFF_GQ + GLA_KW
OFF_GV = OFF_GK + GLA_KW
OFF_GR = OFF_GV + GLA_VW
OFF_NQ = OFF_GR + GLA_VW
OFF_SZ = OFF_NQ + NSA_QW
OFF_SX = OFF_SZ + SSM_INNER
OFF_SB = OFF_SX + SSM_INNER
OFF_SC = OFF_SB + SSM_GROUPS * SSM_STATE
OFF_NKV = OFF_SC + SSM_GROUPS * SSM_STATE
OFF_SMALL = OFF_NKV + 6 * NSA_KVW
D_PK = OFF_SMALL + 2 * LANES
SM_GA = 0
SM_NG = 16
SM_DT = 64


def _cparams(sem):
    return pltpu.CompilerParams(dimension_semantics=sem, vmem_limit_bytes=VMEM_LIMIT)


def _split3(x):
    hi = x.astype(BF16)
    r1 = x - hi.astype(F32)
    mid = r1.astype(BF16)
    lo = (r1 - mid.astype(F32)).astype(BF16)
    return hi, mid, lo


def _dot(a, b):
    return jnp.dot(a, b, preferred_element_type=F32)


def _dot_nt(a, b):
    return lax.dot_general(a, b, (((1,), (1,)), ((), ())), preferred_element_type=F32)


def _dot01_left(m01, x):
    hi, mid, lo = _split3(x)
    return _dot(m01, hi) + _dot(m01, mid) + _dot(m01, lo)


def _dot01_right(x, m01):
    hi, mid, lo = _split3(x)
    return _dot(hi, m01) + _dot(mid, m01) + _dot(lo, m01)


def _silu(x):
    return x / (1.0 + jnp.exp(-x))


def _norm_rows(x_ref, g_ref, h_ref):
    @pl.when(pl.program_id(1) == 0)
    def _():
        x = x_ref[...]
        y = x * lax.rsqrt(jnp.mean(x * x, axis=-1, keepdims=True) + EPS)
        h_ref[...] = (y * g_ref[...]).astype(h_ref.dtype)


def _norm_mm_kernel(x_ref, g_ref, w_ref, o_ref, h_ref):
    _norm_rows(x_ref, g_ref, h_ref)
    o_ref[...] = _dot(h_ref[...], w_ref[...]).astype(o_ref.dtype)


def norm_matmul(x, g, w, out_dtype=F32, tm=1024, tn=1024):
    m, k = x.shape
    n = w.shape[1]
    tm = min(tm, m)
    tn = min(tn, n)
    return pl.pallas_call(
        _norm_mm_kernel,
        grid=(m // tm, n // tn),
        in_specs=[pl.BlockSpec((tm, k), lambda i, j: (i, 0)), pl.BlockSpec((1, k), lambda i, j: (0, 0)),
                  pl.BlockSpec((k, tn), lambda i, j: (0, j))],
        out_specs=pl.BlockSpec((tm, tn), lambda i, j: (i, j)),
        out_shape=jax.ShapeDtypeStruct((m, n), out_dtype),
        scratch_shapes=[pltpu.VMEM((tm, k), BF16)],
        compiler_params=_cparams(("parallel", "arbitrary")),
        name="in_proj",
    )(x, g.reshape(1, k), w)


def _merge_kernel(yg_ref, yn_ref, ys_ref, wg_ref, wn_ref, ws_ref, g0_ref, g1_ref, g2_ref, o_ref):
    def gate(ref):
        return 1.0 / (1.0 + jnp.exp(-ref[...]))

    acc = gate(g0_ref) * _dot(yg_ref[...], wg_ref[...])
    acc += gate(g1_ref) * _dot(yn_ref[...], wn_ref[...])
    acc += gate(g2_ref) * _dot(ys_ref[...], ws_ref[...])
    o_ref[...] = acc.astype(o_ref.dtype)


def merge_branches(y_gla, y_nsa, y_ssm, w_branch, proj, tm=1024, tn=512):
    m = y_gla.shape[0]
    d = w_branch.shape[-1]
    tm = min(tm, m)
    nj = d // tn
    ys = pl.BlockSpec((tm, BRANCH_W), lambda i, j: (i, 0))

    def wspec(b):
        return pl.BlockSpec((None, BRANCH_W, tn), lambda i, j, b=b: (b, 0, j))

    def gspec(b):
        return pl.BlockSpec((tm, tn), lambda i, j, b=b: (i, (OFF_MG + b * D_MODEL) // tn + j))

    return pl.pallas_call(
        _merge_kernel,
        grid=(m // tm, nj),
        in_specs=[ys, ys, ys, wspec(0), wspec(1), wspec(2), gspec(0), gspec(1), gspec(2)],
        out_specs=pl.BlockSpec((tm, tn), lambda i, j: (i, j)),
        out_shape=jax.ShapeDtypeStruct((m, d), BF16),
        compiler_params=_cparams(("parallel", "parallel")),
        name="merge",
    )(y_gla, y_nsa, y_ssm, w_branch, w_branch, w_branch, proj, proj, proj)


def _proj_norm_res_kernel(a_ref, w_ref, x_ref, g_ref, o_ref, acc_ref):
    k = pl.program_id(1)

    @pl.when(k == 0)
    def _():
        acc_ref[...] = jnp.zeros_like(acc_ref)

    acc_ref[...] += _dot(a_ref[...], w_ref[...])

    @pl.when(k == pl.num_programs(1) - 1)
    def _():
        f = acc_ref[...]
        y = f * lax.rsqrt(jnp.mean(f * f, axis=-1, keepdims=True) + EPS)
        o_ref[...] = x_ref[...] + y * g_ref[...]


def proj_norm_residual(a, w, x, g, tm=1024, tk=512):
    m, kk = a.shape
    d = w.shape[1]
    tm = min(tm, m)
    return pl.pallas_call(
        _proj_norm_res_kernel,
        grid=(m // tm, kk // tk),
        in_specs=[pl.BlockSpec((tm, tk), lambda i, k: (i, k)),
                  pl.BlockSpec((tk, d), lambda i, k: (k, 0)),
                  pl.BlockSpec((tm, d), lambda i, k: (i, 0)),
                  pl.BlockSpec((1, d), lambda i, k: (0, 0))],
        out_specs=pl.BlockSpec((tm, d), lambda i, k: (i, 0)),
        out_shape=jax.ShapeDtypeStruct((m, d), F32),
        scratch_shapes=[pltpu.VMEM((tm, d), F32)],
        compiler_params=_cparams(("parallel", "arbitrary")),
        name="proj_norm_res",
    )(a, w, x, g.reshape(1, d))


def _ffn_up_kernel(x_ref, g_ref, wg_ref, wu_ref, o_ref, h_ref):
    _norm_rows(x_ref, g_ref, h_ref)
    h = h_ref[...]
    a = _dot(h, wg_ref[...])
    u = _dot(h, wu_ref[...])
    o_ref[...] = (_silu(a) * u).astype(o_ref.dtype)


def ffn_up(x, g, wg, wu, tm=1024, tn=512):
    m, k = x.shape
    n = wg.shape[1]
    tm = min(tm, m)
    return pl.pallas_call(
        _ffn_up_kernel,
        grid=(m // tm, n // tn),
        in_specs=[pl.BlockSpec((tm, k), lambda i, j: (i, 0)),
                  pl.BlockSpec((1, k), lambda i, j: (0, 0)),
                  pl.BlockSpec((k, tn), lambda i, j: (0, j)),
                  pl.BlockSpec((k, tn), lambda i, j: (0, j))],
        out_specs=pl.BlockSpec((tm, tn), lambda i, j: (i, j)),
        out_shape=jax.ShapeDtypeStruct((m, n), BF16),
        scratch_shapes=[pltpu.VMEM((tm, k), BF16)],
        compiler_params=_cparams(("parallel", "arbitrary")),
        name="ffn_up",
    )(x, g.reshape(1, k), wg, wu)


def _gla_kernel(q_ref, k_ref, v_ref, r_ref, sm_ref, wa_ref, ba_ref, ng_ref, tri_ref, same_ref, o_ref,
                st_ref, *, tb):
    C = GLA_CHUNK

    @pl.when(pl.program_id(1) == 0)
    def _():
        st_ref[...] = jnp.zeros_like(st_ref)

    pre = _dot(sm_ref[...].astype(BF16), wa_ref[...]) + ba_ref[...]
    la = (jnp.minimum(pre, 0.0) - jnp.log1p(jnp.exp(-jnp.abs(pre)))) * (1.0 / GLA_TAU)
    tri = tri_ref[...]
    same = same_ref[...]
    ri = lax.broadcasted_iota(jnp.int32, (tb, tb), 0)
    ci = lax.broadcasted_iota(jnp.int32, (tb, tb), 1)
    causal = (ri >= ci) & (ri // C == ci // C)
    for h in range(GLA_HEADS):
        ck = slice(h * GLA_DK, (h + 1) * GLA_DK)
        cv = slice(h * GLA_DV, (h + 1) * GLA_DV)
        hi, mid, lo = _split3(la[:, ck])
        bcum = _dot(tri, hi) + _dot(tri, mid) + _dot(tri, lo)
        b_last = _dot(same, hi) + _dot(same, mid) + _dot(same, lo)
        q = q_ref[:, ck] * (GLA_DK ** -0.5)
        k = k_ref[:, ck]
        v = v_ref[:, cv]
        q_dec = (q * jnp.exp(bcum)).astype(BF16)
        k_inv = (k * jnp.exp(-bcum)).astype(BF16)
        k_end = (k * jnp.exp(b_last - bcum)).astype(BF16)
        att = jnp.where(causal, _dot_nt(q_dec, k_inv), 0.0)
        o_intra = _dot(att.astype(BF16), v.astype(BF16))
        st = st_ref[h]
        outs = []
        for c in range(tb // C):
            rows = slice(c * C, (c + 1) * C)
            outs.append(o_intra[rows] + _dot_nt(q_dec[rows], st.astype(BF16)))
            st = jnp.exp(b_last[c * C:c * C + 1, :]) * st + _dot(v[rows].T.astype(BF16), k_end[rows])
        st_ref[h] = st
        o = jnp.concatenate(outs, axis=0)
        y = o * lax.rsqrt(jnp.mean(o * o, axis=-1, keepdims=True) + EPS) * ng_ref[...]
        o_ref[:, cv] = (y * _silu(r_ref[:, cv])).astype(o_ref.dtype)


def _chunk_masks(tb, chunk):
    idx = np.arange(tb)
    same = (idx[:, None] // chunk) == (idx[None, :] // chunk)
    return jnp.asarray(same & (idx[:, None] >= idx[None, :]), BF16), jnp.asarray(same, BF16)


def gla_mixer(proj3, wa_pad, ba, norm_g, tb=256):
    bsz, T, _ = proj3.shape
    tb = min(tb, T)
    tri, same = _chunk_masks(tb, GLA_CHUNK)

    def col(off, w):
        return pl.BlockSpec((None, tb, w), lambda b, t, o=off // w: (b, t, o))

    def full(shape):
        return pl.BlockSpec(shape, lambda b, t: (0,) * len(shape))

    return pl.pallas_call(
        functools.partial(_gla_kernel, tb=tb),
        grid=(bsz, T // tb),
        in_specs=[col(OFF_GQ, GLA_KW), col(OFF_GK, GLA_KW), col(OFF_GV, GLA_VW), col(OFF_GR, GLA_VW),
                  col(OFF_SMALL, LANES),
                  full((LANES, GLA_KW)), full((1, GLA_KW)), full((1, GLA_DV)), full((tb, tb)), full((tb, tb))],
        out_specs=pl.BlockSpec((None, tb, GLA_VW), lambda b, t: (b, t, 0)),
        out_shape=jax.ShapeDtypeStruct((bsz, T, GLA_VW), BF16),
        scratch_shapes=[pltpu.VMEM((GLA_HEADS, GLA_DV, GLA_DK), F32)],
        compiler_params=_cparams(("parallel", "arbitrary")),
        name="gla",
    )(proj3, proj3, proj3, proj3, proj3, wa_pad, ba.reshape(1, GLA_KW), norm_g.reshape(1, GLA_DV), tri, same)


def _ssd_kernel(z_ref, x_ref, bm_ref, cm_ref, sm_ref, cwx_ref, cwb_ref, cwc_ref, cbx_ref, cbb_ref, cbc_ref,
                dtb_ref, alog_ref, dskip_ref, ng_ref, tri_ref, exp_ref, o_ref,
                st_ref, extx_ref, extb_ref, extc_ref, xa_ref, ba_ref, ca_ref, dt_ref, a_ref, *, tb, nchunk):
    L = SSM_CHUNK
    GW = SSM_HPG * SSM_HEADDIM
    NS = SSM_STATE
    first = pl.program_id(1) == 0

    @pl.when(first)
    def _():
        st_ref[...] = jnp.zeros_like(st_ref)

    def conv_silu(src_ref, ext_ref, w_ref, b_ref, dst_ref):
        @pl.when(first)
        def _():
            ext_ref[0:8, :] = jnp.zeros((8, ext_ref.shape[1]), F32)

        @pl.when(jnp.logical_not(first))
        def _():
            ext_ref[0:8, :] = ext_ref[tb:tb + 8, :]

        ext_ref[8:8 + tb, :] = src_ref[...]
        acc = b_ref[...] + w_ref[SSM_CONV - 1:SSM_CONV, :] * ext_ref[8:8 + tb, :]
        for j in range(1, SSM_CONV):
            acc = acc + w_ref[SSM_CONV - 1 - j:SSM_CONV - j, :] * ext_ref[8 - j:8 - j + tb, :]
        dst_ref[...] = _silu(acc)

    conv_silu(x_ref, extx_ref, cwx_ref, cbx_ref, xa_ref)
    conv_silu(bm_ref, extb_ref, cwb_ref, cbb_ref, ba_ref)
    conv_silu(cm_ref, extc_ref, cwc_ref, cbc_ref, ca_ref)

    v = sm_ref[...] + dtb_ref[...]
    dt = jnp.maximum(v, 0.0) + jnp.log1p(jnp.exp(-jnp.abs(v)))
    dt_ref[...] = dt
    a_ref[...] = dt * (-jnp.exp(alog_ref[...]))

    tri = tri_ref[...]
    row = lax.broadcasted_iota(jnp.int32, (L, GW), 0)
    lane = lax.broadcasted_iota(jnp.int32, (L, GW), 1)
    lane_in = jnp.bitwise_and(lane, SSM_HEADDIM - 1)
    eye_t = (lane_in == row).astype(F32)
    tril_t = lane_in <= row
    bd_mask = (lax.broadcasted_iota(jnp.int32, (GW, GW), 0) // L
               == lax.broadcasted_iota(jnp.int32, (GW, GW), 1) // SSM_HEADDIM).astype(F32)

    states = [st_ref[g] for g in range(SSM_GROUPS)]
    for c in range(nchunk):
        rows = slice(c * L, (c + 1) * L)
        cum128 = _dot01_left(tri, a_ref[rows, :])
        dt_c = dt_ref[rows, :]
        for g in range(SSM_GROUPS):
            cs = slice(g * GW, (g + 1) * GW)
            e_g = exp_ref[:, cs]
            cum_e = _dot01_right(cum128, e_g)
            dt_e = _dot01_right(dt_c, e_g)
            cum_last = cum_e[L - 1:L, :]
            r_row = jnp.sum(cum_e * eye_t, axis=0, keepdims=True)
            decay = jnp.exp(jnp.where(tril_t, cum_e - r_row, -jnp.inf))
            x_g = xa_ref[rows, cs]
            xdt = x_g * dt_e
            b_g = ba_ref[rows, g * NS:(g + 1) * NS]
            c_g = ca_ref[rows, g * NS:(g + 1) * NS].astype(BF16)
            bb = b_g.astype(BF16)
            cb_t = _dot_nt(c_g, jnp.concatenate([bb] * SSM_HPG, axis=0))
            xdt_bd = (jnp.concatenate([xdt] * SSM_HPG, axis=0) * bd_mask).astype(BF16)
            y = _dot((cb_t * decay).astype(BF16), xdt_bd)
            st = states[g]
            y = y + _dot(c_g, st.astype(BF16)) * jnp.exp(cum_e)
            dend = jnp.exp(cum_last - cum_e)
            states[g] = jnp.exp(cum_last) * st + _dot(b_g.T.astype(BF16), (dend * xdt).astype(BF16))
            y = y + x_g * dskip_ref[:, cs]
            y = y * _silu(z_ref[rows, cs])
            y = y * lax.rsqrt(jnp.mean(y * y, axis=-1, keepdims=True) + EPS) * ng_ref[:, cs]
            o_ref[rows, cs] = y.astype(o_ref.dtype)
    for g in range(SSM_GROUPS):
        st_ref[g] = states[g]


def _ssd_consts():
    tri = jnp.asarray(np.tril(np.ones((SSM_CHUNK, SSM_CHUNK), np.float32)), BF16)
    e = np.zeros((LANES, SSM_INNER), np.float32)
    for h in range(SSM_HEADS):
        e[SM_DT + h, h * SSM_HEADDIM:(h + 1) * SSM_HEADDIM] = 1.0
    return tri, jnp.asarray(e, BF16)


def _small_row(v, off):
    return jnp.zeros((1, LANES), F32).at[0, off:off + v.shape[0]].set(v.astype(F32))


def ssd_mixer(proj3, conv_w, conv_b, dt_bias, a_log, d_skip, norm_g, tb=256):
    bsz, T, _ = proj3.shape
    tb = min(tb, T)
    nchunk = tb // SSM_CHUNK
    GN = SSM_GROUPS * SSM_STATE
    tri, expand = _ssd_consts()
    cwx, cwb, cwc = conv_w[:, :SSM_INNER], conv_w[:, SSM_INNER:SSM_INNER + GN], conv_w[:, SSM_INNER + GN:]
    cb2 = conv_b.reshape(1, SSM_CONV_CH)
    cbx, cbb, cbc = cb2[:, :SSM_INNER], cb2[:, SSM_INNER:SSM_INNER + GN], cb2[:, SSM_INNER + GN:]
    dtb = _small_row(dt_bias, SM_DT)
    alog = _small_row(a_log, SM_DT)
    dskip = jnp.repeat(d_skip.astype(F32), SSM_HEADDIM).reshape(1, SSM_INNER)

    def col(off, w):
        return pl.BlockSpec((None, tb, w), lambda b, t, o=off // w: (b, t, o))

    def full(shape):
        return pl.BlockSpec(shape, lambda b, t: (0,) * len(shape))

    return pl.pallas_call(
        functools.partial(_ssd_kernel, tb=tb, nchunk=nchunk),
        grid=(bsz, T // tb),
        in_specs=[col(OFF_SZ, SSM_INNER), col(OFF_SX, SSM_INNER), col(OFF_SB, GN), col(OFF_SC, GN),
                  col(OFF_SMALL, LANES),
                  full((SSM_CONV, SSM_INNER)), full((SSM_CONV, GN)), full((SSM_CONV, GN)),
                  full((1, SSM_INNER)), full((1, GN)), full((1, GN)),
                  full((1, LANES)), full((1, LANES)), full((1, SSM_INNER)), full((1, SSM_INNER)),
                  full((SSM_CHUNK, SSM_CHUNK)), full((LANES, SSM_INNER))],
        out_specs=pl.BlockSpec((None, tb, SSM_INNER), lambda b, t: (b, t, 0)),
        out_shape=jax.ShapeDtypeStruct((bsz, T, SSM_INNER), BF16),
        scratch_shapes=[pltpu.VMEM((SSM_GROUPS, SSM_STATE, SSM_HPG * SSM_HEADDIM), F32),
                        pltpu.VMEM((tb + 8, SSM_INNER), F32), pltpu.VMEM((tb + 8, GN), F32),
                        pltpu.VMEM((tb + 8, GN), F32),
                        pltpu.VMEM((tb, SSM_INNER), F32), pltpu.VMEM((tb, GN), F32), pltpu.VMEM((tb, GN), F32),
                        pltpu.VMEM((tb, LANES), F32), pltpu.VMEM((tb, LANES), F32)],
        compiler_params=_cparams(("parallel", "arbitrary")),
        name="ssd",
    )(proj3, proj3, proj3, proj3, proj3, cwx, cwb, cwc, cbx, cbb, cbc, dtb, alog, dskip,
      norm_g.reshape(1, SSM_INNER), tri, expand)


NSA_SLOPES = tuple(float(np.float32(2.0 ** (-8.0 * (i + 1) / NSA_HEADS))) for i in range(NSA_HEADS))
NSA_SCALE = NSA_DH ** -0.5
KV_W = 2 * NSA_KVW
NEG_HUGE = -3.0e38


def _head_cols(hh):
    return slice(hh * NSA_DH, (hh + 1) * NSA_DH)


def _nsa_compress_kernel(k_ref, v_ref, pos_ref, bd1_ref, bd2_ref, o_ref, *, n16):
    S = NSA_CMP_STRIDE
    top = jnp.zeros((n16, KV_W), F32)
    bot = jnp.zeros((n16, KV_W), F32)
    for l in range(S):
        rows = pl.ds(l, n16, stride=S)
        x = jnp.concatenate([k_ref[rows, :], v_ref[rows, :]], axis=-1)
        top += _dot((x + pos_ref[l:l + 1, :]).astype(BF16), bd1_ref[l])
        bot += _dot((x + pos_ref[S + l:S + l + 1, :]).astype(BF16), bd1_ref[S + l])
    pre = top + pltpu.roll(bot, n16 - 1, axis=0)
    out = _dot(_silu(pre).astype(BF16), bd2_ref[...])
    row = lax.broadcasted_iota(jnp.int32, (n16, KV_W), 0)
    o_ref[...] = jnp.where(row < n16 - 1, out, 0.0)


def nsa_compress(proj3, cmp_pos, cmp_w1, cmp_w2):
    bsz, T, _ = proj3.shape
    n16 = T // NSA_CMP_STRIDE
    sel = np.array([0, 0, 1, 1])
    eye = jnp.eye(4, dtype=F32)
    w1r = cmp_w1.reshape(2, NSA_CMP_LEN, NSA_DH, NSA_DH)[sel]
    bd1 = jnp.einsum('ab,alde->ladbe', eye, w1r).reshape(NSA_CMP_LEN, KV_W, KV_W).astype(BF16)
    bd2 = jnp.einsum('ab,ade->adbe', eye, cmp_w2[sel]).reshape(KV_W, KV_W).astype(BF16)
    pos = jnp.concatenate([cmp_pos[0], cmp_pos[0], cmp_pos[1], cmp_pos[1]], axis=-1)
    return pl.pallas_call(
        functools.partial(_nsa_compress_kernel, n16=n16),
        grid=(bsz,),
        in_specs=[pl.BlockSpec((None, T, NSA_KVW), lambda b: (b, 0, OFF_NKV // NSA_KVW)),
                  pl.BlockSpec((None, T, NSA_KVW), lambda b: (b, 0, OFF_NKV // NSA_KVW + 1)),
                  pl.BlockSpec((NSA_CMP_LEN, KV_W), lambda b: (0, 0)),
                  pl.BlockSpec((NSA_CMP_LEN, KV_W, KV_W), lambda b: (0, 0, 0)),
                  pl.BlockSpec((KV_W, KV_W), lambda b: (0, 0))],
        out_specs=pl.BlockSpec((None, n16, KV_W), lambda b: (b, 0, 0)),
        out_shape=jax.ShapeDtypeStruct((bsz, n16, KV_W), F32),
        compiler_params=_cparams(("parallel",)),
        name="nsa_compress",
    )(proj3, proj3, pos, bd1, bd2)


def _stack_queries(q_ref, qc_ref, qa_ref, g, tq, shared=None):
    for h in range(NSA_HPG):
        hh = g * NSA_HPG + h
        qh = (q_ref[:, _head_cols(hh)] * NSA_SCALE).astype(BF16)
        qx = jnp.broadcast_to(qc_ref[hh:hh + 1, :], (tq, EXTRA_W))
        qa_ref[h * tq:(h + 1) * tq, 0:NSA_DH + EXTRA_W] = jnp.concatenate([qh, qx], axis=-1)
        if shared is not None:
            qa_ref[h * tq:(h + 1) * tq, NSA_DH + EXTRA_W:] = shared


def _nsa_cmp_topk_kernel(q_ref, kcv_ref, cpos_ref, qc_ref, sm_ref, ov_ref, o_ref, sel_ref, act_ref, qa_ref, imp_ref,
                         *, tq, n16, n_sel, n_top, cw):
    t0 = pl.program_id(1) * tq
    tpos = t0 + lax.broadcasted_iota(jnp.int32, (1, tq), 1)
    any_valid = (tpos >= NSA_CMP_LEN - 1).astype(F32)
    gates = 1.0 / (1.0 + jnp.exp(-sm_ref[...]))
    blk = lax.broadcasted_iota(jnp.int32, (n_sel, 1), 0)
    blk_f = blk.astype(F32)
    cur = tpos // NSA_SEL_BLOCK
    forced = (blk == 0) | (blk == cur) | (blk == cur - 1)
    future = blk * NSA_SEL_BLOCK > tpos
    blk_row = lax.broadcasted_iota(jnp.int32, (1, n_sel), 1)

    def attend(nc):
        cmp_end = lax.broadcasted_iota(jnp.int32, (nc, 1), 0) * NSA_CMP_STRIDE + (NSA_CMP_LEN - 1)
        bias = jnp.where(cmp_end <= tpos, 0.0, -BIG)
        for g in range(NSA_GROUPS):
            kc = jnp.concatenate([kcv_ref[:nc, g * NSA_DH:(g + 1) * NSA_DH].astype(BF16), cpos_ref[:nc, :]], axis=-1)
            vc_t = kcv_ref[:nc, NSA_KVW + g * NSA_DH:NSA_KVW + (g + 1) * NSA_DH].T.astype(BF16)
            _stack_queries(q_ref, qc_ref, qa_ref, g, tq)
            psum = None
            for h in range(NSA_HPG):
                hh = g * NSA_HPG + h
                s = _dot_nt(kc, qa_ref[h * tq:(h + 1) * tq, :]) + bias
                e = jnp.exp(s - jnp.max(s, axis=0, keepdims=True))
                p = e * (any_valid / jnp.sum(e, axis=0, keepdims=True))
                psum = p if psum is None else psum + p
                o_ref[:, _head_cols(hh)] = gates[:, SM_NG + hh:SM_NG + hh + 1] * _dot(vc_t, p.astype(BF16)).T
            imp_ref[g] = _dot01_left(ov_ref[:, :nc], psum)

    need = (t0 + tq) // NSA_CMP_STRIDE
    nchunks = n16 // cw
    for k in range(1, nchunks + 1):
        lo = (k - 1) * cw
        cond = (need > lo) & (need <= k * cw) if k < nchunks else need > lo
        pl.when(cond)(functools.partial(attend, k * cw))

    for g in range(NSA_GROUPS):
        work = jnp.where(forced, BIG, jnp.where(future, -BIG, imp_ref[g]))
        sel_t = jnp.zeros((n_sel, tq), F32)
        for _ in range(n_top):
            top = jnp.max(work, axis=0, keepdims=True)
            idx = jnp.min(jnp.where(work == top, blk_f, float(n_sel)), axis=0, keepdims=True)
            pick = blk_f == idx
            sel_t = jnp.where(pick, 1.0, sel_t)
            work = jnp.where(pick, NEG_HUGE, work)
        sel = sel_t.T
        sel_ref[:, g * n_sel:(g + 1) * n_sel] = sel.astype(sel_ref.dtype)
        union = jnp.max(sel, axis=0, keepdims=True)
        act_ref[:, g * n_sel:(g + 1) * n_sel] = jnp.where(blk_row * NSA_SEL_BLOCK < t0, union, 0.0)


def _nsa_overlap(n16, n_sel):
    n_cmp = n16 - 1
    tok = (np.arange(n_cmp) * NSA_CMP_STRIDE)[:, None] + np.arange(NSA_CMP_LEN)[None, :]
    ov = np.zeros((n16, n_sel), np.float32)
    np.add.at(ov, (np.repeat(np.arange(n_cmp), NSA_CMP_LEN), (tok // NSA_SEL_BLOCK).ravel()), 1.0 / NSA_CMP_LEN)
    return jnp.asarray(ov.T, BF16)


def nsa_cmp_topk(proj3, kcv, tq=256):
    bsz, T, _ = proj3.shape
    tq = min(tq, T)
    n16 = T // NSA_CMP_STRIDE
    n_sel = T // NSA_SEL_BLOCK
    n_top = min(NSA_TOPN, n_sel)
    cmp_end = np.arange(n16) * NSA_CMP_STRIDE + NSA_CMP_LEN - 1
    cpos = jnp.asarray(_pos_cols(cmp_end, np.ones(n16, bool)), BF16)
    return pl.pallas_call(
        functools.partial(_nsa_cmp_topk_kernel, tq=tq, n16=n16, n_sel=n_sel, n_top=n_top, cw=min(LANES, n16)),
        grid=(bsz, T // tq),
        in_specs=[pl.BlockSpec((None, tq, NSA_QW), lambda b, i: (b, i, OFF_NQ // NSA_QW)),
                  pl.BlockSpec((None, n16, KV_W), lambda b, i: (b, 0, 0)),
                  pl.BlockSpec((n16, EXTRA_W), lambda b, i: (0, 0)),
                  pl.BlockSpec((NSA_HEADS, EXTRA_W), lambda b, i: (0, 0)),
                  pl.BlockSpec((None, tq, LANES), lambda b, i: (b, i, OFF_SMALL // LANES)),
                  pl.BlockSpec((n_sel, n16), lambda b, i: (0, 0))],
        out_specs=[pl.BlockSpec((None, tq, NSA_QW), lambda b, i: (b, i, 0)),
                   pl.BlockSpec((None, tq, NSA_GROUPS * n_sel), lambda b, i: (b, i, 0)),
                   pl.BlockSpec((None, None, 1, NSA_GROUPS * n_sel), lambda b, i: (b, i, 0, 0))],
        out_shape=[jax.ShapeDtypeStruct((bsz, T, NSA_QW), F32),
                   jax.ShapeDtypeStruct((bsz, T, NSA_GROUPS * n_sel), BF16),
                   jax.ShapeDtypeStruct((bsz, T // tq, 1, NSA_GROUPS * n_sel), F32)],
        scratch_shapes=[pltpu.VMEM((NSA_HPG * tq, NSA_DH + EXTRA_W), BF16),
                        pltpu.VMEM((NSA_GROUPS, n_sel, tq), F32)],
        compiler_params=_cparams(("parallel", "parallel")),
        name="nsa_cmp_topk",
    )(proj3, kcv, cpos, _nsa_query_consts(), proj3, _nsa_overlap(n16, n_sel))


def _nsa_window_kernel(q_ref, k0_ref, k1_ref, k2_ref, v0_ref, v1_ref, v2_ref, qc_ref, sm_ref, prev_ref, o_ref,
                       qa_ref, *, tq):
    nb = NSA_WINDOW // tq + 1
    i = pl.program_id(1)
    tpos = i * tq + lax.broadcasted_iota(jnp.int32, (1, tq), 1)
    kpos = (i - (nb - 1)) * tq + lax.broadcasted_iota(jnp.int32, (nb * tq, 1), 0)
    d = tpos - kpos
    valid = (d >= 0) & (d < NSA_WINDOW) & (kpos >= 0)
    hpc = NSA_HPG // WIN_CHAINS
    bias = jnp.concatenate([jnp.where(valid, 0.0, -BIG)] * hpc, axis=1)
    gates = 1.0 / (1.0 + jnp.exp(-sm_ref[...]))
    for g in range(NSA_GROUPS):
        kw = jnp.concatenate([k0_ref[g], k1_ref[g], k2_ref[g]], axis=0)
        vw = jnp.concatenate([v0_ref[g], v1_ref[g], v2_ref[g]], axis=0)
        vw_t = vw.astype(F32).T.astype(BF16)
        _stack_queries(q_ref, qc_ref, qa_ref, g, tq)
        for c in range(WIN_CHAINS):
            s = _dot_nt(kw, qa_ref[c * hpc * tq:(c + 1) * hpc * tq, :]) + bias
            e = jnp.exp(s - jnp.max(s, axis=0, keepdims=True))
            acc = _dot(vw_t, e.astype(BF16))
            o = (acc[:NSA_DH, :] * (1.0 / acc[NSA_DH:NSA_DH + 1, :])).T
            for h in range(hpc):
                hh = g * NSA_HPG + c * hpc + h
                gate = gates[:, SM_NG + 2 * NSA_HEADS + hh:SM_NG + 2 * NSA_HEADS + hh + 1]
                o_ref[:, _head_cols(hh)] = prev_ref[:, _head_cols(hh)] + gate * o[h * tq:(h + 1) * tq]


def nsa_window(proj3, prev, tq=256):
    bsz, T, _ = proj3.shape
    tq = min(tq, T)
    assert NSA_WINDOW % tq == 0 and NSA_WINDOW // tq == 2
    kc, vc = _nsa_key_consts(T, tq, onehot=False)
    kaug, vaug = nsa_kprep(proj3, kc, vc, tq, (OFF_NKV + 2 * KV_W) // KV_W)
    wk = kaug.shape[-1]

    def kvspec(back, w):
        return pl.BlockSpec((None, NSA_GROUPS, tq, w), lambda b, i, back=back: (b, 0, jnp.maximum(i - back, 0), 0))

    return pl.pallas_call(
        functools.partial(_nsa_window_kernel, tq=tq),
        grid=(bsz, T // tq),
        in_specs=[pl.BlockSpec((None, tq, NSA_QW), lambda b, i: (b, i, OFF_NQ // NSA_QW)),
                  kvspec(2, wk), kvspec(1, wk), kvspec(0, wk),
                  kvspec(2, 2 * NSA_DH), kvspec(1, 2 * NSA_DH), kvspec(0, 2 * NSA_DH),
                  pl.BlockSpec((NSA_HEADS, EXTRA_W), lambda b, i: (0, 0)),
                  pl.BlockSpec((None, tq, LANES), lambda b, i: (b, i, OFF_SMALL // LANES)),
                  pl.BlockSpec((None, tq, NSA_QW), lambda b, i: (b, i, 0))],
        out_specs=pl.BlockSpec((None, tq, NSA_QW), lambda b, i: (b, i, 0)),
        out_shape=jax.ShapeDtypeStruct((bsz, T, NSA_QW), F32),
        scratch_shapes=[pltpu.VMEM((NSA_HPG * tq, wk), BF16)],
        compiler_params=_cparams(("parallel", "parallel")),
        name="nsa_window",
    )(proj3, kaug, kaug, kaug, vaug, vaug, vaug, _nsa_query_consts(), proj3, prev)


MASK_BIG = 1e30
SEL_STEP = 8
NSA_CHAINS = 1
WIN_CHAINS = 2
POS_HI = 128
EXTRA_W = NSA_DH


def _bf16_pieces(x):
    x = np.float32(x)
    out = []
    for _ in range(3):
        p = np.float32(np.asarray(x, dtype=jnp.bfloat16))
        out.append(p)
        x = np.float32(x - p)
    return out


def _pos_cols(pos, real):
    c = np.zeros((len(pos), EXTRA_W), np.float32)
    for j in range(3):
        c[real, j] = (pos[real] // POS_HI) * POS_HI
        c[real, 3 + j] = pos[real] % POS_HI
    c[~real, 6] = 1.0
    return c


def _nsa_query_consts():
    qc = np.zeros((NSA_HEADS, EXTRA_W), np.float32)
    for hh in range(NSA_HEADS):
        qc[hh, 0:3] = qc[hh, 3:6] = _bf16_pieces(NSA_SLOPES[hh])
        qc[hh, 6] = -MASK_BIG
    return jnp.asarray(qc, BF16)


def _nsa_key_consts(T, pad, onehot):
    n_sel = T // NSA_SEL_BLOCK
    pos = np.arange(T + pad)
    real = pos < T
    kc = _pos_cols(pos, real)
    if onehot:
        oh = np.zeros((T + pad, n_sel), np.float32)
        oh[pos[real], pos[real] // NSA_SEL_BLOCK] = 1.0
        kc = np.concatenate([kc, oh], axis=1)
    vc = np.zeros((T + pad, NSA_DH), np.float32)
    vc[real, 0] = 1.0
    return jnp.asarray(kc, BF16), jnp.asarray(vc, BF16)


def _nsa_kprep_kernel(kv_ref, kc_ref, vc_ref, ka_ref, va_ref):
    real = pl.program_id(1) < pl.num_programs(1) - 1
    kv = (kv_ref[...] * jnp.where(real, 1.0, 0.0)).astype(BF16)
    for g in range(NSA_GROUPS):
        ka_ref[g] = jnp.concatenate([kv[:, g * NSA_DH:(g + 1) * NSA_DH], kc_ref[...]], axis=-1)
        va_ref[g] = jnp.concatenate([kv[:, NSA_KVW + g * NSA_DH:NSA_KVW + (g + 1) * NSA_DH], vc_ref[...]], axis=-1)


def nsa_kprep(proj3, kc, vc, tb, cs):
    bsz, T, _ = proj3.shape
    nt = T // tb
    wk = NSA_DH + kc.shape[1]
    return pl.pallas_call(
        _nsa_kprep_kernel,
        grid=(bsz, nt + 1),
        in_specs=[pl.BlockSpec((None, tb, KV_W), lambda b, t: (b, jnp.minimum(t, nt - 1), cs)),
                  pl.BlockSpec((tb, kc.shape[1]), lambda b, t: (t, 0)),
                  pl.BlockSpec((tb, NSA_DH), lambda b, t: (t, 0))],
        out_specs=[pl.BlockSpec((None, NSA_GROUPS, tb, wk), lambda b, t: (b, 0, t, 0)),
                   pl.BlockSpec((None, NSA_GROUPS, tb, 2 * NSA_DH), lambda b, t: (b, 0, t, 0))],
        out_shape=[jax.ShapeDtypeStruct((bsz, NSA_GROUPS, T + tb, wk), BF16),
                   jax.ShapeDtypeStruct((bsz, NSA_GROUPS, T + tb, 2 * NSA_DH), BF16)],
        compiler_params=_cparams(("parallel", "parallel")),
        name="nsa_kprep",
    )(proj3, kc, vc)


def _nsa_select_kernel(ids_ref, cnt_ref, q_ref, ka_ref, va_ref, sel_ref, qc_ref, sm_ref, prev_ref, o_ref,
                       qa_ref, kt_ref, vt_ref, m_ref, acc_ref, *, tq, n_sel, lmax):
    b = pl.program_id(0)
    i = pl.program_id(1)
    B = NSA_SEL_BLOCK
    kpos = i * tq + lax.broadcasted_iota(jnp.int32, (tq, 1), 0)
    tpos = i * tq + lax.broadcasted_iota(jnp.int32, (1, tq), 1)
    causal_bias = jnp.where(kpos <= tpos, 0.0, -BIG)
    hpc = NSA_HPG // NSA_CHAINS
    chains = [slice(c * hpc * tq, (c + 1) * hpc * tq) for c in range(NSA_CHAINS)]
    causal_bias = jnp.concatenate([causal_bias] * hpc, axis=1)
    gates = 1.0 / (1.0 + jnp.exp(-sm_ref[...]))
    own = pl.ds(pl.multiple_of(i * tq, tq), tq)
    for g in range(NSA_GROUPS):
        mcols = ((sel_ref[:, g * n_sel:(g + 1) * n_sel].astype(F32) - 1.0) * MASK_BIG).astype(BF16)
        _stack_queries(q_ref, qc_ref, qa_ref, g, tq, shared=mcols)
        v_own = va_ref[g, own, :].astype(F32).T.astype(BF16)
        for rs in chains:
            s = _dot_nt(ka_ref[g, own, :], qa_ref[rs, :]) + causal_bias
            m0 = jnp.max(s, axis=0, keepdims=True)
            m_ref[:, rs] = m0
            acc_ref[:, rs] = _dot(v_own, jnp.exp(s - m0).astype(BF16))
        slot = (b * pl.num_programs(1) + i) * NSA_GROUPS + g

        def step(st, carry):
            for u in range(SEL_STEP):
                r0 = pl.multiple_of(ids_ref[slot * lmax + st * SEL_STEP + u] * B, B)
                kt_ref[u * B:(u + 1) * B, :] = ka_ref[g, pl.ds(r0, B), :]
                vt_ref[u * B:(u + 1) * B, :] = va_ref[g, pl.ds(r0, B), :]
            v_t = vt_ref[...].astype(F32).T.astype(BF16)
            for rs in chains:
                s = _dot_nt(kt_ref[...], qa_ref[rs, :])
                m_old = m_ref[:, rs]
                m_new = jnp.maximum(m_old, jnp.max(s, axis=0, keepdims=True))
                p = jnp.exp(s - m_new).astype(BF16)
                acc_ref[:, rs] = jnp.exp(m_old - m_new) * acc_ref[:, rs] + _dot(v_t, p)
                m_ref[:, rs] = m_new
            return carry

        lax.fori_loop(0, (cnt_ref[slot] + SEL_STEP - 1) // SEL_STEP, step, 0)
        acc = acc_ref[...]
        o = (acc[:NSA_DH, :] * (1.0 / acc[NSA_DH:NSA_DH + 1, :])).T
        for h in range(NSA_HPG):
            hh = g * NSA_HPG + h
            gate = gates[:, SM_NG + NSA_HEADS + hh:SM_NG + NSA_HEADS + hh + 1]
            o_ref[:, _head_cols(hh)] = (prev_ref[:, _head_cols(hh)] + gate * o[h * tq:(h + 1) * tq, :]
                                        ).astype(o_ref.dtype)


def nsa_select(proj3, sel, act, prev, tq=256):
    bsz, T, _ = proj3.shape
    tq = min(tq, T)
    nt = T // tq
    n_sel = T // NSA_SEL_BLOCK
    lmax = -(-n_sel // SEL_STEP) * SEL_STEP
    kc, vc = _nsa_key_consts(T, tq, onehot=True)
    qc = _nsa_query_consts()
    kaug, vaug = nsa_kprep(proj3, kc, vc, tq, (OFF_NKV + KV_W) // KV_W)
    wk = kaug.shape[-1]
    on = (act.reshape(bsz, nt, NSA_GROUPS, n_sel) > 0.0).astype(jnp.int32)
    seen = jnp.cumsum(on, axis=-1)
    ids = jnp.sum(seen[..., None, :] <= jnp.arange(lmax, dtype=jnp.int32)[:, None], axis=-1, dtype=jnp.int32)
    ids = ids.reshape(-1)
    cnt = seen[..., -1].reshape(-1)
    rows = NSA_HPG * tq
    grid_spec = pltpu.PrefetchScalarGridSpec(
        num_scalar_prefetch=2,
        grid=(bsz, nt),
        in_specs=[pl.BlockSpec((None, tq, NSA_QW), lambda b, i, *_: (b, i, OFF_NQ // NSA_QW)),
                  pl.BlockSpec((None, NSA_GROUPS, T + tq, wk), lambda b, i, *_: (b, 0, 0, 0)),
                  pl.BlockSpec((None, NSA_GROUPS, T + tq, 2 * NSA_DH), lambda b, i, *_: (b, 0, 0, 0)),
                  pl.BlockSpec((None, tq, NSA_GROUPS * n_sel), lambda b, i, *_: (b, i, 0)),
                  pl.BlockSpec((NSA_HEADS, EXTRA_W), lambda b, i, *_: (0, 0)),
                  pl.BlockSpec((None, tq, LANES), lambda b, i, *_: (b, i, OFF_SMALL // LANES)),
                  pl.BlockSpec((None, tq, NSA_QW), lambda b, i, *_: (b, i, 0))],
        out_specs=pl.BlockSpec((None, tq, NSA_QW), lambda b, i, *_: (b, i, 0)),
        scratch_shapes=[pltpu.VMEM((rows, wk), BF16),
                        pltpu.VMEM((SEL_STEP * NSA_SEL_BLOCK, wk), BF16),
                        pltpu.VMEM((SEL_STEP * NSA_SEL_BLOCK, 2 * NSA_DH), BF16),
                        pltpu.VMEM((1, rows), F32), pltpu.VMEM((2 * NSA_DH, rows), F32)])
    return pl.pallas_call(
        functools.partial(_nsa_select_kernel, tq=tq, n_sel=n_sel, lmax=lmax),
        grid_spec=grid_spec,
        out_shape=jax.ShapeDtypeStruct((bsz, T, NSA_QW), BF16),
        compiler_params=_cparams(("parallel", "arbitrary")),
        name="nsa_select",
    )(ids, cnt, proj3, kaug, vaug, sel, qc, proj3, prev)


def nsa_mixer(proj3, cmp_pos, cmp_w1, cmp_w2):
    kcv = nsa_compress(proj3, cmp_pos, cmp_w1, cmp_w2)
    o_cmp, sel, act = nsa_cmp_topk(proj3, kcv)
    o_cw = nsa_window(proj3, o_cmp)
    return nsa_select(proj3, sel, act, o_cw)


def _pack_w_in(w_in):
    w_in = w_in.astype(BF16)
    (g_q, g_k, g_v, g_r, g_a, n_q, n_kv, n_g, s_z, s_xbc, s_dt, m_g) = jnp.split(w_in, SPLIT_POINTS, axis=-1)
    pad = jnp.zeros(w_in.shape[:-1] + (D_PK - OFF_SMALL - GLA_RANK - 3 * NSA_HEADS - SSM_HEADS,), w_in.dtype)
    return jnp.concatenate([m_g, g_q, g_k, g_v, g_r, n_q, s_z, s_xbc, n_kv, g_a, n_g, s_dt, pad], axis=-1)


def kernel(x, w_in, gla_a2, gla_a_bias, gla_norm, nsa_cmp_pos, nsa_cmp_w1, nsa_cmp_w2, ssm_conv_w, ssm_conv_b,
           ssm_dt_bias, ssm_a_log, ssm_d, ssm_norm, w_branch, w_out, norm_pre_mix, norm_post_mix, norm_pre_ffn,
           norm_post_ffn, w_ffn_gate, w_ffn_up, w_ffn_down):
    bsz, T, D = x.shape
    depth = w_in.shape[0]
    n = bsz * T
    w_in_pk = _pack_w_in(w_in)
    wa_pad = jnp.zeros((depth, LANES, GLA_KW), F32).at[:, SM_GA:SM_GA + GLA_RANK].set(gla_a2).astype(BF16)
    w_branch_b = w_branch.astype(BF16)
    w_out_b = w_out.astype(BF16)
    w_gate_b = w_ffn_gate.astype(BF16)
    w_up_b = w_ffn_up.astype(BF16)
    w_down_b = w_ffn_down.astype(BF16)
    xf = x.reshape(n, D)
    for l in range(depth):
        proj = norm_matmul(xf, norm_pre_mix[l], w_in_pk[l])
        proj3 = proj.reshape(bsz, T, D_PK)
        y_gla = gla_mixer(proj3, wa_pad[l], gla_a_bias[l], gla_norm[l])
        y_nsa = nsa_mixer(proj3, nsa_cmp_pos[l], nsa_cmp_w1[l], nsa_cmp_w2[l])
        y_ssm = ssd_mixer(proj3, ssm_conv_w[l], ssm_conv_b[l], ssm_dt_bias[l], ssm_a_log[l], ssm_d[l], ssm_norm[l])
        merged = merge_branches(y_gla.reshape(n, BRANCH_W), y_nsa.reshape(n, BRANCH_W), y_ssm.reshape(n, BRANCH_W),
                                w_branch_b[l], proj)
        xf = proj_norm_residual(merged, w_out_b[l], xf, norm_post_mix[l], tm=512, tk=D)
        act = ffn_up(xf, norm_pre_ffn[l], w_gate_b[l], w_up_b[l])
        xf = proj_norm_residual(act, w_down_b[l], xf, norm_post_ffn[l], tm=1024, tk=512)
    return xf.reshape(bsz, T, D)
```

```python
import functools

import numpy as np
import jax
import jax.numpy as jnp
from jax import lax
from jax.experimental import pallas as pl
from jax.experimental.pallas import tpu as pltpu

F32 = jnp.float32
BF16 = jnp.bfloat16

D_MODEL = 2048
EPS = 1e-6
N_BRANCH = 3
BRANCH_W = 1024
GLA_HEADS, GLA_DK, GLA_DV, GLA_RANK, GLA_TAU, GLA_CHUNK = 4, 256, 256, 16, 16.0, 64
GLA_KW = GLA_HEADS * GLA_DK
GLA_VW = GLA_HEADS * GLA_DV
NSA_HEADS, NSA_GROUPS, NSA_DH = 16, 2, 64
NSA_HPG = NSA_HEADS // NSA_GROUPS
NSA_QW = NSA_HEADS * NSA_DH
NSA_KVW = NSA_GROUPS * NSA_DH
NSA_CMP_LEN, NSA_CMP_STRIDE, NSA_SEL_BLOCK, NSA_TOPN, NSA_WINDOW = 32, 16, 64, 16, 512
BIG = 1e30
SSM_HEADS, SSM_HEADDIM, SSM_GROUPS, SSM_STATE, SSM_CONV, SSM_CHUNK = 16, 64, 4, 128, 4, 64
SSM_INNER = SSM_HEADS * SSM_HEADDIM
SSM_HPG = SSM_HEADS // SSM_GROUPS
SSM_CONV_CH = SSM_INNER + 2 * SSM_GROUPS * SSM_STATE
D_FF = ((8 * D_MODEL // 3 + 255) // 256) * 256
IN_SIZES = (GLA_KW, GLA_KW, GLA_VW, GLA_VW, GLA_RANK, NSA_QW, 6 * NSA_KVW, 3 * NSA_HEADS,
            SSM_INNER, SSM_CONV_CH, SSM_HEADS, N_BRANCH * D_MODEL)
SPLIT_POINTS = tuple(int(v) for v in np.cumsum(IN_SIZES)[:-1])

LANES = 128
VMEM_LIMIT = 56 * 1024 * 1024

OFF_MG = 0
OFF_GQ = OFF_MG + N_BRANCH * D_MODEL
OFF_GK = OFF_GQ + GLA_KW
OFF_GV = OFF_GK + GLA_KW
OFF_GR = OFF_GV + GLA_VW
OFF_NQ = OFF_GR + GLA_VW
OFF_SZ = OFF_NQ + NSA_QW
OFF_SX = OFF_SZ + SSM_INNER
OFF_SB = OFF_SX + SSM_INNER
OFF_SC = OFF_SB + SSM_GROUPS * SSM_STATE
OFF_NKV = OFF_SC + SSM_GROUPS * SSM_STATE
OFF_SMALL = OFF_NKV + 6 * NSA_KVW
D_PK = OFF_SMALL + 2 * LANES
SM_GA = 0
SM_NG = 16
SM_DT = 64


def _cparams(sem):
    return pltpu.CompilerParams(dimension_semantics=sem, vmem_limit_bytes=VMEM_LIMIT)


def _split3(x):
    hi = x.astype(BF16)
    r1 = x - hi.astype(F32)
    mid = r1.astype(BF16)
    lo = (r1 - mid.astype(F32)).astype(BF16)
    return hi, mid, lo


def _dot(a, b):
    return jnp.dot(a, b, preferred_element_type=F32)


def _dot_nt(a, b):
    return lax.dot_general(a, b, (((1,), (1,)), ((), ())), preferred_element_type=F32)


def _dot01_left(m01, x):
    hi, mid, lo = _split3(x)
    return _dot(m01, hi) + _dot(m01, mid) + _dot(m01, lo)


def _dot01_right(x, m01):
    hi, mid, lo = _split3(x)
    return _dot(hi, m01) + _dot(mid, m01) + _dot(lo, m01)


def _silu(x):
    return x / (1.0 + jnp.exp(-x))


def _norm_rows(x_ref, g_ref, h_ref):
    @pl.when(pl.program_id(1) == 0)
    def _():
        x = x_ref[...]
        y = x * lax.rsqrt(jnp.mean(x * x, axis=-1, keepdims=True) + EPS)
        h_ref[...] = (y * g_ref[...]).astype(h_ref.dtype)


def _norm_mm_kernel(x_ref, g_ref, w_ref, o_ref, h_ref):
    _norm_rows(x_ref, g_ref, h_ref)
    o_ref[...] = _dot(h_ref[...], w_ref[...]).astype(o_ref.dtype)


def norm_matmul(x, g, w, out_dtype=F32, tm=1024, tn=1024):
    m, k = x.shape
    n = w.shape[1]
    tm = min(tm, m)
    tn = min(tn, n)
    return pl.pallas_call(
        _norm_mm_kernel,
        grid=(m // tm, n // tn),
        in_specs=[pl.BlockSpec((tm, k), lambda i, j: (i, 0)), pl.BlockSpec((1, k), lambda i, j: (0, 0)),
                  pl.BlockSpec((k, tn), lambda i, j: (0, j))],
        out_specs=pl.BlockSpec((tm, tn), lambda i, j: (i, j)),
        out_shape=jax.ShapeDtypeStruct((m, n), out_dtype),
        scratch_shapes=[pltpu.VMEM((tm, k), BF16)],
        compiler_params=_cparams(("parallel", "arbitrary")),
        name="in_proj",
    )(x, g.reshape(1, k), w)


def _merge_kernel(yg_ref, yn_ref, ys_ref, wg_ref, wn_ref, ws_ref, g0_ref, g1_ref, g2_ref, o_ref):
    def gate(ref):
        return 1.0 / (1.0 + jnp.exp(-ref[...]))

    acc = gate(g0_ref) * _dot(yg_ref[...], wg_ref[...])
    acc += gate(g1_ref) * _dot(yn_ref[...], wn_ref[...])
    acc += gate(g2_ref) * _dot(ys_ref[...], ws_ref[...])
    o_ref[...] = acc.astype(o_ref.dtype)


def merge_branches(y_gla, y_nsa, y_ssm, w_branch, proj, tm=1024, tn=512):
    m = y_gla.shape[0]
    d = w_branch.shape[-1]
    tm = min(tm, m)
    nj = d // tn
    ys = pl.BlockSpec((tm, BRANCH_W), lambda i, j: (i, 0))

    def wspec(b):
        return pl.BlockSpec((None, BRANCH_W, tn), lambda i, j, b=b: (b, 0, j))

    def gspec(b):
        return pl.BlockSpec((tm, tn), lambda i, j, b=b: (i, (OFF_MG + b * D_MODEL) // tn + j))

    return pl.pallas_call(
        _merge_kernel,
        grid=(m // tm, nj),
        in_specs=[ys, ys, ys, wspec(0), wspec(1), wspec(2), gspec(0), gspec(1), gspec(2)],
        out_specs=pl.BlockSpec((tm, tn), lambda i, j: (i, j)),
        out_shape=jax.ShapeDtypeStruct((m, d), BF16),
        compiler_params=_cparams(("parallel", "parallel")),
        name="merge",
    )(y_gla, y_nsa, y_ssm, w_branch, w_branch, w_branch, proj, proj, proj)


def _proj_norm_res_kernel(a_ref, w_ref, x_ref, g_ref, o_ref, acc_ref):
    k = pl.program_id(1)

    @pl.when(k == 0)
    def _():
        acc_ref[...] = jnp.zeros_like(acc_ref)

    acc_ref[...] += _dot(a_ref[...], w_ref[...])

    @pl.when(k == pl.num_programs(1) - 1)
    def _():
        f = acc_ref[...]
        y = f * lax.rsqrt(jnp.mean(f * f, axis=-1, keepdims=True) + EPS)
        o_ref[...] = x_ref[...] + y * g_ref[...]


def proj_norm_residual(a, w, x, g, tm=1024, tk=512):
    m, kk = a.shape
    d = w.shape[1]
    tm = min(tm, m)
    return pl.pallas_call(
        _proj_norm_res_kernel,
        grid=(m // tm, kk // tk),
        in_specs=[pl.BlockSpec((tm, tk), lambda i, k: (i, k)),
                  pl.BlockSpec((tk, d), lambda i, k: (k, 0)),
                  pl.BlockSpec((tm, d), lambda i, k: (i, 0)),
                  pl.BlockSpec((1, d), lambda i, k: (0, 0))],
        out_specs=pl.BlockSpec((tm, d), lambda i, k: (i, 0)),
        out_shape=jax.ShapeDtypeStruct((m, d), F32),
        scratch_shapes=[pltpu.VMEM((tm, d), F32)],
        compiler_params=_cparams(("parallel", "arbitrary")),
        name="proj_norm_res",
    )(a, w, x, g.reshape(1, d))


def _ffn_up_kernel(x_ref, g_ref, wg_ref, wu_ref, o_ref, h_ref):
    _norm_rows(x_ref, g_ref, h_ref)
    h = h_ref[...]
    a = _dot(h, wg_ref[...])
    u = _dot(h, wu_ref[...])
    o_ref[...] = (_silu(a) * u).astype(o_ref.dtype)


def ffn_up(x, g, wg, wu, tm=1024, tn=512):
    m, k = x.shape
    n = wg.shape[1]
    tm = min(tm, m)
    return pl.pallas_call(
        _ffn_up_kernel,
        grid=(m // tm, n // tn),
        in_specs=[pl.BlockSpec((tm, k), lambda i, j: (i, 0)),
                  pl.BlockSpec((1, k), lambda i, j: (0, 0)),
                  pl.BlockSpec((k, tn), lambda i, j: (0, j)),
                  pl.BlockSpec((k, tn), lambda i, j: (0, j))],
        out_specs=pl.BlockSpec((tm, tn), lambda i, j: (i, j)),
        out_shape=jax.ShapeDtypeStruct((m, n), BF16),
        scratch_shapes=[pltpu.VMEM((tm, k), BF16)],
        compiler_params=_cparams(("parallel", "arbitrary")),
        name="ffn_up",
    )(x, g.reshape(1, k), wg, wu)


def _gla_kernel(q_ref, k_ref, v_ref, r_ref, sm_ref, wa_ref, ba_ref, ng_ref, tri_ref, same_ref, o_ref,
                st_ref, *, tb):
    C = GLA_CHUNK

    @pl.when(pl.program_id(1) == 0)
    def _():
        st_ref[...] = jnp.zeros_like(st_ref)

    pre = _dot(sm_ref[...].astype(BF16), wa_ref[...]) + ba_ref[...]
    la = (jnp.minimum(pre, 0.0) - jnp.log1p(jnp.exp(-jnp.abs(pre)))) * (1.0 / GLA_TAU)
    tri = tri_ref[...]
    same = same_ref[...]
    ri = lax.broadcasted_iota(jnp.int32, (tb, tb), 0)
    ci = lax.broadcasted_iota(jnp.int32, (tb, tb), 1)
    causal = (ri >= ci) & (ri // C == ci // C)
    for h in range(GLA_HEADS):
        ck = slice(h * GLA_DK, (h + 1) * GLA_DK)
        cv = slice(h * GLA_DV, (h + 1) * GLA_DV)
        hi, mid, lo = _split3(la[:, ck])
        bcum = _dot(tri, hi) + _dot(tri, mid) + _dot(tri, lo)
        b_last = _dot(same, hi) + _dot(same, mid) + _dot(same, lo)
        q = q_ref[:, ck] * (GLA_DK ** -0.5)
        k = k_ref[:, ck]
        v = v_ref[:, cv]
        q_dec = (q * jnp.exp(bcum)).astype(BF16)
        k_inv = (k * jnp.exp(-bcum)).astype(BF16)
        k_end = (k * jnp.exp(b_last - bcum)).astype(BF16)
        att = jnp.where(causal, _dot_nt(q_dec, k_inv), 0.0)
        o_intra = _dot(att.astype(BF16), v.astype(BF16))
        st = st_ref[h]
        outs = []
        for c in range(tb // C):
            rows = slice(c * C, (c + 1) * C)
            outs.append(o_intra[rows] + _dot_nt(q_dec[rows], st.astype(BF16)))
            st = jnp.exp(b_last[c * C:c * C + 1, :]) * st + _dot(v[rows].T.astype(BF16), k_end[rows])
        st_ref[h] = st
        o = jnp.concatenate(outs, axis=0)
        y = o * lax.rsqrt(jnp.mean(o * o, axis=-1, keepdims=True) + EPS) * ng_ref[...]
        o_ref[:, cv] = (y * _silu(r_ref[:, cv])).astype(o_ref.dtype)


def _chunk_masks(tb, chunk):
    idx = np.arange(tb)
    same = (idx[:, None] // chunk) == (idx[None, :] // chunk)
    return jnp.asarray(same & (idx[:, None] >= idx[None, :]), BF16), jnp.asarray(same, BF16)


def gla_mixer(proj3, wa_pad, ba, norm_g, tb=256):
    bsz, T, _ = proj3.shape
    tb = min(tb, T)
    tri, same = _chunk_masks(tb, GLA_CHUNK)

    def col(off, w):
        return pl.BlockSpec((None, tb, w), lambda b, t, o=off // w: (b, t, o))

    def full(shape):
        return pl.BlockSpec(shape, lambda b, t: (0,) * len(shape))

    return pl.pallas_call(
        functools.partial(_gla_kernel, tb=tb),
        grid=(bsz, T // tb),
        in_specs=[col(OFF_GQ, GLA_KW), col(OFF_GK, GLA_KW), col(OFF_GV, GLA_VW), col(OFF_GR, GLA_VW),
                  col(OFF_SMALL, LANES),
                  full((LANES, GLA_KW)), full((1, GLA_KW)), full((1, GLA_DV)), full((tb, tb)), full((tb, tb))],
        out_specs=pl.BlockSpec((None, tb, GLA_VW), lambda b, t: (b, t, 0)),
        out_shape=jax.ShapeDtypeStruct((bsz, T, GLA_VW), BF16),
        scratch_shapes=[pltpu.VMEM((GLA_HEADS, GLA_DV, GLA_DK), F32)],
        compiler_params=_cparams(("parallel", "arbitrary")),
        name="gla",
    )(proj3, proj3, proj3, proj3, proj3, wa_pad, ba.reshape(1, GLA_KW), norm_g.reshape(1, GLA_DV), tri, same)


def _ssd_kernel(z_ref, x_ref, bm_ref, cm_ref, sm_ref, cwx_ref, cwb_ref, cwc_ref, cbx_ref, cbb_ref, cbc_ref,
                dtb_ref, alog_ref, dskip_ref, ng_ref, tri_ref, exp_ref, o_ref,
                st_ref, extx_ref, extb_ref, extc_ref, xa_ref, ba_ref, ca_ref, dt_ref, a_ref, *, tb, nchunk):
    L = SSM_CHUNK
    GW = SSM_HPG * SSM_HEADDIM
    NS = SSM_STATE
    first = pl.program_id(1) == 0

    @pl.when(first)
    def _():
        st_ref[...] = jnp.zeros_like(st_ref)

    def conv_silu(src_ref, ext_ref, w_ref, b_ref, dst_ref):
        @pl.when(first)
        def _():
            ext_ref[0:8, :] = jnp.zeros((8, ext_ref.shape[1]), F32)

        @pl.when(jnp.logical_not(first))
        def _():
            ext_ref[0:8, :] = ext_ref[tb:tb + 8, :]

        ext_ref[8:8 + tb, :] = src_ref[...]
        acc = b_ref[...] + w_ref[SSM_CONV - 1:SSM_CONV, :] * ext_ref[8:8 + tb, :]
        for j in range(1, SSM_CONV):
            acc = acc + w_ref[SSM_CONV - 1 - j:SSM_CONV - j, :] * ext_ref[8 - j:8 - j + tb, :]
        dst_ref[...] = _silu(acc)

    conv_silu(x_ref, extx_ref, cwx_ref, cbx_ref, xa_ref)
    conv_silu(bm_ref, extb_ref, cwb_ref, cbb_ref, ba_ref)
    conv_silu(cm_ref, extc_ref, cwc_ref, cbc_ref, ca_ref)

    v = sm_ref[...] + dtb_ref[...]
    dt = jnp.maximum(v, 0.0) + jnp.log1p(jnp.exp(-jnp.abs(v)))
    dt_ref[...] = dt
    a_ref[...] = dt * (-jnp.exp(alog_ref[...]))

    tri = tri_ref[...]
    row = lax.broadcasted_iota(jnp.int32, (L, GW), 0)
    lane = lax.broadcasted_iota(jnp.int32, (L, GW), 1)
    lane_in = jnp.bitwise_and(lane, SSM_HEADDIM - 1)
    eye_t = (lane_in == row).astype(F32)
    tril_t = lane_in <= row
    bd_mask = (lax.broadcasted_iota(jnp.int32, (GW, GW), 0) // L
               == lax.broadcasted_iota(jnp.int32, (GW, GW), 1) // SSM_HEADDIM).astype(F32)

    states = [st_ref[g] for g in range(SSM_GROUPS)]
    for c in range(nchunk):
        rows = slice(c * L, (c + 1) * L)
        cum128 = _dot01_left(tri, a_ref[rows, :])
        dt_c = dt_ref[rows, :]
        for g in range(SSM_GROUPS):
            cs = slice(g * GW, (g + 1) * GW)
            e_g = exp_ref[:, cs]
            cum_e = _dot01_right(cum128, e_g)
            dt_e = _dot01_right(dt_c, e_g)
            cum_last = cum_e[L - 1:L, :]
            r_row = jnp.sum(cum_e * eye_t, axis=0, keepdims=True)
            decay = jnp.exp(jnp.where(tril_t, cum_e - r_row, -jnp.inf))
            x_g = xa_ref[rows, cs]
            xdt = x_g * dt_e
            b_g = ba_ref[rows, g * NS:(g + 1) * NS]
            c_g = ca_ref[rows, g * NS:(g + 1) * NS].astype(BF16)
            bb = b_g.astype(BF16)
            cb_t = _dot_nt(c_g, jnp.concatenate([bb] * SSM_HPG, axis=0))
            xdt_bd = (jnp.concatenate([xdt] * SSM_HPG, axis=0) * bd_mask).astype(BF16)
            y = _dot((cb_t * decay).astype(BF16), xdt_bd)
            st = states[g]
            y = y + _dot(c_g, st.astype(BF16)) * jnp.exp(cum_e)
            dend = jnp.exp(cum_last - cum_e)
            states[g] = jnp.exp(cum_last) * st + _dot(b_g.T.astype(BF16), (dend * xdt).astype(BF16))
            y = y + x_g * dskip_ref[:, cs]
            y = y * _silu(z_ref[rows, cs])
            y = y * lax.rsqrt(jnp.mean(y * y, axis=-1, keepdims=True) + EPS) * ng_ref[:, cs]
            o_ref[rows, cs] = y.astype(o_ref.dtype)
    for g in range(SSM_GROUPS):
        st_ref[g] = states[g]


def _ssd_consts():
    tri = jnp.asarray(np.tril(np.ones((SSM_CHUNK, SSM_CHUNK), np.float32)), BF16)
    e = np.zeros((LANES, SSM_INNER), np.float32)
    for h in range(SSM_HEADS):
        e[SM_DT + h, h * SSM_HEADDIM:(h + 1) * SSM_HEADDIM] = 1.0
    return tri, jnp.asarray(e, BF16)


def _small_row(v, off):
    return jnp.zeros((1, LANES), F32).at[0, off:off + v.shape[0]].set(v.astype(F32))


def ssd_mixer(proj3, conv_w, conv_b, dt_bias, a_log, d_skip, norm_g, tb=256):
    bsz, T, _ = proj3.shape
    tb = min(tb, T)
    nchunk = tb // SSM_CHUNK
    GN = SSM_GROUPS * SSM_STATE
    tri, expand = _ssd_consts()
    cwx, cwb, cwc = conv_w[:, :SSM_INNER], conv_w[:, SSM_INNER:SSM_INNER + GN], conv_w[:, SSM_INNER + GN:]
    cb2 = conv_b.reshape(1, SSM_CONV_CH)
    cbx, cbb, cbc = cb2[:, :SSM_INNER], cb2[:, SSM_INNER:SSM_INNER + GN], cb2[:, SSM_INNER + GN:]
    dtb = _small_row(dt_bias, SM_DT)
    alog = _small_row(a_log, SM_DT)
    dskip = jnp.repeat(d_skip.astype(F32), SSM_HEADDIM).reshape(1, SSM_INNER)

    def col(off, w):
        return pl.BlockSpec((None, tb, w), lambda b, t, o=off // w: (b, t, o))

    def full(shape):
        return pl.BlockSpec(shape, lambda b, t: (0,) * len(shape))

    return pl.pallas_call(
        functools.partial(_ssd_kernel, tb=tb, nchunk=nchunk),
        grid=(bsz, T // tb),
        in_specs=[col(OFF_SZ, SSM_INNER), col(OFF_SX, SSM_INNER), col(OFF_SB, GN), col(OFF_SC, GN),
                  col(OFF_SMALL, LANES),
                  full((SSM_CONV, SSM_INNER)), full((SSM_CONV, GN)), full((SSM_CONV, GN)),
                  full((1, SSM_INNER)), full((1, GN)), full((1, GN)),
                  full((1, LANES)), full((1, LANES)), full((1, SSM_INNER)), full((1, SSM_INNER)),
                  full((SSM_CHUNK, SSM_CHUNK)), full((LANES, SSM_INNER))],
        out_specs=pl.BlockSpec((None, tb, SSM_INNER), lambda b, t: (b, t, 0)),
        out_shape=jax.ShapeDtypeStruct((bsz, T, SSM_INNER), BF16),
        scratch_shapes=[pltpu.VMEM((SSM_GROUPS, SSM_STATE, SSM_HPG * SSM_HEADDIM), F32),
                        pltpu.VMEM((tb + 8, SSM_INNER), F32), pltpu.VMEM((tb + 8, GN), F32),
                        pltpu.VMEM((tb + 8, GN), F32),
                        pltpu.VMEM((tb, SSM_INNER), F32), pltpu.VMEM((tb, GN), F32), pltpu.VMEM((tb, GN), F32),
                        pltpu.VMEM((tb, LANES), F32), pltpu.VMEM((tb, LANES), F32)],
        compiler_params=_cparams(("parallel", "arbitrary")),
        name="ssd",
    )(proj3, proj3, proj3, proj3, proj3, cwx, cwb, cwc, cbx, cbb, cbc, dtb, alog, dskip,
      norm_g.reshape(1, SSM_INNER), tri, expand)


NSA_SLOPES = tuple(float(np.float32(2.0 ** (-8.0 * (i + 1) / NSA_HEADS))) for i in range(NSA_HEADS))
NSA_SCALE = NSA_DH ** -0.5
KV_W = 2 * NSA_KVW
NEG_HUGE = -3.0e38


def _head_cols(hh):
    return slice(hh * NSA_DH, (hh + 1) * NSA_DH)


def _nsa_compress_kernel(k_ref, v_ref, pos_ref, bd1_ref, bd2_ref, o_ref, *, n16):
    S = NSA_CMP_STRIDE
    top = jnp.zeros((n16, KV_W), F32)
    bot = jnp.zeros((n16, KV_W), F32)
    for l in range(S):
        rows = pl.ds(l, n16, stride=S)
        x = jnp.concatenate([k_ref[rows, :], v_ref[rows, :]], axis=-1)
        top += _dot((x + pos_ref[l:l + 1, :]).astype(BF16), bd1_ref[l])
        bot += _dot((x + pos_ref[S + l:S + l + 1, :]).astype(BF16), bd1_ref[S + l])
    pre = top + pltpu.roll(bot, n16 - 1, axis=0)
    out = _dot(_silu(pre).astype(BF16), bd2_ref[...])
    row = lax.broadcasted_iota(jnp.int32, (n16, KV_W), 0)
    o_ref[...] = jnp.where(row < n16 - 1, out, 0.0)


def nsa_compress(proj3, cmp_pos, cmp_w1, cmp_w2):
    bsz, T, _ = proj3.shape
    n16 = T // NSA_CMP_STRIDE
    sel = np.array([0, 0, 1, 1])
    eye = jnp.eye(4, dtype=F32)
    w1r = cmp_w1.reshape(2, NSA_CMP_LEN, NSA_DH, NSA_DH)[sel]
    bd1 = jnp.einsum('ab,alde->ladbe', eye, w1r).reshape(NSA_CMP_LEN, KV_W, KV_W).astype(BF16)
    bd2 = jnp.einsum('ab,ade->adbe', eye, cmp_w2[sel]).reshape(KV_W, KV_W).astype(BF16)
    pos = jnp.concatenate([cmp_pos[0], cmp_pos[0], cmp_pos[1], cmp_pos[1]], axis=-1)
    return pl.pallas_call(
        functools.partial(_nsa_compress_kernel, n16=n16),
        grid=(bsz,),
        in_specs=[pl.BlockSpec((None, T, NSA_KVW), lambda b: (b, 0, OFF_NKV // NSA_KVW)),
                  pl.BlockSpec((None, T, NSA_KVW), lambda b: (b, 0, OFF_NKV // NSA_KVW + 1)),
                  pl.BlockSpec((NSA_CMP_LEN, KV_W), lambda b: (0, 0)),
                  pl.BlockSpec((NSA_CMP_LEN, KV_W, KV_W), lambda b: (0, 0, 0)),
                  pl.BlockSpec((KV_W, KV_W), lambda b: (0, 0))],
        out_specs=pl.BlockSpec((None, n16, KV_W), lambda b: (b, 0, 0)),
        out_shape=jax.ShapeDtypeStruct((bsz, n16, KV_W), F32),
        compiler_params=_cparams(("parallel",)),
        name="nsa_compress",
    )(proj3, proj3, pos, bd1, bd2)


CMP_SEG_MAX = 320


def _cmp_segments(nc):
    bad = 2 * LANES
    if nc <= CMP_SEG_MAX + NSA_DH and nc != bad:
        return [(0, nc)]
    first = CMP_SEG_MAX if nc > CMP_SEG_MAX else nc - NSA_DH
    return [(0, first), (first, nc)]


def _stack_queries(q_ref, qc_ref, qa_ref, g, tq, shared=None):
    for h in range(NSA_HPG):
        hh = g * NSA_HPG + h
        qh = (q_ref[:, _head_cols(hh)] * NSA_SCALE).astype(BF16)
        qx = jnp.broadcast_to(qc_ref[hh:hh + 1, :], (tq, EXTRA_W))
        qa_ref[h * tq:(h + 1) * tq, 0:NSA_DH + EXTRA_W] = jnp.concatenate([qh, qx], axis=-1)
        if shared is not None:
            qa_ref[h * tq:(h + 1) * tq, NSA_DH + EXTRA_W:] = shared


def _nsa_cmp_topk_kernel(q_ref, kcv_ref, cpos_ref, qc_ref, sm_ref, ov_ref, o_ref, sel_ref, act_ref, qa_ref, imp_ref,
                         *, tq, n16, n_sel, n_top, cw):
    t0 = pl.program_id(1) * tq
    tpos = t0 + lax.broadcasted_iota(jnp.int32, (1, tq), 1)
    any_valid = (tpos >= NSA_CMP_LEN - 1).astype(F32)
    gates = 1.0 / (1.0 + jnp.exp(-sm_ref[...]))
    blk = lax.broadcasted_iota(jnp.int32, (n_sel, 1), 0)
    blk_f = blk.astype(F32)
    cur = tpos // NSA_SEL_BLOCK
    forced = (blk == 0) | (blk == cur) | (blk == cur - 1)
    future = blk * NSA_SEL_BLOCK > tpos
    blk_row = lax.broadcasted_iota(jnp.int32, (1, n_sel), 1)

    def attend(nc):
        segs = _cmp_segments(nc)
        bias = []
        for a, b in segs:
            cmp_end = (a + lax.broadcasted_iota(jnp.int32, (b - a, 1), 0)) * NSA_CMP_STRIDE + (NSA_CMP_LEN - 1)
            bias.append(jnp.where(cmp_end <= tpos, 0.0, -BIG))
        for g in range(NSA_GROUPS):
            kc = [jnp.concatenate([kcv_ref[a:b, g * NSA_DH:(g + 1) * NSA_DH].astype(BF16), cpos_ref[a:b, :]], axis=-1)
                  for a, b in segs]
            vo = [jnp.concatenate([kcv_ref[a:b, NSA_KVW + g * NSA_DH:NSA_KVW + (g + 1) * NSA_DH].T.astype(BF16),
                                   ov_ref[:, a:b]], axis=0) for a, b in segs]
            _stack_queries(q_ref, qc_ref, qa_ref, g, tq)
            imp = None
            for h in range(NSA_HPG):
                hh = g * NSA_HPG + h
                qa = qa_ref[h * tq:(h + 1) * tq, :]
                s = [_dot_nt(kc[i], qa) + bias[i] for i in range(len(segs))]
                m = functools.reduce(jnp.maximum, [jnp.max(x, axis=0, keepdims=True) for x in s])
                e = [jnp.exp(x - m) for x in s]
                inv = any_valid / sum(jnp.sum(x, axis=0, keepdims=True) for x in e)
                r = sum(_dot(vo[i], e[i].astype(BF16)) for i in range(len(segs))) * inv
                imp = r[NSA_DH:] if imp is None else imp + r[NSA_DH:]
                o_ref[:, _head_cols(hh)] = gates[:, SM_NG + hh:SM_NG + hh + 1] * r[:NSA_DH].T
            imp_ref[g] = imp

    need = (t0 + tq) // NSA_CMP_STRIDE
    nchunks = n16 // cw
    for k in range(1, nchunks + 1):
        lo = (k - 1) * cw
        cond = (need > lo) & (need <= k * cw) if k < nchunks else need > lo
        pl.when(cond)(functools.partial(attend, k * cw))

    for g in range(NSA_GROUPS):
        work = jnp.where(forced, BIG, jnp.where(future, -BIG, imp_ref[g]))
        sel_t = jnp.zeros((n_sel, tq), F32)
        for _ in range(n_top):
            top = jnp.max(work, axis=0, keepdims=True)
            idx = jnp.min(jnp.where(work == top, blk_f, float(n_sel)), axis=0, keepdims=True)
            pick = blk_f == idx
            sel_t = jnp.where(pick, 1.0, sel_t)
            work = jnp.where(pick, NEG_HUGE, work)
        sel = sel_t.T
        sel_ref[:, g * n_sel:(g + 1) * n_sel] = sel.astype(sel_ref.dtype)
        union = jnp.max(sel, axis=0, keepdims=True)
        act_ref[:, g * n_sel:(g + 1) * n_sel] = jnp.where(blk_row * NSA_SEL_BLOCK < t0, union, 0.0)


def _nsa_overlap(n16, n_sel):
    n_cmp = n16 - 1
    tok = (np.arange(n_cmp) * NSA_CMP_STRIDE)[:, None] + np.arange(NSA_CMP_LEN)[None, :]
    ov = np.zeros((n16, n_sel), np.float32)
    np.add.at(ov, (np.repeat(np.arange(n_cmp), NSA_CMP_LEN), (tok // NSA_SEL_BLOCK).ravel()), 1.0 / NSA_CMP_LEN)
    return jnp.asarray(ov.T, BF16)


def nsa_cmp_topk(proj3, kcv, tq=256):
    bsz, T, _ = proj3.shape
    tq = min(tq, T)
    n16 = T // NSA_CMP_STRIDE
    n_sel = T // NSA_SEL_BLOCK
    n_top = min(NSA_TOPN, n_sel)
    cmp_end = np.arange(n16) * NSA_CMP_STRIDE + NSA_CMP_LEN - 1
    cpos = jnp.asarray(_pos_cols(cmp_end, np.ones(n16, bool)), BF16)
    return pl.pallas_call(
        functools.partial(_nsa_cmp_topk_kernel, tq=tq, n16=n16, n_sel=n_sel, n_top=n_top, cw=min(LANES // 2, n16)),
        grid=(bsz, T // tq),
        in_specs=[pl.BlockSpec((None, tq, NSA_QW), lambda b, i: (b, i, OFF_NQ // NSA_QW)),
                  pl.BlockSpec((None, n16, KV_W), lambda b, i: (b, 0, 0)),
                  pl.BlockSpec((n16, EXTRA_W), lambda b, i: (0, 0)),
                  pl.BlockSpec((NSA_HEADS, EXTRA_W), lambda b, i: (0, 0)),
                  pl.BlockSpec((None, tq, LANES), lambda b, i: (b, i, OFF_SMALL // LANES)),
                  pl.BlockSpec((n_sel, n16), lambda b, i: (0, 0))],
        out_specs=[pl.BlockSpec((None, tq, NSA_QW), lambda b, i: (b, i, 0)),
                   pl.BlockSpec((None, tq, NSA_GROUPS * n_sel), lambda b, i: (b, i, 0)),
                   pl.BlockSpec((None, None, 1, NSA_GROUPS * n_sel), lambda b, i: (b, i, 0, 0))],
        out_shape=[jax.ShapeDtypeStruct((bsz, T, NSA_QW), F32),
                   jax.ShapeDtypeStruct((bsz, T, NSA_GROUPS * n_sel), BF16),
                   jax.ShapeDtypeStruct((bsz, T // tq, 1, NSA_GROUPS * n_sel), F32)],
        scratch_shapes=[pltpu.VMEM((NSA_HPG * tq, NSA_DH + EXTRA_W), BF16),
                        pltpu.VMEM((NSA_GROUPS, n_sel, tq), F32)],
        compiler_params=_cparams(("parallel", "parallel")),
        name="nsa_cmp_topk",
    )(proj3, kcv, cpos, _nsa_query_consts(), proj3, _nsa_overlap(n16, n_sel))


def _nsa_window_kernel(q_ref, k0_ref, k1_ref, k2_ref, v0_ref, v1_ref, v2_ref, qc_ref, sm_ref, prev_ref, o_ref,
                       qa_ref, *, tq):
    nb = NSA_WINDOW // tq + 1
    i = pl.program_id(1)
    tpos = i * tq + lax.broadcasted_iota(jnp.int32, (1, tq), 1)
    kpos = (i - (nb - 1)) * tq + lax.broadcasted_iota(jnp.int32, (nb * tq, 1), 0)
    d = tpos - kpos
    valid = (d >= 0) & (d < NSA_WINDOW) & (kpos >= 0)
    hpc = NSA_HPG // WIN_CHAINS
    bias = jnp.concatenate([jnp.where(valid, 0.0, -BIG)] * hpc, axis=1)
    gates = 1.0 / (1.0 + jnp.exp(-sm_ref[...]))
    for g in range(NSA_GROUPS):
        kw = jnp.concatenate([k0_ref[g], k1_ref[g], k2_ref[g]], axis=0)
        vw = jnp.concatenate([v0_ref[g], v1_ref[g], v2_ref[g]], axis=0)
        vw_t = vw.astype(F32).T.astype(BF16)
        _stack_queries(q_ref, qc_ref, qa_ref, g, tq)
        for c in range(WIN_CHAINS):
            s = _dot_nt(kw, qa_ref[c * hpc * tq:(c + 1) * hpc * tq, :]) + bias
            e = jnp.exp(s - jnp.max(s, axis=0, keepdims=True))
            acc = _dot(vw_t, e.astype(BF16))
            o = (acc[:NSA_DH, :] * (1.0 / acc[NSA_DH:NSA_DH + 1, :])).T
            for h in range(hpc):
                hh = g * NSA_HPG + c * hpc + h
                gate = gates[:, SM_NG + 2 * NSA_HEADS + hh:SM_NG + 2 * NSA_HEADS + hh + 1]
                o_ref[:, _head_cols(hh)] = prev_ref[:, _head_cols(hh)] + gate * o[h * tq:(h + 1) * tq]


def nsa_window(proj3, prev, tq=256):
    bsz, T, _ = proj3.shape
    tq = min(tq, T)
    assert NSA_WINDOW % tq == 0 and NSA_WINDOW // tq == 2
    tp = min(KPREP_ROWS, T)
    kc, vc = _nsa_key_consts(T, tp, onehot=False)
    kaug, vaug = nsa_kprep(proj3, kc, vc, tp, (OFF_NKV + 2 * KV_W) // KV_W)
    wk = kaug.shape[-1]

    def kvspec(back, w):
        return pl.BlockSpec((None, NSA_GROUPS, tq, w), lambda b, i, back=back: (b, 0, jnp.maximum(i - back, 0), 0))

    return pl.pallas_call(
        functools.partial(_nsa_window_kernel, tq=tq),
        grid=(bsz, T // tq),
        in_specs=[pl.BlockSpec((None, tq, NSA_QW), lambda b, i: (b, i, OFF_NQ // NSA_QW)),
                  kvspec(2, wk), kvspec(1, wk), kvspec(0, wk),
                  kvspec(2, 2 * NSA_DH), kvspec(1, 2 * NSA_DH), kvspec(0, 2 * NSA_DH),
                  pl.BlockSpec((NSA_HEADS, EXTRA_W), lambda b, i: (0, 0)),
                  pl.BlockSpec((None, tq, LANES), lambda b, i: (b, i, OFF_SMALL // LANES)),
                  pl.BlockSpec((None, tq, NSA_QW), lambda b, i: (b, i, 0))],
        out_specs=pl.BlockSpec((None, tq, NSA_QW), lambda b, i: (b, i, 0)),
        out_shape=jax.ShapeDtypeStruct((bsz, T, NSA_QW), F32),
        scratch_shapes=[pltpu.VMEM((NSA_HPG * tq, wk), BF16)],
        compiler_params=_cparams(("parallel", "parallel")),
        name="nsa_window",
    )(proj3, kaug, kaug, kaug, vaug, vaug, vaug, _nsa_query_consts(), proj3, prev)


MASK_BIG = 1e30
SEL_STEP = 8
KPREP_ROWS = 1024
NSA_CHAINS = 1
WIN_CHAINS = 2
POS_HI = 128
EXTRA_W = NSA_DH


def _bf16_pieces(x):
    x = np.float32(x)
    out = []
    for _ in range(3):
        p = np.float32(np.asarray(x, dtype=jnp.bfloat16))
        out.append(p)
        x = np.float32(x - p)
    return out


def _pos_cols(pos, real):
    c = np.zeros((len(pos), EXTRA_W), np.float32)
    for j in range(3):
        c[real, j] = (pos[real] // POS_HI) * POS_HI
        c[real, 3 + j] = pos[real] % POS_HI
    c[~real, 6] = 1.0
    return c


def _nsa_query_consts():
    qc = np.zeros((NSA_HEADS, EXTRA_W), np.float32)
    for hh in range(NSA_HEADS):
        qc[hh, 0:3] = qc[hh, 3:6] = _bf16_pieces(NSA_SLOPES[hh])
        qc[hh, 6] = -MASK_BIG
    return jnp.asarray(qc, BF16)


def _nsa_key_consts(T, pad, onehot):
    n_sel = T // NSA_SEL_BLOCK
    pos = np.arange(T + pad)
    real = pos < T
    kc = _pos_cols(pos, real)
    if onehot:
        oh = np.zeros((T + pad, n_sel), np.float32)
        oh[pos[real], pos[real] // NSA_SEL_BLOCK] = 1.0
        kc = np.concatenate([kc, oh], axis=1)
    vc = np.zeros((T + pad, NSA_DH), np.float32)
    vc[real, 0] = 1.0
    return jnp.asarray(kc, BF16), jnp.asarray(vc, BF16)


def _nsa_kprep_kernel(kv_ref, kc_ref, vc_ref, ka_ref, va_ref):
    real = pl.program_id(1) < pl.num_programs(1) - 1
    kv = (kv_ref[...] * jnp.where(real, 1.0, 0.0)).astype(BF16)
    for g in range(NSA_GROUPS):
        ka_ref[g] = jnp.concatenate([kv[:, g * NSA_DH:(g + 1) * NSA_DH], kc_ref[...]], axis=-1)
        va_ref[g] = jnp.concatenate([kv[:, NSA_KVW + g * NSA_DH:NSA_KVW + (g + 1) * NSA_DH], vc_ref[...]], axis=-1)


def nsa_kprep(proj3, kc, vc, tb, cs):
    bsz, T, _ = proj3.shape
    nt = T // tb
    wk = NSA_DH + kc.shape[1]
    return pl.pallas_call(
        _nsa_kprep_kernel,
        grid=(bsz, nt + 1),
        in_specs=[pl.BlockSpec((None, tb, KV_W), lambda b, t: (b, jnp.minimum(t, nt - 1), cs)),
                  pl.BlockSpec((tb, kc.shape[1]), lambda b, t: (t, 0)),
                  pl.BlockSpec((tb, NSA_DH), lambda b, t: (t, 0))],
        out_specs=[pl.BlockSpec((None, NSA_GROUPS, tb, wk), lambda b, t: (b, 0, t, 0)),
                   pl.BlockSpec((None, NSA_GROUPS, tb, 2 * NSA_DH), lambda b, t: (b, 0, t, 0))],
        out_shape=[jax.ShapeDtypeStruct((bsz, NSA_GROUPS, T + tb, wk), BF16),
                   jax.ShapeDtypeStruct((bsz, NSA_GROUPS, T + tb, 2 * NSA_DH), BF16)],
        compiler_params=_cparams(("parallel", "parallel")),
        name="nsa_kprep",
    )(proj3, kc, vc)


def _nsa_select_kernel(ids_ref, cnt_ref, q_ref, ka_ref, va_ref, sel_ref, qc_ref, sm_ref, prev_ref, o_ref,
                       qa_ref, kt_ref, vt_ref, m_ref, acc_ref, *, tq, n_sel, lmax):
    b = pl.program_id(0)
    i = pl.program_id(1)
    B = NSA_SEL_BLOCK
    kpos = i * tq + lax.broadcasted_iota(jnp.int32, (tq, 1), 0)
    tpos = i * tq + lax.broadcasted_iota(jnp.int32, (1, tq), 1)
    causal_bias = jnp.where(kpos <= tpos, 0.0, -BIG)
    hpc = NSA_HPG // NSA_CHAINS
    chains = [slice(c * hpc * tq, (c + 1) * hpc * tq) for c in range(NSA_CHAINS)]
    causal_bias = jnp.concatenate([causal_bias] * hpc, axis=1)
    gates = 1.0 / (1.0 + jnp.exp(-sm_ref[...]))
    own = pl.ds(pl.multiple_of(i * tq, tq), tq)
    for g in range(NSA_GROUPS):
        mcols = ((sel_ref[:, g * n_sel:(g + 1) * n_sel].astype(F32) - 1.0) * MASK_BIG).astype(BF16)
        _stack_queries(q_ref, qc_ref, qa_ref, g, tq, shared=mcols)
        v_own = va_ref[g, own, :].astype(F32).T.astype(BF16)
        for rs in chains:
            s = _dot_nt(ka_ref[g, own, :], qa_ref[rs, :]) + causal_bias
            m0 = jnp.max(s, axis=0, keepdims=True)
            m_ref[:, rs] = m0
            acc_ref[:, rs] = _dot(v_own, jnp.exp(s - m0).astype(BF16))
        slot = (b * pl.num_programs(1) + i) * NSA_GROUPS + g

        def step(st, carry):
            for u in range(SEL_STEP):
                r0 = pl.multiple_of(ids_ref[slot * lmax + st * SEL_STEP + u] * B, B)
                kt_ref[u * B:(u + 1) * B, :] = ka_ref[g, pl.ds(r0, B), :]
                vt_ref[u * B:(u + 1) * B, :] = va_ref[g, pl.ds(r0, B), :]
            v_t = vt_ref[...].astype(F32).T.astype(BF16)
            for rs in chains:
                s = _dot_nt(kt_ref[...], qa_ref[rs, :])
                m_old = m_ref[:, rs]
                m_new = jnp.maximum(m_old, jnp.max(s, axis=0, keepdims=True))
                p = jnp.exp(s - m_new).astype(BF16)
                acc_ref[:, rs] = jnp.exp(m_old - m_new) * acc_ref[:, rs] + _dot(v_t, p)
                m_ref[:, rs] = m_new
            return carry

        lax.fori_loop(0, (cnt_ref[slot] + SEL_STEP - 1) // SEL_STEP, step, 0)
        acc = acc_ref[...]
        o = (acc[:NSA_DH, :] * (1.0 / acc[NSA_DH:NSA_DH + 1, :])).T
        for h in range(NSA_HPG):
            hh = g * NSA_HPG + h
            gate = gates[:, SM_NG + NSA_HEADS + hh:SM_NG + NSA_HEADS + hh + 1]
            o_ref[:, _head_cols(hh)] = (prev_ref[:, _head_cols(hh)] + gate * o[h * tq:(h + 1) * tq, :]
                                        ).astype(o_ref.dtype)


def nsa_select(proj3, sel, act, prev, tq=256):
    bsz, T, _ = proj3.shape
    tq = min(tq, T)
    nt = T // tq
    n_sel = T // NSA_SEL_BLOCK
    lmax = -(-n_sel // SEL_STEP) * SEL_STEP
    tp = min(KPREP_ROWS, T)
    kc, vc = _nsa_key_consts(T, tp, onehot=True)
    qc = _nsa_query_consts()
    kaug, vaug = nsa_kprep(proj3, kc, vc, tp, (OFF_NKV + KV_W) // KV_W)
    wk = kaug.shape[-1]
    on = (act.reshape(bsz, nt, NSA_GROUPS, n_sel) > 0.0).astype(jnp.int32)
    seen = jnp.cumsum(on, axis=-1)
    ids = jnp.sum(seen[..., None, :] <= jnp.arange(lmax, dtype=jnp.int32)[:, None], axis=-1, dtype=jnp.int32)
    ids = ids.reshape(-1)
    cnt = seen[..., -1].reshape(-1)
    rows = NSA_HPG * tq
    grid_spec = pltpu.PrefetchScalarGridSpec(
        num_scalar_prefetch=2,
        grid=(bsz, nt),
        in_specs=[pl.BlockSpec((None, tq, NSA_QW), lambda b, i, *_: (b, i, OFF_NQ // NSA_QW)),
                  pl.BlockSpec((None, NSA_GROUPS, T + tp, wk), lambda b, i, *_: (b, 0, 0, 0)),
                  pl.BlockSpec((None, NSA_GROUPS, T + tp, 2 * NSA_DH), lambda b, i, *_: (b, 0, 0, 0)),
                  pl.BlockSpec((None, tq, NSA_GROUPS * n_sel), lambda b, i, *_: (b, i, 0)),
                  pl.BlockSpec((NSA_HEADS, EXTRA_W), lambda b, i, *_: (0, 0)),
                  pl.BlockSpec((None, tq, LANES), lambda b, i, *_: (b, i, OFF_SMALL // LANES)),
                  pl.BlockSpec((None, tq, NSA_QW), lambda b, i, *_: (b, i, 0))],
        out_specs=pl.BlockSpec((None, tq, NSA_QW), lambda b, i, *_: (b, i, 0)),
        scratch_shapes=[pltpu.VMEM((rows, wk), BF16),
                        pltpu.VMEM((SEL_STEP * NSA_SEL_BLOCK, wk), BF16),
                        pltpu.VMEM((SEL_STEP * NSA_SEL_BLOCK, 2 * NSA_DH), BF16),
                        pltpu.VMEM((1, rows), F32), pltpu.VMEM((2 * NSA_DH, rows), F32)])
    return pl.pallas_call(
        functools.partial(_nsa_select_kernel, tq=tq, n_sel=n_sel, lmax=lmax),
        grid_spec=grid_spec,
        out_shape=jax.ShapeDtypeStruct((bsz, T, NSA_QW), BF16),
        compiler_params=_cparams(("parallel", "arbitrary")),
        name="nsa_select",
    )(ids, cnt, proj3, kaug, vaug, sel, qc, proj3, prev)


def nsa_mixer(proj3, cmp_pos, cmp_w1, cmp_w2):
    kcv = nsa_compress(proj3, cmp_pos, cmp_w1, cmp_w2)
    o_cmp, sel, act = nsa_cmp_topk(proj3, kcv)
    o_cw = nsa_window(proj3, o_cmp)
    return nsa_select(proj3, sel, act, o_cw)


def _pack_w_in(w_in):
    w_in = w_in.astype(BF16)
    (g_q, g_k, g_v, g_r, g_a, n_q, n_kv, n_g, s_z, s_xbc, s_dt, m_g) = jnp.split(w_in, SPLIT_POINTS, axis=-1)
    pad = jnp.zeros(w_in.shape[:-1] + (D_PK - OFF_SMALL - GLA_RANK - 3 * NSA_HEADS - SSM_HEADS,), w_in.dtype)
    return jnp.concatenate([m_g, g_q, g_k, g_v, g_r, n_q, s_z, s_xbc, n_kv, g_a, n_g, s_dt, pad], axis=-1)


def kernel(x, w_in, gla_a2, gla_a_bias, gla_norm, nsa_cmp_pos, nsa_cmp_w1, nsa_cmp_w2, ssm_conv_w, ssm_conv_b,
           ssm_dt_bias, ssm_a_log, ssm_d, ssm_norm, w_branch, w_out, norm_pre_mix, norm_post_mix, norm_pre_ffn,
           norm_post_ffn, w_ffn_gate, w_ffn_up, w_ffn_down):
    bsz, T, D = x.shape
    depth = w_in.shape[0]
    n = bsz * T
    w_in_pk = _pack_w_in(w_in)
    wa_pad = jnp.zeros((depth, LANES, GLA_KW), F32).at[:, SM_GA:SM_GA + GLA_RANK].set(gla_a2).astype(BF16)
    w_branch_b = w_branch.astype(BF16)
    w_out_b = w_out.astype(BF16)
    w_gate_b = w_ffn_gate.astype(BF16)
    w_up_b = w_ffn_up.astype(BF16)
    w_down_b = w_ffn_down.astype(BF16)
    xf = x.reshape(n, D)
    for l in range(depth):
        proj = norm_matmul(xf, norm_pre_mix[l], w_in_pk[l])
        proj3 = proj.reshape(bsz, T, D_PK)
        y_gla = gla_mixer(proj3, wa_pad[l], gla_a_bias[l], gla_norm[l])
        y_nsa = nsa_mixer(proj3, nsa_cmp_pos[l], nsa_cmp_w1[l], nsa_cmp_w2[l])
        y_ssm = ssd_mixer(proj3, ssm_conv_w[l], ssm_conv_b[l], ssm_dt_bias[l], ssm_a_log[l], ssm_d[l], ssm_norm[l])
        merged = merge_branches(y_gla.reshape(n, BRANCH_W), y_nsa.reshape(n, BRANCH_W), y_ssm.reshape(n, BRANCH_W),
                                w_branch_b[l], proj)
        xf = proj_norm_residual(merged, w_out_b[l], xf, norm_post_mix[l], tm=512, tk=D)
        act = ffn_up(xf, norm_pre_ffn[l], w_gate_b[l], w_up_b[l])
        xf = proj_norm_residual(act, w_down_b[l], xf, norm_post_ffn[l], tm=1024, tk=512)
    return xf.reshape(bsz, T, D)
```

```python
import functools

import numpy as np
import jax
import jax.numpy as jnp
from jax import lax
from jax.experimental import pallas as pl
from jax.experimental.pallas import tpu as pltpu

F32 = jnp.float32
BF16 = jnp.bfloat16

D_MODEL = 2048
EPS = 1e-6
N_BRANCH = 3
BRANCH_W = 1024
GLA_HEADS, GLA_DK, GLA_DV, GLA_RANK, GLA_TAU, GLA_CHUNK = 4, 256, 256, 16, 16.0, 64
GLA_KW = GLA_HEADS * GLA_DK
GLA_VW = GLA_HEADS * GLA_DV
NSA_HEADS, NSA_GROUPS, NSA_DH = 16, 2, 64
NSA_HPG = NSA_HEADS // NSA_GROUPS
NSA_QW = NSA_HEADS * NSA_DH
NSA_KVW = NSA_GROUPS * NSA_DH
NSA_CMP_LEN, NSA_CMP_STRIDE, NSA_SEL_BLOCK, NSA_TOPN, NSA_WINDOW = 32, 16, 64, 16, 512
BIG = 1e30
SSM_HEADS, SSM_HEADDIM, SSM_GROUPS, SSM_STATE, SSM_CONV, SSM_CHUNK = 16, 64, 4, 128, 4, 64
SSM_INNER = SSM_HEADS * SSM_HEADDIM
SSM_HPG = SSM_HEADS // SSM_GROUPS
SSM_CONV_CH = SSM_INNER + 2 * SSM_GROUPS * SSM_STATE
D_FF = ((8 * D_MODEL // 3 + 255) // 256) * 256
IN_SIZES = (GLA_KW, GLA_KW, GLA_VW, GLA_VW, GLA_RANK, NSA_QW, 6 * NSA_KVW, 3 * NSA_HEADS,
            SSM_INNER, SSM_CONV_CH, SSM_HEADS, N_BRANCH * D_MODEL)
SPLIT_POINTS = tuple(int(v) for v in np.cumsum(IN_SIZES)[:-1])

LANES = 128
VMEM_LIMIT = 56 * 1024 * 1024

OFF_MG = 0
OFF_GQ = OFF_MG + N_BRANCH * D_MODEL
OFF_GK = OFF_GQ + GLA_KW
OFF_GV = OFF_GK + GLA_KW
OFF_GR = OFF_GV + GLA_VW
OFF_NQ = OFF_GR + GLA_VW
OFF_SZ = OFF_NQ + NSA_QW
OFF_SX = OFF_SZ + SSM_INNER
OFF_SB = OFF_SX + SSM_INNER
OFF_SC = OFF_SB + SSM_GROUPS * SSM_STATE
OFF_NKV = OFF_SC + SSM_GROUPS * SSM_STATE
OFF_SMALL = OFF_NKV + 6 * NSA_KVW
D_PK = OFF_SMALL + 2 * LANES
SM_GA = 0
SM_NG = 16
SM_DT = 64


def _cparams(sem):
    return pltpu.CompilerParams(dimension_semantics=sem, vmem_limit_bytes=VMEM_LIMIT)


def _split3(x):
    hi = x.astype(BF16)
    r1 = x - hi.astype(F32)
    mid = r1.astype(BF16)
    lo = (r1 - mid.astype(F32)).astype(BF16)
    return hi, mid, lo


def _dot(a, b):
    return jnp.dot(a, b, preferred_element_type=F32)


def _dot_nt(a, b):
    return lax.dot_general(a, b, (((1,), (1,)), ((), ())), preferred_element_type=F32)


def _dot01_left(m01, x):
    hi, mid, lo = _split3(x)
    return _dot(m01, hi) + _dot(m01, mid) + _dot(m01, lo)


def _dot01_right(x, m01):
    hi, mid, lo = _split3(x)
    return _dot(hi, m01) + _dot(mid, m01) + _dot(lo, m01)


def _silu(x):
    return x / (1.0 + jnp.exp(-x))


def _norm_rows(x_ref, g_ref, h_ref):
    @pl.when(pl.program_id(1) == 0)
    def _():
        x = x_ref[...]
        y = x * lax.rsqrt(jnp.mean(x * x, axis=-1, keepdims=True) + EPS)
        h_ref[...] = (y * g_ref[...]).astype(h_ref.dtype)


def _norm_mm_kernel(x_ref, g_ref, w_ref, o_ref, h_ref):
    _norm_rows(x_ref, g_ref, h_ref)
    o_ref[...] = _dot(h_ref[...], w_ref[...]).astype(o_ref.dtype)


def norm_matmul(x, g, w, out_dtype=F32, tm=1024, tn=1024):
    m, k = x.shape
    n = w.shape[1]
    tm = min(tm, m)
    tn = min(tn, n)
    return pl.pallas_call(
        _norm_mm_kernel,
        grid=(m // tm, n // tn),
        in_specs=[pl.BlockSpec((tm, k), lambda i, j: (i, 0)), pl.BlockSpec((1, k), lambda i, j: (0, 0)),
                  pl.BlockSpec((k, tn), lambda i, j: (0, j))],
        out_specs=pl.BlockSpec((tm, tn), lambda i, j: (i, j)),
        out_shape=jax.ShapeDtypeStruct((m, n), out_dtype),
        scratch_shapes=[pltpu.VMEM((tm, k), BF16)],
        compiler_params=_cparams(("parallel", "arbitrary")),
        name="in_proj",
    )(x, g.reshape(1, k), w)


def _merge_kernel(yg_ref, yn_ref, ys_ref, wg_ref, wn_ref, ws_ref, g0_ref, g1_ref, g2_ref, o_ref):
    def gate(ref):
        return 1.0 / (1.0 + jnp.exp(-ref[...]))

    acc = gate(g0_ref) * _dot(yg_ref[...], wg_ref[...])
    acc += gate(g1_ref) * _dot(yn_ref[...], wn_ref[...])
    acc += gate(g2_ref) * _dot(ys_ref[...], ws_ref[...])
    o_ref[...] = acc.astype(o_ref.dtype)


def merge_branches(y_gla, y_nsa, y_ssm, w_branch, proj, tm=1024, tn=512):
    m = y_gla.shape[0]
    d = w_branch.shape[-1]
    tm = min(tm, m)
    nj = d // tn
    ys = pl.BlockSpec((tm, BRANCH_W), lambda i, j: (i, 0))

    def wspec(b):
        return pl.BlockSpec((None, BRANCH_W, tn), lambda i, j, b=b: (b, 0, j))

    def gspec(b):
        return pl.BlockSpec((tm, tn), lambda i, j, b=b: (i, (OFF_MG + b * D_MODEL) // tn + j))

    return pl.pallas_call(
        _merge_kernel,
        grid=(m // tm, nj),
        in_specs=[ys, ys, ys, wspec(0), wspec(1), wspec(2), gspec(0), gspec(1), gspec(2)],
        out_specs=pl.BlockSpec((tm, tn), lambda i, j: (i, j)),
        out_shape=jax.ShapeDtypeStruct((m, d), BF16),
        compiler_params=_cparams(("parallel", "parallel")),
        name="merge",
    )(y_gla, y_nsa, y_ssm, w_branch, w_branch, w_branch, proj, proj, proj)


def _proj_norm_res_kernel(a_ref, w_ref, x_ref, g_ref, o_ref, acc_ref):
    k = pl.program_id(1)

    @pl.when(k == 0)
    def _():
        acc_ref[...] = jnp.zeros_like(acc_ref)

    acc_ref[...] += _dot(a_ref[...], w_ref[...])

    @pl.when(k == pl.num_programs(1) - 1)
    def _():
        f = acc_ref[...]
        y = f * lax.rsqrt(jnp.mean(f * f, axis=-1, keepdims=True) + EPS)
        o_ref[...] = x_ref[...] + y * g_ref[...]


def proj_norm_residual(a, w, x, g, tm=1024, tk=512):
    m, kk = a.shape
    d = w.shape[1]
    tm = min(tm, m)
    return pl.pallas_call(
        _proj_norm_res_kernel,
        grid=(m // tm, kk // tk),
        in_specs=[pl.BlockSpec((tm, tk), lambda i, k: (i, k)),
                  pl.BlockSpec((tk, d), lambda i, k: (k, 0)),
                  pl.BlockSpec((tm, d), lambda i, k: (i, 0)),
                  pl.BlockSpec((1, d), lambda i, k: (0, 0))],
        out_specs=pl.BlockSpec((tm, d), lambda i, k: (i, 0)),
        out_shape=jax.ShapeDtypeStruct((m, d), F32),
        scratch_shapes=[pltpu.VMEM((tm, d), F32)],
        compiler_params=_cparams(("parallel", "arbitrary")),
        name="proj_norm_res",
    )(a, w, x, g.reshape(1, d))


def _ffn_up_kernel(x_ref, g_ref, wg_ref, wu_ref, o_ref, h_ref):
    _norm_rows(x_ref, g_ref, h_ref)
    h = h_ref[...]
    a = _dot(h, wg_ref[...])
    u = _dot(h, wu_ref[...])
    o_ref[...] = (_silu(a) * u).astype(o_ref.dtype)


def ffn_up(x, g, wg, wu, tm=1024, tn=512):
    m, k = x.shape
    n = wg.shape[1]
    tm = min(tm, m)
    return pl.pallas_call(
        _ffn_up_kernel,
        grid=(m // tm, n // tn),
        in_specs=[pl.BlockSpec((tm, k), lambda i, j: (i, 0)),
                  pl.BlockSpec((1, k), lambda i, j: (0, 0)),
                  pl.BlockSpec((k, tn), lambda i, j: (0, j)),
                  pl.BlockSpec((k, tn), lambda i, j: (0, j))],
        out_specs=pl.BlockSpec((tm, tn), lambda i, j: (i, j)),
        out_shape=jax.ShapeDtypeStruct((m, n), BF16),
        scratch_shapes=[pltpu.VMEM((tm, k), BF16)],
        compiler_params=_cparams(("parallel", "arbitrary")),
        name="ffn_up",
    )(x, g.reshape(1, k), wg, wu)


def _gla_kernel(q_ref, k_ref, v_ref, r_ref, sm_ref, wa_ref, ba_ref, ng_ref, tri_ref, same_ref, o_ref,
                st_ref, *, tb):
    C = GLA_CHUNK

    @pl.when(pl.program_id(1) == 0)
    def _():
        st_ref[...] = jnp.zeros_like(st_ref)

    pre = _dot(sm_ref[...].astype(BF16), wa_ref[...]) + ba_ref[...]
    la = (jnp.minimum(pre, 0.0) - jnp.log1p(jnp.exp(-jnp.abs(pre)))) * (1.0 / GLA_TAU)
    tri = tri_ref[...]
    same = same_ref[...]
    ri = lax.broadcasted_iota(jnp.int32, (tb, tb), 0)
    ci = lax.broadcasted_iota(jnp.int32, (tb, tb), 1)
    causal = (ri >= ci) & (ri // C == ci // C)
    for h in range(GLA_HEADS):
        ck = slice(h * GLA_DK, (h + 1) * GLA_DK)
        cv = slice(h * GLA_DV, (h + 1) * GLA_DV)
        hi, mid, lo = _split3(la[:, ck])
        bcum = _dot(tri, hi) + _dot(tri, mid) + _dot(tri, lo)
        b_last = _dot(same, hi) + _dot(same, mid) + _dot(same, lo)
        q = q_ref[:, ck] * (GLA_DK ** -0.5)
        k = k_ref[:, ck]
        v = v_ref[:, cv]
        q_dec = (q * jnp.exp(bcum)).astype(BF16)
        k_inv = (k * jnp.exp(-bcum)).astype(BF16)
        k_end = (k * jnp.exp(b_last - bcum)).astype(BF16)
        att = jnp.where(causal, _dot_nt(q_dec, k_inv), 0.0)
        o_intra = _dot(att.astype(BF16), v.astype(BF16))
        st = st_ref[h]
        outs = []
        for c in range(tb // C):
            rows = slice(c * C, (c + 1) * C)
            outs.append(o_intra[rows] + _dot_nt(q_dec[rows], st.astype(BF16)))
            st = jnp.exp(b_last[c * C:c * C + 1, :]) * st + _dot(v[rows].T.astype(BF16), k_end[rows])
        st_ref[h] = st
        o = jnp.concatenate(outs, axis=0)
        y = o * lax.rsqrt(jnp.mean(o * o, axis=-1, keepdims=True) + EPS) * ng_ref[...]
        o_ref[:, cv] = (y * _silu(r_ref[:, cv])).astype(o_ref.dtype)


def _chunk_masks(tb, chunk):
    idx = np.arange(tb)
    same = (idx[:, None] // chunk) == (idx[None, :] // chunk)
    return jnp.asarray(same & (idx[:, None] >= idx[None, :]), BF16), jnp.asarray(same, BF16)


def gla_mixer(proj3, wa_pad, ba, norm_g, tb=256):
    bsz, T, _ = proj3.shape
    tb = min(tb, T)
    tri, same = _chunk_masks(tb, GLA_CHUNK)

    def col(off, w):
        return pl.BlockSpec((None, tb, w), lambda b, t, o=off // w: (b, t, o))

    def full(shape):
        return pl.BlockSpec(shape, lambda b, t: (0,) * len(shape))

    return pl.pallas_call(
        functools.partial(_gla_kernel, tb=tb),
        grid=(bsz, T // tb),
        in_specs=[col(OFF_GQ, GLA_KW), col(OFF_GK, GLA_KW), col(OFF_GV, GLA_VW), col(OFF_GR, GLA_VW),
                  col(OFF_SMALL, LANES),
                  full((LANES, GLA_KW)), full((1, GLA_KW)), full((1, GLA_DV)), full((tb, tb)), full((tb, tb))],
        out_specs=pl.BlockSpec((None, tb, GLA_VW), lambda b, t: (b, t, 0)),
        out_shape=jax.ShapeDtypeStruct((bsz, T, GLA_VW), BF16),
        scratch_shapes=[pltpu.VMEM((GLA_HEADS, GLA_DV, GLA_DK), F32)],
        compiler_params=_cparams(("parallel", "arbitrary")),
        name="gla",
    )(proj3, proj3, proj3, proj3, proj3, wa_pad, ba.reshape(1, GLA_KW), norm_g.reshape(1, GLA_DV), tri, same)


def _ssd_kernel(z_ref, x_ref, bm_ref, cm_ref, sm_ref, cwx_ref, cwb_ref, cwc_ref, cbx_ref, cbb_ref, cbc_ref,
                dtb_ref, alog_ref, dskip_ref, ng_ref, tri_ref, exp_ref, o_ref,
                st_ref, extx_ref, extb_ref, extc_ref, xa_ref, ba_ref, ca_ref, dt_ref, a_ref, *, tb, nchunk):
    L = SSM_CHUNK
    GW = SSM_HPG * SSM_HEADDIM
    NS = SSM_STATE
    first = pl.program_id(1) == 0

    @pl.when(first)
    def _():
        st_ref[...] = jnp.zeros_like(st_ref)

    def conv_silu(src_ref, ext_ref, w_ref, b_ref, dst_ref):
        @pl.when(first)
        def _():
            ext_ref[0:8, :] = jnp.zeros((8, ext_ref.shape[1]), F32)

        @pl.when(jnp.logical_not(first))
        def _():
            ext_ref[0:8, :] = ext_ref[tb:tb + 8, :]

        ext_ref[8:8 + tb, :] = src_ref[...]
        acc = b_ref[...] + w_ref[SSM_CONV - 1:SSM_CONV, :] * ext_ref[8:8 + tb, :]
        for j in range(1, SSM_CONV):
            acc = acc + w_ref[SSM_CONV - 1 - j:SSM_CONV - j, :] * ext_ref[8 - j:8 - j + tb, :]
        dst_ref[...] = _silu(acc)

    conv_silu(x_ref, extx_ref, cwx_ref, cbx_ref, xa_ref)
    conv_silu(bm_ref, extb_ref, cwb_ref, cbb_ref, ba_ref)
    conv_silu(cm_ref, extc_ref, cwc_ref, cbc_ref, ca_ref)

    v = sm_ref[...] + dtb_ref[...]
    dt = jnp.maximum(v, 0.0) + jnp.log1p(jnp.exp(-jnp.abs(v)))
    dt_ref[...] = dt
    a_ref[...] = dt * (-jnp.exp(alog_ref[...]))

    tri = tri_ref[...]
    row = lax.broadcasted_iota(jnp.int32, (L, GW), 0)
    lane = lax.broadcasted_iota(jnp.int32, (L, GW), 1)
    lane_in = jnp.bitwise_and(lane, SSM_HEADDIM - 1)
    eye_t = (lane_in == row).astype(F32)
    tril_t = lane_in <= row
    bd_mask = (lax.broadcasted_iota(jnp.int32, (GW, GW), 0) // L
               == lax.broadcasted_iota(jnp.int32, (GW, GW), 1) // SSM_HEADDIM).astype(F32)

    states = [st_ref[g] for g in range(SSM_GROUPS)]
    for c in range(nchunk):
        rows = slice(c * L, (c + 1) * L)
        cum128 = _dot01_left(tri, a_ref[rows, :])
        dt_c = dt_ref[rows, :]
        for g in range(SSM_GROUPS):
            cs = slice(g * GW, (g + 1) * GW)
            e_g = exp_ref[:, cs]
            cum_e = _dot01_right(cum128, e_g)
            dt_e = _dot01_right(dt_c, e_g)
            cum_last = cum_e[L - 1:L, :]
            r_row = jnp.sum(cum_e * eye_t, axis=0, keepdims=True)
            decay = jnp.exp(jnp.where(tril_t, cum_e - r_row, -jnp.inf))
            x_g = xa_ref[rows, cs]
            xdt = x_g * dt_e
            b_g = ba_ref[rows, g * NS:(g + 1) * NS]
            c_g = ca_ref[rows, g * NS:(g + 1) * NS].astype(BF16)
            bb = b_g.astype(BF16)
            cb_t = _dot_nt(c_g, jnp.concatenate([bb] * SSM_HPG, axis=0))
            xdt_bd = (jnp.concatenate([xdt] * SSM_HPG, axis=0) * bd_mask).astype(BF16)
            y = _dot((cb_t * decay).astype(BF16), xdt_bd)
            st = states[g]
            y = y + _dot(c_g, st.astype(BF16)) * jnp.exp(cum_e)
            dend = jnp.exp(cum_last - cum_e)
            states[g] = jnp.exp(cum_last) * st + _dot(b_g.T.astype(BF16), (dend * xdt).astype(BF16))
            y = y + x_g * dskip_ref[:, cs]
            y = y * _silu(z_ref[rows, cs])
            y = y * lax.rsqrt(jnp.mean(y * y, axis=-1, keepdims=True) + EPS) * ng_ref[:, cs]
            o_ref[rows, cs] = y.astype(o_ref.dtype)
    for g in range(SSM_GROUPS):
        st_ref[g] = states[g]


def _ssd_consts():
    tri = jnp.asarray(np.tril(np.ones((SSM_CHUNK, SSM_CHUNK), np.float32)), BF16)
    e = np.zeros((LANES, SSM_INNER), np.float32)
    for h in range(SSM_HEADS):
        e[SM_DT + h, h * SSM_HEADDIM:(h + 1) * SSM_HEADDIM] = 1.0
    return tri, jnp.asarray(e, BF16)


def _small_row(v, off):
    return jnp.zeros((1, LANES), F32).at[0, off:off + v.shape[0]].set(v.astype(F32))


def ssd_mixer(proj3, conv_w, conv_b, dt_bias, a_log, d_skip, norm_g, tb=256):
    bsz, T, _ = proj3.shape
    tb = min(tb, T)
    nchunk = tb // SSM_CHUNK
    GN = SSM_GROUPS * SSM_STATE
    tri, expand = _ssd_consts()
    cwx, cwb, cwc = conv_w[:, :SSM_INNER], conv_w[:, SSM_INNER:SSM_INNER + GN], conv_w[:, SSM_INNER + GN:]
    cb2 = conv_b.reshape(1, SSM_CONV_CH)
    cbx, cbb, cbc = cb2[:, :SSM_INNER], cb2[:, SSM_INNER:SSM_INNER + GN], cb2[:, SSM_INNER + GN:]
    dtb = _small_row(dt_bias, SM_DT)
    alog = _small_row(a_log, SM_DT)
    dskip = jnp.repeat(d_skip.astype(F32), SSM_HEADDIM).reshape(1, SSM_INNER)

    def col(off, w):
        return pl.BlockSpec((None, tb, w), lambda b, t, o=off // w: (b, t, o))

    def full(shape):
        return pl.BlockSpec(shape, lambda b, t: (0,) * len(shape))

    return pl.pallas_call(
        functools.partial(_ssd_kernel, tb=tb, nchunk=nchunk),
        grid=(bsz, T // tb),
        in_specs=[col(OFF_SZ, SSM_INNER), col(OFF_SX, SSM_INNER), col(OFF_SB, GN), col(OFF_SC, GN),
                  col(OFF_SMALL, LANES),
                  full((SSM_CONV, SSM_INNER)), full((SSM_CONV, GN)), full((SSM_CONV, GN)),
                  full((1, SSM_INNER)), full((1, GN)), full((1, GN)),
                  full((1, LANES)), full((1, LANES)), full((1, SSM_INNER)), full((1, SSM_INNER)),
                  full((SSM_CHUNK, SSM_CHUNK)), full((LANES, SSM_INNER))],
        out_specs=pl.BlockSpec((None, tb, SSM_INNER), lambda b, t: (b, t, 0)),
        out_shape=jax.ShapeDtypeStruct((bsz, T, SSM_INNER), BF16),
        scratch_shapes=[pltpu.VMEM((SSM_GROUPS, SSM_STATE, SSM_HPG * SSM_HEADDIM), F32),
                        pltpu.VMEM((tb + 8, SSM_INNER), F32), pltpu.VMEM((tb + 8, GN), F32),
                        pltpu.VMEM((tb + 8, GN), F32),
                        pltpu.VMEM((tb, SSM_INNER), F32), pltpu.VMEM((tb, GN), F32), pltpu.VMEM((tb, GN), F32),
                        pltpu.VMEM((tb, LANES), F32), pltpu.VMEM((tb, LANES), F32)],
        compiler_params=_cparams(("parallel", "arbitrary")),
        name="ssd",
    )(proj3, proj3, proj3, proj3, proj3, cwx, cwb, cwc, cbx, cbb, cbc, dtb, alog, dskip,
      norm_g.reshape(1, SSM_INNER), tri, expand)


NSA_SLOPES = tuple(float(np.float32(2.0 ** (-8.0 * (i + 1) / NSA_HEADS))) for i in range(NSA_HEADS))
NSA_SCALE = NSA_DH ** -0.5
KV_W = 2 * NSA_KVW
NEG_HUGE = -3.0e38


def _head_cols(hh):
    return slice(hh * NSA_DH, (hh + 1) * NSA_DH)


def _nsa_compress_kernel(k_ref, v_ref, pos_ref, bd1_ref, bd2_ref, o_ref, *, n16):
    S = NSA_CMP_STRIDE
    top = jnp.zeros((n16, KV_W), F32)
    bot = jnp.zeros((n16, KV_W), F32)
    for l in range(S):
        rows = pl.ds(l, n16, stride=S)
        x = jnp.concatenate([k_ref[rows, :], v_ref[rows, :]], axis=-1)
        top += _dot((x + pos_ref[l:l + 1, :]).astype(BF16), bd1_ref[l])
        bot += _dot((x + pos_ref[S + l:S + l + 1, :]).astype(BF16), bd1_ref[S + l])
    pre = top + pltpu.roll(bot, n16 - 1, axis=0)
    out = _dot(_silu(pre).astype(BF16), bd2_ref[...])
    row = lax.broadcasted_iota(jnp.int32, (n16, KV_W), 0)
    o_ref[...] = jnp.where(row < n16 - 1, out, 0.0)


def nsa_compress(proj3, cmp_pos, cmp_w1, cmp_w2):
    bsz, T, _ = proj3.shape
    n16 = T // NSA_CMP_STRIDE
    sel = np.array([0, 0, 1, 1])
    eye = jnp.eye(4, dtype=F32)
    w1r = cmp_w1.reshape(2, NSA_CMP_LEN, NSA_DH, NSA_DH)[sel]
    bd1 = jnp.einsum('ab,alde->ladbe', eye, w1r).reshape(NSA_CMP_LEN, KV_W, KV_W).astype(BF16)
    bd2 = jnp.einsum('ab,ade->adbe', eye, cmp_w2[sel]).reshape(KV_W, KV_W).astype(BF16)
    pos = jnp.concatenate([cmp_pos[0], cmp_pos[0], cmp_pos[1], cmp_pos[1]], axis=-1)
    return pl.pallas_call(
        functools.partial(_nsa_compress_kernel, n16=n16),
        grid=(bsz,),
        in_specs=[pl.BlockSpec((None, T, NSA_KVW), lambda b: (b, 0, OFF_NKV // NSA_KVW)),
                  pl.BlockSpec((None, T, NSA_KVW), lambda b: (b, 0, OFF_NKV // NSA_KVW + 1)),
                  pl.BlockSpec((NSA_CMP_LEN, KV_W), lambda b: (0, 0)),
                  pl.BlockSpec((NSA_CMP_LEN, KV_W, KV_W), lambda b: (0, 0, 0)),
                  pl.BlockSpec((KV_W, KV_W), lambda b: (0, 0))],
        out_specs=pl.BlockSpec((None, n16, KV_W), lambda b: (b, 0, 0)),
        out_shape=jax.ShapeDtypeStruct((bsz, n16, KV_W), F32),
        compiler_params=_cparams(("parallel",)),
        name="nsa_compress",
    )(proj3, proj3, pos, bd1, bd2)


CMP_SEG_MAX = 320


def _cmp_segments(nc):
    bad = 2 * LANES
    if nc <= CMP_SEG_MAX + NSA_DH and nc != bad:
        return [(0, nc)]
    first = CMP_SEG_MAX if nc > CMP_SEG_MAX else nc - NSA_DH
    return [(0, first), (first, nc)]


def _stack_queries(q_ref, qc_ref, qa_ref, g, tq, shared=None):
    for h in range(NSA_HPG):
        hh = g * NSA_HPG + h
        qh = (q_ref[:, _head_cols(hh)] * NSA_SCALE).astype(BF16)
        qx = jnp.broadcast_to(qc_ref[hh:hh + 1, :], (tq, EXTRA_W))
        qa_ref[h * tq:(h + 1) * tq, 0:NSA_DH + EXTRA_W] = jnp.concatenate([qh, qx], axis=-1)
        if shared is not None:
            qa_ref[h * tq:(h + 1) * tq, NSA_DH + EXTRA_W:] = shared


def _nsa_cmp_topk_kernel(q_ref, kcv_ref, cpos_ref, qc_ref, sm_ref, ov_ref, o_ref, sel_ref, act_ref, qa_ref, imp_ref,
                         *, tq, n16, n_sel, n_top, cw):
    t0 = pl.program_id(1) * tq
    tpos = t0 + lax.broadcasted_iota(jnp.int32, (1, tq), 1)
    any_valid = (tpos >= NSA_CMP_LEN - 1).astype(F32)
    gates = 1.0 / (1.0 + jnp.exp(-sm_ref[...]))
    blk = lax.broadcasted_iota(jnp.int32, (n_sel, 1), 0)
    blk_f = blk.astype(F32)
    cur = tpos // NSA_SEL_BLOCK
    forced = (blk == 0) | (blk == cur) | (blk == cur - 1)
    future = blk * NSA_SEL_BLOCK > tpos
    blk_row = lax.broadcasted_iota(jnp.int32, (1, n_sel), 1)

    def attend(nc):
        segs = _cmp_segments(nc)
        bias = []
        for a, b in segs:
            cmp_end = (a + lax.broadcasted_iota(jnp.int32, (b - a, 1), 0)) * NSA_CMP_STRIDE + (NSA_CMP_LEN - 1)
            bias.append(jnp.where(cmp_end <= tpos, 0.0, -BIG))
        for g in range(NSA_GROUPS):
            kc = [jnp.concatenate([kcv_ref[a:b, g * NSA_DH:(g + 1) * NSA_DH].astype(BF16), cpos_ref[a:b, :]], axis=-1)
                  for a, b in segs]
            vo = [jnp.concatenate([kcv_ref[a:b, NSA_KVW + g * NSA_DH:NSA_KVW + (g + 1) * NSA_DH].T.astype(BF16),
                                   ov_ref[:, a:b]], axis=0) for a, b in segs]
            _stack_queries(q_ref, qc_ref, qa_ref, g, tq)
            imp = None
            for h in range(NSA_HPG):
                hh = g * NSA_HPG + h
                qa = qa_ref[h * tq:(h + 1) * tq, :]
                s = [_dot_nt(kc[i], qa) + bias[i] for i in range(len(segs))]
                m = functools.reduce(jnp.maximum, [jnp.max(x, axis=0, keepdims=True) for x in s])
                e = [jnp.exp(x - m) for x in s]
                inv = any_valid / sum(jnp.sum(x, axis=0, keepdims=True) for x in e)
                r = sum(_dot(vo[i], e[i].astype(BF16)) for i in range(len(segs))) * inv
                imp = r[NSA_DH:] if imp is None else imp + r[NSA_DH:]
                o_ref[:, _head_cols(hh)] = gates[:, SM_NG + hh:SM_NG + hh + 1] * r[:NSA_DH].T
            imp_ref[g] = imp

    need = (t0 + tq) // NSA_CMP_STRIDE
    nchunks = n16 // cw
    for k in range(1, nchunks + 1):
        lo = (k - 1) * cw
        cond = (need > lo) & (need <= k * cw) if k < nchunks else need > lo
        pl.when(cond)(functools.partial(attend, k * cw))

    for g in range(NSA_GROUPS):
        work = jnp.where(forced, BIG, jnp.where(future, -BIG, imp_ref[g]))
        sel_t = jnp.zeros((n_sel, tq), F32)
        for _ in range(n_top):
            top = jnp.max(work, axis=0, keepdims=True)
            idx = jnp.min(jnp.where(work == top, blk_f, float(n_sel)), axis=0, keepdims=True)
            pick = blk_f == idx
            sel_t = jnp.where(pick, 1.0, sel_t)
            work = jnp.where(pick, NEG_HUGE, work)
        sel = sel_t.T
        sel_ref[:, g * n_sel:(g + 1) * n_sel] = sel.astype(sel_ref.dtype)
        union = jnp.max(sel, axis=0, keepdims=True)
        act_ref[:, g * n_sel:(g + 1) * n_sel] = jnp.where(blk_row * NSA_SEL_BLOCK < t0, union, 0.0)


def _nsa_overlap(n16, n_sel):
    n_cmp = n16 - 1
    tok = (np.arange(n_cmp) * NSA_CMP_STRIDE)[:, None] + np.arange(NSA_CMP_LEN)[None, :]
    ov = np.zeros((n16, n_sel), np.float32)
    np.add.at(ov, (np.repeat(np.arange(n_cmp), NSA_CMP_LEN), (tok // NSA_SEL_BLOCK).ravel()), 1.0 / NSA_CMP_LEN)
    return jnp.asarray(ov.T, BF16)


def nsa_cmp_topk(proj3, kcv, tq=256):
    bsz, T, _ = proj3.shape
    tq = min(tq, T)
    n16 = T // NSA_CMP_STRIDE
    n_sel = T // NSA_SEL_BLOCK
    n_top = min(NSA_TOPN, n_sel)
    cmp_end = np.arange(n16) * NSA_CMP_STRIDE + NSA_CMP_LEN - 1
    cpos = jnp.asarray(_pos_cols(cmp_end, np.ones(n16, bool)), BF16)
    return pl.pallas_call(
        functools.partial(_nsa_cmp_topk_kernel, tq=tq, n16=n16, n_sel=n_sel, n_top=n_top, cw=min(LANES // 2, n16)),
        grid=(bsz, T // tq),
        in_specs=[pl.BlockSpec((None, tq, NSA_QW), lambda b, i: (b, i, OFF_NQ // NSA_QW)),
                  pl.BlockSpec((None, n16, KV_W), lambda b, i: (b, 0, 0)),
                  pl.BlockSpec((n16, EXTRA_W), lambda b, i: (0, 0)),
                  pl.BlockSpec((NSA_HEADS, EXTRA_W), lambda b, i: (0, 0)),
                  pl.BlockSpec((None, tq, LANES), lambda b, i: (b, i, OFF_SMALL // LANES)),
                  pl.BlockSpec((n_sel, n16), lambda b, i: (0, 0))],
        out_specs=[pl.BlockSpec((None, tq, NSA_QW), lambda b, i: (b, i, 0)),
                   pl.BlockSpec((None, tq, NSA_GROUPS * n_sel), lambda b, i: (b, i, 0)),
                   pl.BlockSpec((None, None, 1, NSA_GROUPS * n_sel), lambda b, i: (b, i, 0, 0))],
        out_shape=[jax.ShapeDtypeStruct((bsz, T, NSA_QW), F32),
                   jax.ShapeDtypeStruct((bsz, T, NSA_GROUPS * n_sel), BF16),
                   jax.ShapeDtypeStruct((bsz, T // tq, 1, NSA_GROUPS * n_sel), F32)],
        scratch_shapes=[pltpu.VMEM((NSA_HPG * tq, NSA_DH + EXTRA_W), BF16),
                        pltpu.VMEM((NSA_GROUPS, n_sel, tq), F32)],
        compiler_params=_cparams(("parallel", "parallel")),
        name="nsa_cmp_topk",
    )(proj3, kcv, cpos, _nsa_query_consts(), proj3, _nsa_overlap(n16, n_sel))


def _nsa_window_kernel(q_ref, k0_ref, k1_ref, k2_ref, v0_ref, v1_ref, v2_ref, qc_ref, sm_ref, prev_ref, o_ref,
                       qa_ref, *, tq):
    nb = NSA_WINDOW // tq + 1
    i = pl.program_id(1)
    tpos = i * tq + lax.broadcasted_iota(jnp.int32, (1, tq), 1)
    kpos = (i - (nb - 1)) * tq + lax.broadcasted_iota(jnp.int32, (nb * tq, 1), 0)
    d = tpos - kpos
    valid = (d >= 0) & (d < NSA_WINDOW) & (kpos >= 0)
    hpc = NSA_HPG // WIN_CHAINS
    bias = jnp.concatenate([jnp.where(valid, 0.0, -BIG)] * hpc, axis=1)
    gates = 1.0 / (1.0 + jnp.exp(-sm_ref[...]))
    for g in range(NSA_GROUPS):
        kw = jnp.concatenate([k0_ref[g], k1_ref[g], k2_ref[g]], axis=0)
        vw = jnp.concatenate([v0_ref[g], v1_ref[g], v2_ref[g]], axis=0)
        vw_t = vw.astype(F32).T.astype(BF16)
        _stack_queries(q_ref, qc_ref, qa_ref, g, tq)
        for c in range(WIN_CHAINS):
            s = _dot_nt(kw, qa_ref[c * hpc * tq:(c + 1) * hpc * tq, :]) + bias
            e = jnp.exp(s - jnp.max(s, axis=0, keepdims=True))
            acc = _dot(vw_t, e.astype(BF16))
            o = (acc[:NSA_DH, :] * (1.0 / acc[NSA_DH:NSA_DH + 1, :])).T
            for h in range(hpc):
                hh = g * NSA_HPG + c * hpc + h
                gate = gates[:, SM_NG + 2 * NSA_HEADS + hh:SM_NG + 2 * NSA_HEADS + hh + 1]
                o_ref[:, _head_cols(hh)] = prev_ref[:, _head_cols(hh)] + gate * o[h * tq:(h + 1) * tq]


def nsa_window(proj3, prev, tq=256):
    bsz, T, _ = proj3.shape
    tq = min(tq, T)
    assert NSA_WINDOW % tq == 0 and NSA_WINDOW // tq == 2
    tp = min(KPREP_ROWS, T)
    kc, vc = _nsa_key_consts(T, tp, onehot=False)
    kaug, vaug = nsa_kprep(proj3, kc, vc, tp, (OFF_NKV + 2 * KV_W) // KV_W)
    wk = kaug.shape[-1]

    def kvspec(back, w):
        return pl.BlockSpec((None, NSA_GROUPS, tq, w), lambda b, i, back=back: (b, 0, jnp.maximum(i - back, 0), 0))

    return pl.pallas_call(
        functools.partial(_nsa_window_kernel, tq=tq),
        grid=(bsz, T // tq),
        in_specs=[pl.BlockSpec((None, tq, NSA_QW), lambda b, i: (b, i, OFF_NQ // NSA_QW)),
                  kvspec(2, wk), kvspec(1, wk), kvspec(0, wk),
                  kvspec(2, 2 * NSA_DH), kvspec(1, 2 * NSA_DH), kvspec(0, 2 * NSA_DH),
                  pl.BlockSpec((NSA_HEADS, EXTRA_W), lambda b, i: (0, 0)),
                  pl.BlockSpec((None, tq, LANES), lambda b, i: (b, i, OFF_SMALL // LANES)),
                  pl.BlockSpec((None, tq, NSA_QW), lambda b, i: (b, i, 0))],
        out_specs=pl.BlockSpec((None, tq, NSA_QW), lambda b, i: (b, i, 0)),
        out_shape=jax.ShapeDtypeStruct((bsz, T, NSA_QW), F32),
        scratch_shapes=[pltpu.VMEM((NSA_HPG * tq, wk), BF16)],
        compiler_params=_cparams(("parallel", "parallel")),
        name="nsa_window",
    )(proj3, kaug, kaug, kaug, vaug, vaug, vaug, _nsa_query_consts(), proj3, prev)


MASK_BIG = 1e30
SEL_STEP = 8
KPREP_ROWS = 1024
NSA_CHAINS = 1
WIN_CHAINS = 2
POS_HI = 128
EXTRA_W = NSA_DH


def _bf16_pieces(x):
    x = np.float32(x)
    out = []
    for _ in range(3):
        p = np.float32(np.asarray(x, dtype=jnp.bfloat16))
        out.append(p)
        x = np.float32(x - p)
    return out


def _pos_cols(pos, real):
    c = np.zeros((len(pos), EXTRA_W), np.float32)
    for j in range(3):
        c[real, j] = (pos[real] // POS_HI) * POS_HI
        c[real, 3 + j] = pos[real] % POS_HI
    c[~real, 6] = 1.0
    return c


def _nsa_query_consts():
    qc = np.zeros((NSA_HEADS, EXTRA_W), np.float32)
    for hh in range(NSA_HEADS):
        qc[hh, 0:3] = qc[hh, 3:6] = _bf16_pieces(NSA_SLOPES[hh])
        qc[hh, 6] = -MASK_BIG
    return jnp.asarray(qc, BF16)


def _nsa_key_consts(T, pad, onehot):
    n_sel = T // NSA_SEL_BLOCK
    pos = np.arange(T + pad)
    real = pos < T
    kc = _pos_cols(pos, real)
    if onehot:
        oh = np.zeros((T + pad, n_sel), np.float32)
        oh[pos[real], pos[real] // NSA_SEL_BLOCK] = 1.0
        kc = np.concatenate([kc, oh], axis=1)
    vc = np.zeros((T + pad, NSA_DH), np.float32)
    vc[real, 0] = 1.0
    return jnp.asarray(kc, BF16), jnp.asarray(vc, BF16)


def _nsa_kprep_kernel(kv_ref, kc_ref, vc_ref, ka_ref, va_ref):
    real = pl.program_id(1) < pl.num_programs(1) - 1
    kv = (kv_ref[...] * jnp.where(real, 1.0, 0.0)).astype(BF16)
    for g in range(NSA_GROUPS):
        ka_ref[g] = jnp.concatenate([kv[:, g * NSA_DH:(g + 1) * NSA_DH], kc_ref[...]], axis=-1)
        va_ref[g] = jnp.concatenate([kv[:, NSA_KVW + g * NSA_DH:NSA_KVW + (g + 1) * NSA_DH], vc_ref[...]], axis=-1)


def nsa_kprep(proj3, kc, vc, tb, cs):
    bsz, T, _ = proj3.shape
    nt = T // tb
    wk = NSA_DH + kc.shape[1]
    return pl.pallas_call(
        _nsa_kprep_kernel,
        grid=(bsz, nt + 1),
        in_specs=[pl.BlockSpec((None, tb, KV_W), lambda b, t: (b, jnp.minimum(t, nt - 1), cs)),
                  pl.BlockSpec((tb, kc.shape[1]), lambda b, t: (t, 0)),
                  pl.BlockSpec((tb, NSA_DH), lambda b, t: (t, 0))],
        out_specs=[pl.BlockSpec((None, NSA_GROUPS, tb, wk), lambda b, t: (b, 0, t, 0)),
                   pl.BlockSpec((None, NSA_GROUPS, tb, 2 * NSA_DH), lambda b, t: (b, 0, t, 0))],
        out_shape=[jax.ShapeDtypeStruct((bsz, NSA_GROUPS, T + tb, wk), BF16),
                   jax.ShapeDtypeStruct((bsz, NSA_GROUPS, T + tb, 2 * NSA_DH), BF16)],
        compiler_params=_cparams(("parallel", "parallel")),
        name="nsa_kprep",
    )(proj3, kc, vc)


def _nsa_select_kernel(ids_ref, cnt_ref, q_ref, ka_ref, va_ref, sel_ref, qc_ref, sm_ref, prev_ref, o_ref,
                       qa_ref, kt_ref, vt_ref, m_ref, acc_ref, *, tq, n_sel, lmax):
    b = pl.program_id(0)
    i = pl.program_id(1)
    B = NSA_SEL_BLOCK
    kpos = i * tq + lax.broadcasted_iota(jnp.int32, (tq, 1), 0)
    tpos = i * tq + lax.broadcasted_iota(jnp.int32, (1, tq), 1)
    causal_bias = jnp.where(kpos <= tpos, 0.0, -BIG)
    hpc = NSA_HPG // NSA_CHAINS
    chains = [slice(c * hpc * tq, (c + 1) * hpc * tq) for c in range(NSA_CHAINS)]
    causal_bias = jnp.concatenate([causal_bias] * hpc, axis=1)
    gates = 1.0 / (1.0 + jnp.exp(-sm_ref[...]))
    own = pl.ds(pl.multiple_of(i * tq, tq), tq)
    for g in range(NSA_GROUPS):
        mcols = ((sel_ref[:, g * n_sel:(g + 1) * n_sel].astype(F32) - 1.0) * MASK_BIG).astype(BF16)
        _stack_queries(q_ref, qc_ref, qa_ref, g, tq, shared=mcols)
        v_own = va_ref[g, own, :].astype(F32).T.astype(BF16)
        for rs in chains:
            s = _dot_nt(ka_ref[g, own, :], qa_ref[rs, :]) + causal_bias
            m0 = jnp.max(s, axis=0, keepdims=True)
            m_ref[:, rs] = m0
            acc_ref[:, rs] = _dot(v_own, jnp.exp(s - m0).astype(BF16))
        slot = (b * pl.num_programs(1) + i) * NSA_GROUPS + g

        def step(st, carry):
            for u in range(SEL_STEP):
                r0 = pl.multiple_of(ids_ref[slot * lmax + st * SEL_STEP + u] * B, B)
                kt_ref[u * B:(u + 1) * B, :] = ka_ref[g, pl.ds(r0, B), :]
                vt_ref[u * B:(u + 1) * B, :] = va_ref[g, pl.ds(r0, B), :]
            v_t = vt_ref[...].astype(F32).T.astype(BF16)
            for rs in chains:
                s = _dot_nt(kt_ref[...], qa_ref[rs, :])
                m_old = m_ref[:, rs]
                m_new = jnp.maximum(m_old, jnp.max(s, axis=0, keepdims=True))
                p = jnp.exp(s - m_new).astype(BF16)
                acc_ref[:, rs] = jnp.exp(m_old - m_new) * acc_ref[:, rs] + _dot(v_t, p)
                m_ref[:, rs] = m_new
            return carry

        lax.fori_loop(0, (cnt_ref[slot] + SEL_STEP - 1) // SEL_STEP, step, 0)
        acc = acc_ref[...]
        o = (acc[:NSA_DH, :] * (1.0 / acc[NSA_DH:NSA_DH + 1, :])).T
        for h in range(NSA_HPG):
            hh = g * NSA_HPG + h
            gate = gates[:, SM_NG + NSA_HEADS + hh:SM_NG + NSA_HEADS + hh + 1]
            o_ref[:, _head_cols(hh)] = (prev_ref[:, _head_cols(hh)] + gate * o[h * tq:(h + 1) * tq, :]
                                        ).astype(o_ref.dtype)


def nsa_select(proj3, sel, act, prev, tq=256):
    bsz, T, _ = proj3.shape
    tq = min(tq, T)
    nt = T // tq
    n_sel = T // NSA_SEL_BLOCK
    lmax = -(-n_sel // SEL_STEP) * SEL_STEP
    tp = min(KPREP_ROWS, T)
    kc, vc = _nsa_key_consts(T, tp, onehot=True)
    qc = _nsa_query_consts()
    kaug, vaug = nsa_kprep(proj3, kc, vc, tp, (OFF_NKV + KV_W) // KV_W)
    wk = kaug.shape[-1]
    on = (act.reshape(bsz, nt, NSA_GROUPS, n_sel) > 0.0).astype(jnp.int32)
    seen = jnp.cumsum(on, axis=-1)
    ids = jnp.sum(seen[..., None, :] <= jnp.arange(lmax, dtype=jnp.int32)[:, None], axis=-1, dtype=jnp.int32)
    ids = ids.reshape(-1)
    cnt = seen[..., -1].reshape(-1)
    rows = NSA_HPG * tq
    grid_spec = pltpu.PrefetchScalarGridSpec(
        num_scalar_prefetch=2,
        grid=(bsz, nt),
        in_specs=[pl.BlockSpec((None, tq, NSA_QW), lambda b, i, *_: (b, i, OFF_NQ // NSA_QW)),
                  pl.BlockSpec((None, NSA_GROUPS, T + tp, wk), lambda b, i, *_: (b, 0, 0, 0)),
                  pl.BlockSpec((None, NSA_GROUPS, T + tp, 2 * NSA_DH), lambda b, i, *_: (b, 0, 0, 0)),
                  pl.BlockSpec((None, tq, NSA_GROUPS * n_sel), lambda b, i, *_: (b, i, 0)),
                  pl.BlockSpec((NSA_HEADS, EXTRA_W), lambda b, i, *_: (0, 0)),
                  pl.BlockSpec((None, tq, LANES), lambda b, i, *_: (b, i, OFF_SMALL // LANES)),
                  pl.BlockSpec((None, tq, NSA_QW), lambda b, i, *_: (b, i, 0))],
        out_specs=pl.BlockSpec((None, tq, NSA_QW), lambda b, i, *_: (b, i, 0)),
        scratch_shapes=[pltpu.VMEM((rows, wk), BF16),
                        pltpu.VMEM((SEL_STEP * NSA_SEL_BLOCK, wk), BF16),
                        pltpu.VMEM((SEL_STEP * NSA_SEL_BLOCK, 2 * NSA_DH), BF16),
                        pltpu.VMEM((1, rows), F32), pltpu.VMEM((2 * NSA_DH, rows), F32)])
    return pl.pallas_call(
        functools.partial(_nsa_select_kernel, tq=tq, n_sel=n_sel, lmax=lmax),
        grid_spec=grid_spec,
        out_shape=jax.ShapeDtypeStruct((bsz, T, NSA_QW), BF16),
        compiler_params=_cparams(("parallel", "arbitrary")),
        name="nsa_select",
    )(ids, cnt, proj3, kaug, vaug, sel, qc, proj3, prev)


def nsa_mixer(proj3, cmp_pos, cmp_w1, cmp_w2):
    kcv = nsa_compress(proj3, cmp_pos, cmp_w1, cmp_w2)
    o_cmp, sel, act = nsa_cmp_topk(proj3, kcv)
    o_cw = nsa_window(proj3, o_cmp)
    return nsa_select(proj3, sel, act, o_cw)


def _pack_w_in(w_in):
    w_in = w_in.astype(BF16)
    (g_q, g_k, g_v, g_r, g_a, n_q, n_kv, n_g, s_z, s_xbc, s_dt, m_g) = jnp.split(w_in, SPLIT_POINTS, axis=-1)
    pad = jnp.zeros(w_in.shape[:-1] + (D_PK - OFF_SMALL - GLA_RANK - 3 * NSA_HEADS - SSM_HEADS,), w_in.dtype)
    return jnp.concatenate([m_g, g_q, g_k, g_v, g_r, n_q, s_z, s_xbc, n_kv, g_a, n_g, s_dt, pad], axis=-1)


def kernel(x, w_in, gla_a2, gla_a_bias, gla_norm, nsa_cmp_pos, nsa_cmp_w1, nsa_cmp_w2, ssm_conv_w, ssm_conv_b,
           ssm_dt_bias, ssm_a_log, ssm_d, ssm_norm, w_branch, w_out, norm_pre_mix, norm_post_mix, norm_pre_ffn,
           norm_post_ffn, w_ffn_gate, w_ffn_up, w_ffn_down):
    bsz, T, D = x.shape
    depth = w_in.shape[0]
    n = bsz * T
    xf = x.reshape(n, D)
    for l in range(depth):
        wa_pad = jnp.zeros((LANES, GLA_KW), F32).at[SM_GA:SM_GA + GLA_RANK].set(gla_a2[l]).astype(BF16)
        proj = norm_matmul(xf, norm_pre_mix[l], _pack_w_in(w_in[l]))
        proj3 = proj.reshape(bsz, T, D_PK)
        y_gla = gla_mixer(proj3, wa_pad, gla_a_bias[l], gla_norm[l])
        y_nsa = nsa_mixer(proj3, nsa_cmp_pos[l], nsa_cmp_w1[l], nsa_cmp_w2[l])
        y_ssm = ssd_mixer(proj3, ssm_conv_w[l], ssm_conv_b[l], ssm_dt_bias[l], ssm_a_log[l], ssm_d[l], ssm_norm[l])
        merged = merge_branches(y_gla.reshape(n, BRANCH_W), y_nsa.reshape(n, BRANCH_W), y_ssm.reshape(n, BRANCH_W),
                                w_branch[l].astype(BF16), proj)
        xf = proj_norm_residual(merged, w_out[l].astype(BF16), xf, norm_post_mix[l], tm=512, tk=D)
        act = ffn_up(xf, norm_pre_ffn[l], w_ffn_gate[l].astype(BF16), w_ffn_up[l].astype(BF16))
        xf = proj_norm_residual(act, w_ffn_down[l].astype(BF16), xf, norm_post_ffn[l], tm=1024, tk=512)
    return xf.reshape(bsz, T, D)
```

```python
import functools

import numpy as np
import jax
import jax.numpy as jnp
from jax import lax
from jax.experimental import pallas as pl
from jax.experimental.pallas import tpu as pltpu

F32 = jnp.float32
BF16 = jnp.bfloat16

D_MODEL = 2048
EPS = 1e-6
N_BRANCH = 3
BRANCH_W = 1024
GLA_HEADS, GLA_DK, GLA_DV, GLA_RANK, GLA_TAU, GLA_CHUNK = 4, 256, 256, 16, 16.0, 64
GLA_KW = GLA_HEADS * GLA_DK
GLA_VW = GLA_HEADS * GLA_DV
NSA_HEADS, NSA_GROUPS, NSA_DH = 16, 2, 64
NSA_HPG = NSA_HEADS // NSA_GROUPS
NSA_QW = NSA_HEADS * NSA_DH
NSA_KVW = NSA_GROUPS * NSA_DH
NSA_CMP_LEN, NSA_CMP_STRIDE, NSA_SEL_BLOCK, NSA_TOPN, NSA_WINDOW = 32, 16, 64, 16, 512
BIG = 1e30
SSM_HEADS, SSM_HEADDIM, SSM_GROUPS, SSM_STATE, SSM_CONV, SSM_CHUNK = 16, 64, 4, 128, 4, 64
SSM_INNER = SSM_HEADS * SSM_HEADDIM
SSM_HPG = SSM_HEADS // SSM_GROUPS
SSM_CONV_CH = SSM_INNER + 2 * SSM_GROUPS * SSM_STATE
D_FF = ((8 * D_MODEL // 3 + 255) // 256) * 256
IN_SIZES = (GLA_KW, GLA_KW, GLA_VW, GLA_VW, GLA_RANK, NSA_QW, 6 * NSA_KVW, 3 * NSA_HEADS,
            SSM_INNER, SSM_CONV_CH, SSM_HEADS, N_BRANCH * D_MODEL)
SPLIT_POINTS = tuple(int(v) for v in np.cumsum(IN_SIZES)[:-1])

LANES = 128
VMEM_LIMIT = 56 * 1024 * 1024

OFF_MG = 0
OFF_GQ = OFF_MG + N_BRANCH * D_MODEL
OFF_GK = OFF_GQ + GLA_KW
OFF_GV = OFF_GK + GLA_KW
OFF_GR = OFF_GV + GLA_VW
OFF_NQ = OFF_GR + GLA_VW
OFF_SZ = OFF_NQ + NSA_QW
OFF_SX = OFF_SZ + SSM_INNER
OFF_SB = OFF_SX + SSM_INNER
OFF_SC = OFF_SB + SSM_GROUPS * SSM_STATE
OFF_NKV = OFF_SC + SSM_GROUPS * SSM_STATE
OFF_SMALL = OFF_NKV + 6 * NSA_KVW
D_PK = OFF_SMALL + 2 * LANES
SM_GA = 0
SM_NG = 16
SM_DT = 64


def _cparams(sem):
    return pltpu.CompilerParams(dimension_semantics=sem, vmem_limit_bytes=VMEM_LIMIT)


def _split3(x):
    hi = x.astype(BF16)
    r1 = x - hi.astype(F32)
    mid = r1.astype(BF16)
    lo = (r1 - mid.astype(F32)).astype(BF16)
    return hi, mid, lo


def _dot(a, b):
    return jnp.dot(a, b, preferred_element_type=F32)


def _dot_nt(a, b):
    return lax.dot_general(a, b, (((1,), (1,)), ((), ())), preferred_element_type=F32)


def _dot01_left(m01, x):
    hi, mid, lo = _split3(x)
    return _dot(m01, hi) + _dot(m01, mid) + _dot(m01, lo)


def _dot01_right(x, m01):
    hi, mid, lo = _split3(x)
    return _dot(hi, m01) + _dot(mid, m01) + _dot(lo, m01)


def _silu(x):
    return x / (1.0 + jnp.exp(-x))


def _norm_rows(x_ref, g_ref, h_ref):
    @pl.when(pl.program_id(1) == 0)
    def _():
        x = x_ref[...]
        y = x * lax.rsqrt(jnp.mean(x * x, axis=-1, keepdims=True) + EPS)
        h_ref[...] = (y * g_ref[...]).astype(h_ref.dtype)


def _norm_mm_kernel(x_ref, g_ref, w_ref, o_ref, h_ref):
    _norm_rows(x_ref, g_ref, h_ref)
    o_ref[...] = _dot(h_ref[...], w_ref[...]).astype(o_ref.dtype)


def norm_matmul(x, g, w, out_dtype=F32, tm=1024, tn=1024):
    m, k = x.shape
    n = w.shape[1]
    tm = min(tm, m)
    tn = min(tn, n)
    return pl.pallas_call(
        _norm_mm_kernel,
        grid=(m // tm, n // tn),
        in_specs=[pl.BlockSpec((tm, k), lambda i, j: (i, 0)), pl.BlockSpec((1, k), lambda i, j: (0, 0)),
                  pl.BlockSpec((k, tn), lambda i, j: (0, j))],
        out_specs=pl.BlockSpec((tm, tn), lambda i, j: (i, j)),
        out_shape=jax.ShapeDtypeStruct((m, n), out_dtype),
        scratch_shapes=[pltpu.VMEM((tm, k), BF16)],
        compiler_params=_cparams(("parallel", "arbitrary")),
        name="in_proj",
    )(x, g.reshape(1, k), w)


def _merge_kernel(yg_ref, yn_ref, ys_ref, wg_ref, wn_ref, ws_ref, g0_ref, g1_ref, g2_ref, o_ref):
    def gate(ref):
        return 1.0 / (1.0 + jnp.exp(-ref[...]))

    acc = gate(g0_ref) * _dot(yg_ref[...], wg_ref[...])
    acc += gate(g1_ref) * _dot(yn_ref[...], wn_ref[...])
    acc += gate(g2_ref) * _dot(ys_ref[...], ws_ref[...])
    o_ref[...] = acc.astype(o_ref.dtype)


def merge_branches(y_gla, y_nsa, y_ssm, w_branch, proj, tm=1024, tn=512):
    m = y_gla.shape[0]
    d = w_branch.shape[-1]
    tm = min(tm, m)
    nj = d // tn
    ys = pl.BlockSpec((tm, BRANCH_W), lambda i, j: (i, 0))

    def wspec(b):
        return pl.BlockSpec((None, BRANCH_W, tn), lambda i, j, b=b: (b, 0, j))

    def gspec(b):
        return pl.BlockSpec((tm, tn), lambda i, j, b=b: (i, (OFF_MG + b * D_MODEL) // tn + j))

    return pl.pallas_call(
        _merge_kernel,
        grid=(m // tm, nj),
        in_specs=[ys, ys, ys, wspec(0), wspec(1), wspec(2), gspec(0), gspec(1), gspec(2)],
        out_specs=pl.BlockSpec((tm, tn), lambda i, j: (i, j)),
        out_shape=jax.ShapeDtypeStruct((m, d), BF16),
        compiler_params=_cparams(("parallel", "parallel")),
        name="merge",
    )(y_gla, y_nsa, y_ssm, w_branch, w_branch, w_branch, proj, proj, proj)


def _proj_norm_res_kernel(a_ref, w_ref, x_ref, g_ref, o_ref, acc_ref):
    k = pl.program_id(1)

    @pl.when(k == 0)
    def _():
        acc_ref[...] = jnp.zeros_like(acc_ref)

    acc_ref[...] += _dot(a_ref[...], w_ref[...])

    @pl.when(k == pl.num_programs(1) - 1)
    def _():
        f = acc_ref[...]
        y = f * lax.rsqrt(jnp.mean(f * f, axis=-1, keepdims=True) + EPS)
        o_ref[...] = x_ref[...] + y * g_ref[...]


def proj_norm_residual(a, w, x, g, tm=1024, tk=512):
    m, kk = a.shape
    d = w.shape[1]
    tm = min(tm, m)
    return pl.pallas_call(
        _proj_norm_res_kernel,
        grid=(m // tm, kk // tk),
        in_specs=[pl.BlockSpec((tm, tk), lambda i, k: (i, k)),
                  pl.BlockSpec((tk, d), lambda i, k: (k, 0)),
                  pl.BlockSpec((tm, d), lambda i, k: (i, 0)),
                  pl.BlockSpec((1, d), lambda i, k: (0, 0))],
        out_specs=pl.BlockSpec((tm, d), lambda i, k: (i, 0)),
        out_shape=jax.ShapeDtypeStruct((m, d), F32),
        scratch_shapes=[pltpu.VMEM((tm, d), F32)],
        compiler_params=_cparams(("parallel", "arbitrary")),
        name="proj_norm_res",
    )(a, w, x, g.reshape(1, d))


def _ffn_up_kernel(x_ref, g_ref, wg_ref, wu_ref, o_ref, h_ref):
    _norm_rows(x_ref, g_ref, h_ref)
    h = h_ref[...]
    a = _dot(h, wg_ref[...])
    u = _dot(h, wu_ref[...])
    o_ref[...] = (_silu(a) * u).astype(o_ref.dtype)


def ffn_up(x, g, wg, wu, tm=1024, tn=512):
    m, k = x.shape
    n = wg.shape[1]
    tm = min(tm, m)
    return pl.pallas_call(
        _ffn_up_kernel,
        grid=(m // tm, n // tn),
        in_specs=[pl.BlockSpec((tm, k), lambda i, j: (i, 0)),
                  pl.BlockSpec((1, k), lambda i, j: (0, 0)),
                  pl.BlockSpec((k, tn), lambda i, j: (0, j)),
                  pl.BlockSpec((k, tn), lambda i, j: (0, j))],
        out_specs=pl.BlockSpec((tm, tn), lambda i, j: (i, j)),
        out_shape=jax.ShapeDtypeStruct((m, n), BF16),
        scratch_shapes=[pltpu.VMEM((tm, k), BF16)],
        compiler_params=_cparams(("parallel", "arbitrary")),
        name="ffn_up",
    )(x, g.reshape(1, k), wg, wu)


def _gla_kernel(q_ref, k_ref, v_ref, r_ref, sm_ref, wa_ref, ba_ref, ng_ref, tri_ref, same_ref, o_ref,
                st_ref, *, tb):
    C = GLA_CHUNK

    @pl.when(pl.program_id(1) == 0)
    def _():
        st_ref[...] = jnp.zeros_like(st_ref)

    pre = _dot(sm_ref[...].astype(BF16), wa_ref[...]) + ba_ref[...]
    la = (jnp.minimum(pre, 0.0) - jnp.log1p(jnp.exp(-jnp.abs(pre)))) * (1.0 / GLA_TAU)
    tri = tri_ref[...]
    same = same_ref[...]
    ri = lax.broadcasted_iota(jnp.int32, (tb, tb), 0)
    ci = lax.broadcasted_iota(jnp.int32, (tb, tb), 1)
    causal = (ri >= ci) & (ri // C == ci // C)
    for h in range(GLA_HEADS):
        ck = slice(h * GLA_DK, (h + 1) * GLA_DK)
        cv = slice(h * GLA_DV, (h + 1) * GLA_DV)
        hi, mid, lo = _split3(la[:, ck])
        bcum = _dot(tri, hi) + _dot(tri, mid) + _dot(tri, lo)
        b_last = _dot(same, hi) + _dot(same, mid) + _dot(same, lo)
        q = q_ref[:, ck] * (GLA_DK ** -0.5)
        k = k_ref[:, ck]
        v = v_ref[:, cv]
        q_dec = (q * jnp.exp(bcum)).astype(BF16)
        k_inv = (k * jnp.exp(-bcum)).astype(BF16)
        k_end = (k * jnp.exp(b_last - bcum)).astype(BF16)
        att = jnp.where(causal, _dot_nt(q_dec, k_inv), 0.0)
        o_intra = _dot(att.astype(BF16), v.astype(BF16))
        st = st_ref[h]
        outs = []
        for c in range(tb // C):
            rows = slice(c * C, (c + 1) * C)
            outs.append(o_intra[rows] + _dot_nt(q_dec[rows], st.astype(BF16)))
            st = jnp.exp(b_last[c * C:c * C + 1, :]) * st + _dot(v[rows].T.astype(BF16), k_end[rows])
        st_ref[h] = st
        o = jnp.concatenate(outs, axis=0)
        y = o * lax.rsqrt(jnp.mean(o * o, axis=-1, keepdims=True) + EPS) * ng_ref[...]
        o_ref[:, cv] = (y * _silu(r_ref[:, cv])).astype(o_ref.dtype)


def _chunk_masks(tb, chunk):
    idx = np.arange(tb)
    same = (idx[:, None] // chunk) == (idx[None, :] // chunk)
    return jnp.asarray(same & (idx[:, None] >= idx[None, :]), BF16), jnp.asarray(same, BF16)


def gla_mixer(proj3, wa_pad, ba, norm_g, tb=256):
    bsz, T, _ = proj3.shape
    tb = min(tb, T)
    tri, same = _chunk_masks(tb, GLA_CHUNK)

    def col(off, w):
        return pl.BlockSpec((None, tb, w), lambda b, t, o=off // w: (b, t, o))

    def full(shape):
        return pl.BlockSpec(shape, lambda b, t: (0,) * len(shape))

    return pl.pallas_call(
        functools.partial(_gla_kernel, tb=tb),
        grid=(bsz, T // tb),
        in_specs=[col(OFF_GQ, GLA_KW), col(OFF_GK, GLA_KW), col(OFF_GV, GLA_VW), col(OFF_GR, GLA_VW),
                  col(OFF_SMALL, LANES),
                  full((LANES, GLA_KW)), full((1, GLA_KW)), full((1, GLA_DV)), full((tb, tb)), full((tb, tb))],
        out_specs=pl.BlockSpec((None, tb, GLA_VW), lambda b, t: (b, t, 0)),
        out_shape=jax.ShapeDtypeStruct((bsz, T, GLA_VW), BF16),
        scratch_shapes=[pltpu.VMEM((GLA_HEADS, GLA_DV, GLA_DK), F32)],
        compiler_params=_cparams(("parallel", "arbitrary")),
        name="gla",
    )(proj3, proj3, proj3, proj3, proj3, wa_pad, ba.reshape(1, GLA_KW), norm_g.reshape(1, GLA_DV), tri, same)


def _ssd_kernel(z_ref, x_ref, bm_ref, cm_ref, sm_ref, cwx_ref, cwb_ref, cwc_ref, cbx_ref, cbb_ref, cbc_ref,
                dtb_ref, alog_ref, dskip_ref, ng_ref, tri_ref, exp_ref, o_ref,
                st_ref, extx_ref, extb_ref, extc_ref, xa_ref, ba_ref, ca_ref, dt_ref, a_ref, *, tb, nchunk):
    L = SSM_CHUNK
    GW = SSM_HPG * SSM_HEADDIM
    NS = SSM_STATE
    first = pl.program_id(1) == 0

    @pl.when(first)
    def _():
        st_ref[...] = jnp.zeros_like(st_ref)

    def conv_silu(src_ref, ext_ref, w_ref, b_ref, dst_ref):
        @pl.when(first)
        def _():
            ext_ref[0:8, :] = jnp.zeros((8, ext_ref.shape[1]), F32)

        @pl.when(jnp.logical_not(first))
        def _():
            ext_ref[0:8, :] = ext_ref[tb:tb + 8, :]

        ext_ref[8:8 + tb, :] = src_ref[...]
        acc = b_ref[...] + w_ref[SSM_CONV - 1:SSM_CONV, :] * ext_ref[8:8 + tb, :]
        for j in range(1, SSM_CONV):
            acc = acc + w_ref[SSM_CONV - 1 - j:SSM_CONV - j, :] * ext_ref[8 - j:8 - j + tb, :]
        dst_ref[...] = _silu(acc)

    conv_silu(x_ref, extx_ref, cwx_ref, cbx_ref, xa_ref)
    conv_silu(bm_ref, extb_ref, cwb_ref, cbb_ref, ba_ref)
    conv_silu(cm_ref, extc_ref, cwc_ref, cbc_ref, ca_ref)

    v = sm_ref[...] + dtb_ref[...]
    dt = jnp.maximum(v, 0.0) + jnp.log1p(jnp.exp(-jnp.abs(v)))
    dt_ref[...] = dt
    a_ref[...] = dt * (-jnp.exp(alog_ref[...]))

    tri = tri_ref[...]
    row = lax.broadcasted_iota(jnp.int32, (L, GW), 0)
    lane = lax.broadcasted_iota(jnp.int32, (L, GW), 1)
    lane_in = jnp.bitwise_and(lane, SSM_HEADDIM - 1)
    eye_t = (lane_in == row).astype(F32)
    tril_t = lane_in <= row
    bd_mask = (lax.broadcasted_iota(jnp.int32, (GW, GW), 0) // L
               == lax.broadcasted_iota(jnp.int32, (GW, GW), 1) // SSM_HEADDIM).astype(F32).astype(BF16)

    states = [st_ref[g] for g in range(SSM_GROUPS)]
    cum_parts = _split3(_dot01_left(tri, a_ref[...]))
    dt_parts = _split3(dt_ref[...])
    cum_all, dt_all = [], []
    for g in range(SSM_GROUPS):
        e_g = exp_ref[:, g * GW:(g + 1) * GW]
        cum_all.append(sum(_dot(p, e_g) for p in cum_parts))
        dt_all.append(sum(_dot(p, e_g) for p in dt_parts))
    for c in range(nchunk):
        rows = slice(c * L, (c + 1) * L)
        for g in range(SSM_GROUPS):
            cs = slice(g * GW, (g + 1) * GW)
            cum_e = cum_all[g][rows]
            dt_e = dt_all[g][rows]
            cum_last = cum_e[L - 1:L, :]
            r_row = jnp.sum(cum_e * eye_t, axis=0, keepdims=True)
            decay = jnp.exp(jnp.where(tril_t, cum_e - r_row, -jnp.inf))
            x_g = xa_ref[rows, cs]
            xdt = x_g * dt_e
            b_g = ba_ref[rows, g * NS:(g + 1) * NS]
            c_g = ca_ref[rows, g * NS:(g + 1) * NS].astype(BF16)
            bb = b_g.astype(BF16)
            cb_t = _dot_nt(c_g, jnp.concatenate([bb] * SSM_HPG, axis=0))
            xdt_bd = jnp.concatenate([xdt.astype(BF16)] * SSM_HPG, axis=0) * bd_mask
            y = _dot((cb_t * decay).astype(BF16), xdt_bd)
            st = states[g]
            y = y + _dot(c_g, st.astype(BF16)) * jnp.exp(cum_e)
            dend = jnp.exp(cum_last - cum_e)
            states[g] = jnp.exp(cum_last) * st + _dot(b_g.T.astype(BF16), (dend * xdt).astype(BF16))
            y = y + x_g * dskip_ref[:, cs]
            y = y * _silu(z_ref[rows, cs])
            y = y * lax.rsqrt(jnp.mean(y * y, axis=-1, keepdims=True) + EPS) * ng_ref[:, cs]
            o_ref[rows, cs] = y.astype(o_ref.dtype)
    for g in range(SSM_GROUPS):
        st_ref[g] = states[g]


def _ssd_expand():
    e = np.zeros((LANES, SSM_INNER), np.float32)
    for h in range(SSM_HEADS):
        e[SM_DT + h, h * SSM_HEADDIM:(h + 1) * SSM_HEADDIM] = 1.0
    return jnp.asarray(e, BF16)


def _small_row(v, off):
    return jnp.zeros((1, LANES), F32).at[0, off:off + v.shape[0]].set(v.astype(F32))


def ssd_mixer(proj3, conv_w, conv_b, dt_bias, a_log, d_skip, norm_g, tb=256):
    bsz, T, _ = proj3.shape
    tb = min(tb, T)
    nchunk = tb // SSM_CHUNK
    GN = SSM_GROUPS * SSM_STATE
    expand = _ssd_expand()
    tri, _ = _chunk_masks(tb, SSM_CHUNK)
    cwx, cwb, cwc = conv_w[:, :SSM_INNER], conv_w[:, SSM_INNER:SSM_INNER + GN], conv_w[:, SSM_INNER + GN:]
    cb2 = conv_b.reshape(1, SSM_CONV_CH)
    cbx, cbb, cbc = cb2[:, :SSM_INNER], cb2[:, SSM_INNER:SSM_INNER + GN], cb2[:, SSM_INNER + GN:]
    dtb = _small_row(dt_bias, SM_DT)
    alog = _small_row(a_log, SM_DT)
    dskip = jnp.repeat(d_skip.astype(F32), SSM_HEADDIM).reshape(1, SSM_INNER)

    def col(off, w):
        return pl.BlockSpec((None, tb, w), lambda b, t, o=off // w: (b, t, o))

    def full(shape):
        return pl.BlockSpec(shape, lambda b, t: (0,) * len(shape))

    return pl.pallas_call(
        functools.partial(_ssd_kernel, tb=tb, nchunk=nchunk),
        grid=(bsz, T // tb),
        in_specs=[col(OFF_SZ, SSM_INNER), col(OFF_SX, SSM_INNER), col(OFF_SB, GN), col(OFF_SC, GN),
                  col(OFF_SMALL, LANES),
                  full((SSM_CONV, SSM_INNER)), full((SSM_CONV, GN)), full((SSM_CONV, GN)),
                  full((1, SSM_INNER)), full((1, GN)), full((1, GN)),
                  full((1, LANES)), full((1, LANES)), full((1, SSM_INNER)), full((1, SSM_INNER)),
                  full((tb, tb)), full((LANES, SSM_INNER))],
        out_specs=pl.BlockSpec((None, tb, SSM_INNER), lambda b, t: (b, t, 0)),
        out_shape=jax.ShapeDtypeStruct((bsz, T, SSM_INNER), BF16),
        scratch_shapes=[pltpu.VMEM((SSM_GROUPS, SSM_STATE, SSM_HPG * SSM_HEADDIM), F32),
                        pltpu.VMEM((tb + 8, SSM_INNER), F32), pltpu.VMEM((tb + 8, GN), F32),
                        pltpu.VMEM((tb + 8, GN), F32),
                        pltpu.VMEM((tb, SSM_INNER), F32), pltpu.VMEM((tb, GN), F32), pltpu.VMEM((tb, GN), F32),
                        pltpu.VMEM((tb, LANES), F32), pltpu.VMEM((tb, LANES), F32)],
        compiler_params=_cparams(("parallel", "arbitrary")),
        name="ssd",
    )(proj3, proj3, proj3, proj3, proj3, cwx, cwb, cwc, cbx, cbb, cbc, dtb, alog, dskip,
      norm_g.reshape(1, SSM_INNER), tri, expand)


NSA_SLOPES = tuple(float(np.float32(2.0 ** (-8.0 * (i + 1) / NSA_HEADS))) for i in range(NSA_HEADS))
NSA_SCALE = NSA_DH ** -0.5
KV_W = 2 * NSA_KVW
NEG_HUGE = -3.0e38


def _head_cols(hh):
    return slice(hh * NSA_DH, (hh + 1) * NSA_DH)


def _nsa_compress_kernel(k_ref, v_ref, pos_ref, bd1_ref, bd2_ref, o_ref, *, n16):
    S = NSA_CMP_STRIDE
    top = jnp.zeros((n16, KV_W), F32)
    bot = jnp.zeros((n16, KV_W), F32)
    for l in range(S):
        rows = pl.ds(l, n16, stride=S)
        x = jnp.concatenate([k_ref[rows, :], v_ref[rows, :]], axis=-1)
        top += _dot((x + pos_ref[l:l + 1, :]).astype(BF16), bd1_ref[l])
        bot += _dot((x + pos_ref[S + l:S + l + 1, :]).astype(BF16), bd1_ref[S + l])
    pre = top + pltpu.roll(bot, n16 - 1, axis=0)
    out = _dot(_silu(pre).astype(BF16), bd2_ref[...])
    row = lax.broadcasted_iota(jnp.int32, (n16, KV_W), 0)
    o_ref[...] = jnp.where(row < n16 - 1, out, 0.0)


def nsa_compress(proj3, cmp_pos, cmp_w1, cmp_w2):
    bsz, T, _ = proj3.shape
    n16 = T // NSA_CMP_STRIDE
    sel = np.array([0, 0, 1, 1])
    eye = jnp.eye(4, dtype=F32)
    w1r = cmp_w1.reshape(2, NSA_CMP_LEN, NSA_DH, NSA_DH)[sel]
    bd1 = jnp.einsum('ab,alde->ladbe', eye, w1r).reshape(NSA_CMP_LEN, KV_W, KV_W).astype(BF16)
    bd2 = jnp.einsum('ab,ade->adbe', eye, cmp_w2[sel]).reshape(KV_W, KV_W).astype(BF16)
    pos = jnp.concatenate([cmp_pos[0], cmp_pos[0], cmp_pos[1], cmp_pos[1]], axis=-1)
    return pl.pallas_call(
        functools.partial(_nsa_compress_kernel, n16=n16),
        grid=(bsz,),
        in_specs=[pl.BlockSpec((None, T, NSA_KVW), lambda b: (b, 0, OFF_NKV // NSA_KVW)),
                  pl.BlockSpec((None, T, NSA_KVW), lambda b: (b, 0, OFF_NKV // NSA_KVW + 1)),
                  pl.BlockSpec((NSA_CMP_LEN, KV_W), lambda b: (0, 0)),
                  pl.BlockSpec((NSA_CMP_LEN, KV_W, KV_W), lambda b: (0, 0, 0)),
                  pl.BlockSpec((KV_W, KV_W), lambda b: (0, 0))],
        out_specs=pl.BlockSpec((None, n16, KV_W), lambda b: (b, 0, 0)),
        out_shape=jax.ShapeDtypeStruct((bsz, n16, KV_W), F32),
        compiler_params=_cparams(("parallel",)),
        name="nsa_compress",
    )(proj3, proj3, pos, bd1, bd2)


CMP_SEG_MAX = 320


def _cmp_segments(nc):
    bad = 2 * LANES
    if nc <= CMP_SEG_MAX + NSA_DH and nc != bad:
        return [(0, nc)]
    first = CMP_SEG_MAX if nc > CMP_SEG_MAX else nc - NSA_DH
    return [(0, first), (first, nc)]


def _stack_queries(q_ref, qc_ref, qa_ref, g, tq, shared=None):
    for h in range(NSA_HPG):
        hh = g * NSA_HPG + h
        qh = (q_ref[:, _head_cols(hh)] * NSA_SCALE).astype(BF16)
        qx = jnp.broadcast_to(qc_ref[hh:hh + 1, :], (tq, EXTRA_W))
        qa_ref[h * tq:(h + 1) * tq, 0:NSA_DH + EXTRA_W] = jnp.concatenate([qh, qx], axis=-1)
        if shared is not None:
            qa_ref[h * tq:(h + 1) * tq, NSA_DH + EXTRA_W:] = shared


def _nsa_cmp_topk_kernel(q_ref, kcv_ref, cpos_ref, qc_ref, sm_ref, ov_ref, o_ref, sel_ref, act_ref, qa_ref, imp_ref,
                         *, tq, n16, n_sel, n_top, cw):
    t0 = pl.program_id(1) * tq
    tpos = t0 + lax.broadcasted_iota(jnp.int32, (1, tq), 1)
    any_valid = (tpos >= NSA_CMP_LEN - 1).astype(F32)
    gates = 1.0 / (1.0 + jnp.exp(-sm_ref[...]))
    blk = lax.broadcasted_iota(jnp.int32, (n_sel, 1), 0)
    blk_f = blk.astype(F32)
    cur = tpos // NSA_SEL_BLOCK
    forced = (blk == 0) | (blk == cur) | (blk == cur - 1)
    future = blk * NSA_SEL_BLOCK > tpos
    blk_row = lax.broadcasted_iota(jnp.int32, (1, n_sel), 1)

    def attend(nc):
        segs = _cmp_segments(nc)
        bias = []
        for a, b in segs:
            cmp_end = (a + lax.broadcasted_iota(jnp.int32, (b - a, 1), 0)) * NSA_CMP_STRIDE + (NSA_CMP_LEN - 1)
            bias.append(jnp.where(cmp_end <= tpos, 0.0, -BIG))
        for g in range(NSA_GROUPS):
            kc = [jnp.concatenate([kcv_ref[a:b, g * NSA_DH:(g + 1) * NSA_DH].astype(BF16), cpos_ref[a:b, :]], axis=-1)
                  for a, b in segs]
            vo = [jnp.concatenate([kcv_ref[a:b, NSA_KVW + g * NSA_DH:NSA_KVW + (g + 1) * NSA_DH].T.astype(BF16),
                                   ov_ref[:, a:b]], axis=0) for a, b in segs]
            _stack_queries(q_ref, qc_ref, qa_ref, g, tq)
            imp = None
            for h in range(NSA_HPG):
                hh = g * NSA_HPG + h
                qa = qa_ref[h * tq:(h + 1) * tq, :]
                s = [_dot_nt(kc[i], qa) + bias[i] for i in range(len(segs))]
                m = functools.reduce(jnp.maximum, [jnp.max(x, axis=0, keepdims=True) for x in s])
                e = [jnp.exp(x - m) for x in s]
                inv = any_valid / sum(jnp.sum(x, axis=0, keepdims=True) for x in e)
                r = sum(_dot(vo[i], e[i].astype(BF16)) for i in range(len(segs))) * inv
                imp = r[NSA_DH:] if imp is None else imp + r[NSA_DH:]
                o_ref[:, _head_cols(hh)] = gates[:, SM_NG + hh:SM_NG + hh + 1] * r[:NSA_DH].T
            imp_ref[g] = imp

    need = (t0 + tq) // NSA_CMP_STRIDE
    nchunks = n16 // cw
    for k in range(1, nchunks + 1):
        lo = (k - 1) * cw
        cond = (need > lo) & (need <= k * cw) if k < nchunks else need > lo
        pl.when(cond)(functools.partial(attend, k * cw))

    for g in range(NSA_GROUPS):
        work = jnp.where(forced, BIG, jnp.where(future, -BIG, imp_ref[g]))
        sel_t = jnp.zeros((n_sel, tq), F32)
        for _ in range(n_top):
            top = jnp.max(work, axis=0, keepdims=True)
            idx = jnp.min(jnp.where(work == top, blk_f, float(n_sel)), axis=0, keepdims=True)
            pick = blk_f == idx
            sel_t = jnp.where(pick, 1.0, sel_t)
            work = jnp.where(pick, NEG_HUGE, work)
        sel = sel_t.T
        sel_ref[:, g * n_sel:(g + 1) * n_sel] = sel.astype(sel_ref.dtype)
        union = jnp.max(sel, axis=0, keepdims=True)
        act_ref[:, g * n_sel:(g + 1) * n_sel] = jnp.where(blk_row * NSA_SEL_BLOCK < t0, union, 0.0)


def _nsa_overlap(n16, n_sel):
    n_cmp = n16 - 1
    tok = (np.arange(n_cmp) * NSA_CMP_STRIDE)[:, None] + np.arange(NSA_CMP_LEN)[None, :]
    ov = np.zeros((n16, n_sel), np.float32)
    np.add.at(ov, (np.repeat(np.arange(n_cmp), NSA_CMP_LEN), (tok // NSA_SEL_BLOCK).ravel()), 1.0 / NSA_CMP_LEN)
    return jnp.asarray(ov.T, BF16)


def nsa_cmp_topk(proj3, kcv, tq=256):
    bsz, T, _ = proj3.shape
    tq = min(tq, T)
    n16 = T // NSA_CMP_STRIDE
    n_sel = T // NSA_SEL_BLOCK
    n_top = min(NSA_TOPN, n_sel)
    cmp_end = np.arange(n16) * NSA_CMP_STRIDE + NSA_CMP_LEN - 1
    cpos = jnp.asarray(_pos_cols(cmp_end, np.ones(n16, bool)), BF16)
    return pl.pallas_call(
        functools.partial(_nsa_cmp_topk_kernel, tq=tq, n16=n16, n_sel=n_sel, n_top=n_top, cw=min(LANES // 2, n16)),
        grid=(bsz, T // tq),
        in_specs=[pl.BlockSpec((None, tq, NSA_QW), lambda b, i: (b, i, OFF_NQ // NSA_QW)),
                  pl.BlockSpec((None, n16, KV_W), lambda b, i: (b, 0, 0)),
                  pl.BlockSpec((n16, EXTRA_W), lambda b, i: (0, 0)),
                  pl.BlockSpec((NSA_HEADS, EXTRA_W), lambda b, i: (0, 0)),
                  pl.BlockSpec((None, tq, LANES), lambda b, i: (b, i, OFF_SMALL // LANES)),
                  pl.BlockSpec((n_sel, n16), lambda b, i: (0, 0))],
        out_specs=[pl.BlockSpec((None, tq, NSA_QW), lambda b, i: (b, i, 0)),
                   pl.BlockSpec((None, tq, NSA_GROUPS * n_sel), lambda b, i: (b, i, 0)),
                   pl.BlockSpec((None, None, 1, NSA_GROUPS * n_sel), lambda b, i: (b, i, 0, 0))],
        out_shape=[jax.ShapeDtypeStruct((bsz, T, NSA_QW), F32),
                   jax.ShapeDtypeStruct((bsz, T, NSA_GROUPS * n_sel), BF16),
                   jax.ShapeDtypeStruct((bsz, T // tq, 1, NSA_GROUPS * n_sel), F32)],
        scratch_shapes=[pltpu.VMEM((NSA_HPG * tq, NSA_DH + EXTRA_W), BF16),
                        pltpu.VMEM((NSA_GROUPS, n_sel, tq), F32)],
        compiler_params=_cparams(("parallel", "parallel")),
        name="nsa_cmp_topk",
    )(proj3, kcv, cpos, _nsa_query_consts(), proj3, _nsa_overlap(n16, n_sel))


def _nsa_window_kernel(q_ref, k0_ref, k1_ref, k2_ref, v0_ref, v1_ref, v2_ref, qc_ref, sm_ref, prev_ref, o_ref,
                       qa_ref, *, tq):
    nb = NSA_WINDOW // tq + 1
    i = pl.program_id(1)
    tpos = i * tq + lax.broadcasted_iota(jnp.int32, (1, tq), 1)
    kpos = (i - (nb - 1)) * tq + lax.broadcasted_iota(jnp.int32, (nb * tq, 1), 0)
    d = tpos - kpos
    valid = (d >= 0) & (d < NSA_WINDOW) & (kpos >= 0)
    hpc = NSA_HPG // WIN_CHAINS
    bias = jnp.concatenate([jnp.where(valid, 0.0, -BIG)] * hpc, axis=1)
    gates = 1.0 / (1.0 + jnp.exp(-sm_ref[...]))
    for g in range(NSA_GROUPS):
        kw = jnp.concatenate([k0_ref[g], k1_ref[g], k2_ref[g]], axis=0)
        vw = jnp.concatenate([v0_ref[g], v1_ref[g], v2_ref[g]], axis=0)
        vw_t = vw.astype(F32).T.astype(BF16)
        _stack_queries(q_ref, qc_ref, qa_ref, g, tq)
        for c in range(WIN_CHAINS):
            s = _dot_nt(kw, qa_ref[c * hpc * tq:(c + 1) * hpc * tq, :]) + bias
            e = jnp.exp(s - jnp.max(s, axis=0, keepdims=True))
            acc = _dot(vw_t, e.astype(BF16))
            o = (acc[:NSA_DH, :] * (1.0 / acc[NSA_DH:NSA_DH + 1, :])).T
            for h in range(hpc):
                hh = g * NSA_HPG + c * hpc + h
                gate = gates[:, SM_NG + 2 * NSA_HEADS + hh:SM_NG + 2 * NSA_HEADS + hh + 1]
                o_ref[:, _head_cols(hh)] = prev_ref[:, _head_cols(hh)] + gate * o[h * tq:(h + 1) * tq]


def nsa_window(proj3, prev, tq=256):
    bsz, T, _ = proj3.shape
    tq = min(tq, T)
    assert NSA_WINDOW % tq == 0 and NSA_WINDOW // tq == 2
    tp = min(KPREP_ROWS, T)
    kc, vc = _nsa_key_consts(T, tp, onehot=False)
    kaug, vaug = nsa_kprep(proj3, kc, vc, tp, (OFF_NKV + 2 * KV_W) // KV_W)
    wk = kaug.shape[-1]

    def kvspec(back, w):
        return pl.BlockSpec((None, NSA_GROUPS, tq, w), lambda b, i, back=back: (b, 0, jnp.maximum(i - back, 0), 0))

    return pl.pallas_call(
        functools.partial(_nsa_window_kernel, tq=tq),
        grid=(bsz, T // tq),
        in_specs=[pl.BlockSpec((None, tq, NSA_QW), lambda b, i: (b, i, OFF_NQ // NSA_QW)),
                  kvspec(2, wk), kvspec(1, wk), kvspec(0, wk),
                  kvspec(2, 2 * NSA_DH), kvspec(1, 2 * NSA_DH), kvspec(0, 2 * NSA_DH),
                  pl.BlockSpec((NSA_HEADS, EXTRA_W), lambda b, i: (0, 0)),
                  pl.BlockSpec((None, tq, LANES), lambda b, i: (b, i, OFF_SMALL // LANES)),
                  pl.BlockSpec((None, tq, NSA_QW), lambda b, i: (b, i, 0))],
        out_specs=pl.BlockSpec((None, tq, NSA_QW), lambda b, i: (b, i, 0)),
        out_shape=jax.ShapeDtypeStruct((bsz, T, NSA_QW), F32),
        scratch_shapes=[pltpu.VMEM((NSA_HPG * tq, wk), BF16)],
        compiler_params=_cparams(("parallel", "parallel")),
        name="nsa_window",
    )(proj3, kaug, kaug, kaug, vaug, vaug, vaug, _nsa_query_consts(), proj3, prev)


MASK_BIG = 1e30
SEL_STEP = 8
KPREP_ROWS = 1024
NSA_CHAINS = 1
WIN_CHAINS = 2
POS_HI = 128
EXTRA_W = NSA_DH


def _bf16_pieces(x):
    x = np.float32(x)
    out = []
    for _ in range(3):
        p = np.float32(np.asarray(x, dtype=jnp.bfloat16))
        out.append(p)
        x = np.float32(x - p)
    return out


def _pos_cols(pos, real):
    c = np.zeros((len(pos), EXTRA_W), np.float32)
    for j in range(3):
        c[real, j] = (pos[real] // POS_HI) * POS_HI
        c[real, 3 + j] = pos[real] % POS_HI
    c[~real, 6] = 1.0
    return c


def _nsa_query_consts():
    qc = np.zeros((NSA_HEADS, EXTRA_W), np.float32)
    for hh in range(NSA_HEADS):
        qc[hh, 0:3] = qc[hh, 3:6] = _bf16_pieces(NSA_SLOPES[hh])
        qc[hh, 6] = -MASK_BIG
    return jnp.asarray(qc, BF16)


def _nsa_key_consts(T, pad, onehot):
    n_sel = T // NSA_SEL_BLOCK
    pos = np.arange(T + pad)
    real = pos < T
    kc = _pos_cols(pos, real)
    if onehot:
        oh = np.zeros((T + pad, n_sel), np.float32)
        oh[pos[real], pos[real] // NSA_SEL_BLOCK] = 1.0
        kc = np.concatenate([kc, oh], axis=1)
    vc = np.zeros((T + pad, NSA_DH), np.float32)
    vc[real, 0] = 1.0
    return jnp.asarray(kc, BF16), jnp.asarray(vc, BF16)


def _nsa_kprep_kernel(kv_ref, kc_ref, vc_ref, ka_ref, va_ref):
    real = pl.program_id(1) < pl.num_programs(1) - 1
    kv = (kv_ref[...] * jnp.where(real, 1.0, 0.0)).astype(BF16)
    for g in range(NSA_GROUPS):
        ka_ref[g] = jnp.concatenate([kv[:, g * NSA_DH:(g + 1) * NSA_DH], kc_ref[...]], axis=-1)
        va_ref[g] = jnp.concatenate([kv[:, NSA_KVW + g * NSA_DH:NSA_KVW + (g + 1) * NSA_DH], vc_ref[...]], axis=-1)


def nsa_kprep(proj3, kc, vc, tb, cs):
    bsz, T, _ = proj3.shape
    nt = T // tb
    wk = NSA_DH + kc.shape[1]
    return pl.pallas_call(
        _nsa_kprep_kernel,
        grid=(bsz, nt + 1),
        in_specs=[pl.BlockSpec((None, tb, KV_W), lambda b, t: (b, jnp.minimum(t, nt - 1), cs)),
                  pl.BlockSpec((tb, kc.shape[1]), lambda b, t: (t, 0)),
                  pl.BlockSpec((tb, NSA_DH), lambda b, t: (t, 0))],
        out_specs=[pl.BlockSpec((None, NSA_GROUPS, tb, wk), lambda b, t: (b, 0, t, 0)),
                   pl.BlockSpec((None, NSA_GROUPS, tb, 2 * NSA_DH), lambda b, t: (b, 0, t, 0))],
        out_shape=[jax.ShapeDtypeStruct((bsz, NSA_GROUPS, T + tb, wk), BF16),
                   jax.ShapeDtypeStruct((bsz, NSA_GROUPS, T + tb, 2 * NSA_DH), BF16)],
        compiler_params=_cparams(("parallel", "parallel")),
        name="nsa_kprep",
    )(proj3, kc, vc)


def _nsa_select_kernel(ids_ref, cnt_ref, q_ref, ka_ref, va_ref, sel_ref, qc_ref, sm_ref, prev_ref, o_ref,
                       qa_ref, kt_ref, vt_ref, m_ref, acc_ref, *, tq, n_sel, lmax):
    b = pl.program_id(0)
    i = pl.program_id(1)
    B = NSA_SEL_BLOCK
    kpos = i * tq + lax.broadcasted_iota(jnp.int32, (tq, 1), 0)
    tpos = i * tq + lax.broadcasted_iota(jnp.int32, (1, tq), 1)
    causal_bias = jnp.where(kpos <= tpos, 0.0, -BIG)
    hpc = NSA_HPG // NSA_CHAINS
    chains = [slice(c * hpc * tq, (c + 1) * hpc * tq) for c in range(NSA_CHAINS)]
    causal_bias = jnp.concatenate([causal_bias] * hpc, axis=1)
    gates = 1.0 / (1.0 + jnp.exp(-sm_ref[...]))
    own = pl.ds(pl.multiple_of(i * tq, tq), tq)
    for g in range(NSA_GROUPS):
        mcols = ((sel_ref[:, g * n_sel:(g + 1) * n_sel].astype(F32) - 1.0) * MASK_BIG).astype(BF16)
        _stack_queries(q_ref, qc_ref, qa_ref, g, tq, shared=mcols)
        v_own = va_ref[g, own, :].astype(F32).T.astype(BF16)
        for rs in chains:
            s = _dot_nt(ka_ref[g, own, :], qa_ref[rs, :]) + causal_bias
            m0 = jnp.max(s, axis=0, keepdims=True)
            m_ref[:, rs] = m0
            acc_ref[:, rs] = _dot(v_own, jnp.exp(s - m0).astype(BF16))
        slot = (b * pl.num_programs(1) + i) * NSA_GROUPS + g

        def step(st, carry):
            for u in range(SEL_STEP):
                r0 = pl.multiple_of(ids_ref[slot * lmax + st * SEL_STEP + u] * B, B)
                kt_ref[u * B:(u + 1) * B, :] = ka_ref[g, pl.ds(r0, B), :]
                vt_ref[u * B:(u + 1) * B, :] = va_ref[g, pl.ds(r0, B), :]
            v_t = vt_ref[...].astype(F32).T.astype(BF16)
            for rs in chains:
                s = _dot_nt(kt_ref[...], qa_ref[rs, :])
                m_old = m_ref[:, rs]
                m_new = jnp.maximum(m_old, jnp.max(s, axis=0, keepdims=True))
                p = jnp.exp(s - m_new).astype(BF16)
                acc_ref[:, rs] = jnp.exp(m_old - m_new) * acc_ref[:, rs] + _dot(v_t, p)
                m_ref[:, rs] = m_new
            return carry

        lax.fori_loop(0, (cnt_ref[slot] + SEL_STEP - 1) // SEL_STEP, step, 0)
        acc = acc_ref[...]
        o = (acc[:NSA_DH, :] * (1.0 / acc[NSA_DH:NSA_DH + 1, :])).T
        for h in range(NSA_HPG):
            hh = g * NSA_HPG + h
            gate = gates[:, SM_NG + NSA_HEADS + hh:SM_NG + NSA_HEADS + hh + 1]
            o_ref[:, _head_cols(hh)] = (prev_ref[:, _head_cols(hh)] + gate * o[h * tq:(h + 1) * tq, :]
                                        ).astype(o_ref.dtype)


def nsa_select(proj3, sel, act, prev, tq=256):
    bsz, T, _ = proj3.shape
    tq = min(tq, T)
    nt = T // tq
    n_sel = T // NSA_SEL_BLOCK
    lmax = -(-n_sel // SEL_STEP) * SEL_STEP
    tp = min(KPREP_ROWS, T)
    kc, vc = _nsa_key_consts(T, tp, onehot=True)
    qc = _nsa_query_consts()
    kaug, vaug = nsa_kprep(proj3, kc, vc, tp, (OFF_NKV + KV_W) // KV_W)
    wk = kaug.shape[-1]
    on = (act.reshape(bsz, nt, NSA_GROUPS, n_sel) > 0.0).astype(jnp.int32)
    seen = jnp.cumsum(on, axis=-1)
    ids = jnp.sum(seen[..., None, :] <= jnp.arange(lmax, dtype=jnp.int32)[:, None], axis=-1, dtype=jnp.int32)
    ids = ids.reshape(-1)
    cnt = seen[..., -1].reshape(-1)
    rows = NSA_HPG * tq
    grid_spec = pltpu.PrefetchScalarGridSpec(
        num_scalar_prefetch=2,
        grid=(bsz, nt),
        in_specs=[pl.BlockSpec((None, tq, NSA_QW), lambda b, i, *_: (b, i, OFF_NQ // NSA_QW)),
                  pl.BlockSpec((None, NSA_GROUPS, T + tp, wk), lambda b, i, *_: (b, 0, 0, 0)),
                  pl.BlockSpec((None, NSA_GROUPS, T + tp, 2 * NSA_DH), lambda b, i, *_: (b, 0, 0, 0)),
                  pl.BlockSpec((None, tq, NSA_GROUPS * n_sel), lambda b, i, *_: (b, i, 0)),
                  pl.BlockSpec((NSA_HEADS, EXTRA_W), lambda b, i, *_: (0, 0)),
                  pl.BlockSpec((None, tq, LANES), lambda b, i, *_: (b, i, OFF_SMALL // LANES)),
                  pl.BlockSpec((None, tq, NSA_QW), lambda b, i, *_: (b, i, 0))],
        out_specs=pl.BlockSpec((None, tq, NSA_QW), lambda b, i, *_: (b, i, 0)),
        scratch_shapes=[pltpu.VMEM((rows, wk), BF16),
                        pltpu.VMEM((SEL_STEP * NSA_SEL_BLOCK, wk), BF16),
                        pltpu.VMEM((SEL_STEP * NSA_SEL_BLOCK, 2 * NSA_DH), BF16),
                        pltpu.VMEM((1, rows), F32), pltpu.VMEM((2 * NSA_DH, rows), F32)])
    return pl.pallas_call(
        functools.partial(_nsa_select_kernel, tq=tq, n_sel=n_sel, lmax=lmax),
        grid_spec=grid_spec,
        out_shape=jax.ShapeDtypeStruct((bsz, T, NSA_QW), BF16),
        compiler_params=_cparams(("parallel", "arbitrary")),
        name="nsa_select",
    )(ids, cnt, proj3, kaug, vaug, sel, qc, proj3, prev)


def nsa_mixer(proj3, cmp_pos, cmp_w1, cmp_w2):
    kcv = nsa_compress(proj3, cmp_pos, cmp_w1, cmp_w2)
    o_cmp, sel, act = nsa_cmp_topk(proj3, kcv)
    o_cw = nsa_window(proj3, o_cmp)
    return nsa_select(proj3, sel, act, o_cw)


def _pack_w_in(w_in):
    w_in = w_in.astype(BF16)
    (g_q, g_k, g_v, g_r, g_a, n_q, n_kv, n_g, s_z, s_xbc, s_dt, m_g) = jnp.split(w_in, SPLIT_POINTS, axis=-1)
    pad = jnp.zeros(w_in.shape[:-1] + (D_PK - OFF_SMALL - GLA_RANK - 3 * NSA_HEADS - SSM_HEADS,), w_in.dtype)
    return jnp.concatenate([m_g, g_q, g_k, g_v, g_r, n_q, s_z, s_xbc, n_kv, g_a, n_g, s_dt, pad], axis=-1)


def kernel(x, w_in, gla_a2, gla_a_bias, gla_norm, nsa_cmp_pos, nsa_cmp_w1, nsa_cmp_w2, ssm_conv_w, ssm_conv_b,
           ssm_dt_bias, ssm_a_log, ssm_d, ssm_norm, w_branch, w_out, norm_pre_mix, norm_post_mix, norm_pre_ffn,
           norm_post_ffn, w_ffn_gate, w_ffn_up, w_ffn_down):
    bsz, T, D = x.shape
    depth = w_in.shape[0]
    n = bsz * T
    w_in_pk = _pack_w_in(w_in)
    wa_pad = jnp.zeros((depth, LANES, GLA_KW), F32).at[:, SM_GA:SM_GA + GLA_RANK].set(gla_a2).astype(BF16)
    w_branch_b = w_branch.astype(BF16)
    w_out_b = w_out.astype(BF16)
    w_gate_b = w_ffn_gate.astype(BF16)
    w_up_b = w_ffn_up.astype(BF16)
    w_down_b = w_ffn_down.astype(BF16)
    xf = x.reshape(n, D)
    for l in range(depth):
        proj = norm_matmul(xf, norm_pre_mix[l], w_in_pk[l])
        proj3 = proj.reshape(bsz, T, D_PK)
        y_gla = gla_mixer(proj3, wa_pad[l], gla_a_bias[l], gla_norm[l])
        y_nsa = nsa_mixer(proj3, nsa_cmp_pos[l], nsa_cmp_w1[l], nsa_cmp_w2[l])
        y_ssm = ssd_mixer(proj3, ssm_conv_w[l], ssm_conv_b[l], ssm_dt_bias[l], ssm_a_log[l], ssm_d[l], ssm_norm[l])
        merged = merge_branches(y_gla.reshape(n, BRANCH_W), y_nsa.reshape(n, BRANCH_W), y_ssm.reshape(n, BRANCH_W),
                                w_branch_b[l], proj)
        xf = proj_norm_residual(merged, w_out_b[l], xf, norm_post_mix[l], tm=512, tk=D)
        act = ffn_up(xf, norm_pre_ffn[l], w_gate_b[l], w_up_b[l])
        xf = proj_norm_residual(act, w_down_b[l], xf, norm_post_ffn[l], tm=1024, tk=512)
    return xf.reshape(bsz, T, D)
```

```python
import functools

import numpy as np
import jax
import jax.numpy as jnp
from jax import lax
from jax.experimental import pallas as pl
from jax.experimental.pallas import tpu as pltpu

F32 = jnp.float32
BF16 = jnp.bfloat16

D_MODEL = 2048
EPS = 1e-6
N_BRANCH = 3
BRANCH_W = 1024
GLA_HEADS, GLA_DK, GLA_DV, GLA_RANK, GLA_TAU, GLA_CHUNK = 4, 256, 256, 16, 16.0, 64
GLA_KW = GLA_HEADS * GLA_DK
GLA_VW = GLA_HEADS * GLA_DV
NSA_HEADS, NSA_GROUPS, NSA_DH = 16, 2, 64
NSA_HPG = NSA_HEADS // NSA_GROUPS
NSA_QW = NSA_HEADS * NSA_DH
NSA_KVW = NSA_GROUPS * NSA_DH
NSA_CMP_LEN, NSA_CMP_STRIDE, NSA_SEL_BLOCK, NSA_TOPN, NSA_WINDOW = 32, 16, 64, 16, 512
BIG = 1e30
SSM_HEADS, SSM_HEADDIM, SSM_GROUPS, SSM_STATE, SSM_CONV, SSM_CHUNK = 16, 64, 4, 128, 4, 64
SSM_INNER = SSM_HEADS * SSM_HEADDIM
SSM_HPG = SSM_HEADS // SSM_GROUPS
SSM_CONV_CH = SSM_INNER + 2 * SSM_GROUPS * SSM_STATE
D_FF = ((8 * D_MODEL // 3 + 255) // 256) * 256
IN_SIZES = (GLA_KW, GLA_KW, GLA_VW, GLA_VW, GLA_RANK, NSA_QW, 6 * NSA_KVW, 3 * NSA_HEADS,
            SSM_INNER, SSM_CONV_CH, SSM_HEADS, N_BRANCH * D_MODEL)
SPLIT_POINTS = tuple(int(v) for v in np.cumsum(IN_SIZES)[:-1])

LANES = 128
VMEM_LIMIT = 56 * 1024 * 1024

OFF_MG = 0
OFF_GQ = OFF_MG + N_BRANCH * D_MODEL
OFF_GK = OFF_GQ + GLA_KW
OFF_GV = OFF_GK + GLA_KW
OFF_GR = OFF_GV + GLA_VW
OFF_NQ = OFF_GR + GLA_VW
OFF_SZ = OFF_NQ + NSA_QW
OFF_SX = OFF_SZ + SSM_INNER
OFF_SB = OFF_SX + SSM_INNER
OFF_SC = OFF_SB + SSM_GROUPS * SSM_STATE
OFF_NKV = OFF_SC + SSM_GROUPS * SSM_STATE
OFF_SMALL = OFF_NKV + 6 * NSA_KVW
D_PK = OFF_SMALL + 2 * LANES
SM_GA = 0
SM_NG = 16
SM_DT = 64


def _cparams(sem):
    return pltpu.CompilerParams(dimension_semantics=sem, vmem_limit_bytes=VMEM_LIMIT)


def _split3(x):
    hi = x.astype(BF16)
    r1 = x - hi.astype(F32)
    mid = r1.astype(BF16)
    lo = (r1 - mid.astype(F32)).astype(BF16)
    return hi, mid, lo


def _dot(a, b):
    return jnp.dot(a, b, preferred_element_type=F32)


def _dot_nt(a, b):
    return lax.dot_general(a, b, (((1,), (1,)), ((), ())), preferred_element_type=F32)


def _dot01_left(m01, x):
    hi, mid, lo = _split3(x)
    return _dot(m01, hi) + _dot(m01, mid) + _dot(m01, lo)


def _dot01_right(x, m01):
    hi, mid, lo = _split3(x)
    return _dot(hi, m01) + _dot(mid, m01) + _dot(lo, m01)


def _silu(x):
    return x / (1.0 + jnp.exp(-x))


def _norm_rows(x_ref, g_ref, h_ref):
    @pl.when(pl.program_id(1) == 0)
    def _():
        x = x_ref[...]
        y = x * lax.rsqrt(jnp.mean(x * x, axis=-1, keepdims=True) + EPS)
        h_ref[...] = (y * g_ref[...]).astype(h_ref.dtype)


def _norm_mm_kernel(x_ref, g_ref, w_ref, o_ref, h_ref):
    _norm_rows(x_ref, g_ref, h_ref)
    o_ref[...] = _dot(h_ref[...], w_ref[...]).astype(o_ref.dtype)


def norm_matmul(x, g, w, out_dtype=F32, tm=1024, tn=1024):
    m, k = x.shape
    n = w.shape[1]
    tm = min(tm, m)
    tn = min(tn, n)
    return pl.pallas_call(
        _norm_mm_kernel,
        grid=(m // tm, n // tn),
        in_specs=[pl.BlockSpec((tm, k), lambda i, j: (i, 0)), pl.BlockSpec((1, k), lambda i, j: (0, 0)),
                  pl.BlockSpec((k, tn), lambda i, j: (0, j))],
        out_specs=pl.BlockSpec((tm, tn), lambda i, j: (i, j)),
        out_shape=jax.ShapeDtypeStruct((m, n), out_dtype),
        scratch_shapes=[pltpu.VMEM((tm, k), BF16)],
        compiler_params=_cparams(("parallel", "arbitrary")),
        name="in_proj",
    )(x, g.reshape(1, k), w)


def _merge_kernel(yg_ref, yn_ref, ys_ref, wg_ref, wn_ref, ws_ref, g0_ref, g1_ref, g2_ref, o_ref):
    def gate(ref):
        return 1.0 / (1.0 + jnp.exp(-ref[...]))

    acc = gate(g0_ref) * _dot(yg_ref[...], wg_ref[...])
    acc += gate(g1_ref) * _dot(yn_ref[...], wn_ref[...])
    acc += gate(g2_ref) * _dot(ys_ref[...], ws_ref[...])
    o_ref[...] = acc.astype(o_ref.dtype)


def merge_branches(y_gla, y_nsa, y_ssm, w_branch, proj, tm=1024, tn=512):
    m = y_gla.shape[0]
    d = w_branch.shape[-1]
    tm = min(tm, m)
    nj = d // tn
    ys = pl.BlockSpec((tm, BRANCH_W), lambda i, j: (i, 0))

    def wspec(b):
        return pl.BlockSpec((None, BRANCH_W, tn), lambda i, j, b=b: (b, 0, j))

    def gspec(b):
        return pl.BlockSpec((tm, tn), lambda i, j, b=b: (i, (OFF_MG + b * D_MODEL) // tn + j))

    return pl.pallas_call(
        _merge_kernel,
        grid=(m // tm, nj),
        in_specs=[ys, ys, ys, wspec(0), wspec(1), wspec(2), gspec(0), gspec(1), gspec(2)],
        out_specs=pl.BlockSpec((tm, tn), lambda i, j: (i, j)),
        out_shape=jax.ShapeDtypeStruct((m, d), BF16),
        compiler_params=_cparams(("parallel", "parallel")),
        name="merge",
    )(y_gla, y_nsa, y_ssm, w_branch, w_branch, w_branch, proj, proj, proj)


def _proj_norm_res_kernel(a_ref, w_ref, x_ref, g_ref, o_ref, acc_ref):
    k = pl.program_id(1)

    @pl.when(k == 0)
    def _():
        acc_ref[...] = jnp.zeros_like(acc_ref)

    acc_ref[...] += _dot(a_ref[...], w_ref[...])

    @pl.when(k == pl.num_programs(1) - 1)
    def _():
        f = acc_ref[...]
        y = f * lax.rsqrt(jnp.mean(f * f, axis=-1, keepdims=True) + EPS)
        o_ref[...] = x_ref[...] + y * g_ref[...]


def proj_norm_residual(a, w, x, g, tm=1024, tk=512):
    m, kk = a.shape
    d = w.shape[1]
    tm = min(tm, m)
    return pl.pallas_call(
        _proj_norm_res_kernel,
        grid=(m // tm, kk // tk),
        in_specs=[pl.BlockSpec((tm, tk), lambda i, k: (i, k)),
                  pl.BlockSpec((tk, d), lambda i, k: (k, 0)),
                  pl.BlockSpec((tm, d), lambda i, k: (i, 0)),
                  pl.BlockSpec((1, d), lambda i, k: (0, 0))],
        out_specs=pl.BlockSpec((tm, d), lambda i, k: (i, 0)),
        out_shape=jax.ShapeDtypeStruct((m, d), F32),
        scratch_shapes=[pltpu.VMEM((tm, d), F32)],
        compiler_params=_cparams(("parallel", "arbitrary")),
        name="proj_norm_res",
    )(a, w, x, g.reshape(1, d))


def _ffn_up_kernel(x_ref, g_ref, wg_ref, wu_ref, o_ref, h_ref):
    _norm_rows(x_ref, g_ref, h_ref)
    h = h_ref[...]
    a = _dot(h, wg_ref[...])
    u = _dot(h, wu_ref[...])
    o_ref[...] = (_silu(a) * u).astype(o_ref.dtype)


def ffn_up(x, g, wg, wu, tm=1024, tn=512):
    m, k = x.shape
    n = wg.shape[1]
    tm = min(tm, m)
    return pl.pallas_call(
        _ffn_up_kernel,
        grid=(m // tm, n // tn),
        in_specs=[pl.BlockSpec((tm, k), lambda i, j: (i, 0)),
                  pl.BlockSpec((1, k), lambda i, j: (0, 0)),
                  pl.BlockSpec((k, tn), lambda i, j: (0, j)),
                  pl.BlockSpec((k, tn), lambda i, j: (0, j))],
        out_specs=pl.BlockSpec((tm, tn), lambda i, j: (i, j)),
        out_shape=jax.ShapeDtypeStruct((m, n), BF16),
        scratch_shapes=[pltpu.VMEM((tm, k), BF16)],
        compiler_params=_cparams(("parallel", "arbitrary")),
        name="ffn_up",
    )(x, g.reshape(1, k), wg, wu)


def _gla_kernel(q_ref, k_ref, v_ref, r_ref, sm_ref, wa_ref, ba_ref, ng_ref, tri_ref, same_ref, o_ref,
                st_ref, *, tb):
    C = GLA_CHUNK

    @pl.when(pl.program_id(1) == 0)
    def _():
        st_ref[...] = jnp.zeros_like(st_ref)

    pre = _dot(sm_ref[...].astype(BF16), wa_ref[...]) + ba_ref[...]
    la = (jnp.minimum(pre, 0.0) - jnp.log1p(jnp.exp(-jnp.abs(pre)))) * (1.0 / GLA_TAU)
    tri = tri_ref[...]
    same = same_ref[...]
    ri = lax.broadcasted_iota(jnp.int32, (tb, tb), 0)
    ci = lax.broadcasted_iota(jnp.int32, (tb, tb), 1)
    causal = (ri >= ci) & (ri // C == ci // C)
    for h in range(GLA_HEADS):
        ck = slice(h * GLA_DK, (h + 1) * GLA_DK)
        cv = slice(h * GLA_DV, (h + 1) * GLA_DV)
        hi, mid, lo = _split3(la[:, ck])
        bcum = _dot(tri, hi) + _dot(tri, mid) + _dot(tri, lo)
        b_last = _dot(same, hi) + _dot(same, mid) + _dot(same, lo)
        q = q_ref[:, ck] * (GLA_DK ** -0.5)
        k = k_ref[:, ck]
        v = v_ref[:, cv]
        q_dec = (q * jnp.exp(bcum)).astype(BF16)
        k_inv = (k * jnp.exp(-bcum)).astype(BF16)
        k_end = (k * jnp.exp(b_last - bcum)).astype(BF16)
        att = jnp.where(causal, _dot_nt(q_dec, k_inv), 0.0)
        o_intra = _dot(att.astype(BF16), v.astype(BF16))
        st = st_ref[h]
        outs = []
        for c in range(tb // C):
            rows = slice(c * C, (c + 1) * C)
            outs.append(o_intra[rows] + _dot_nt(q_dec[rows], st.astype(BF16)))
            st = jnp.exp(b_last[c * C:c * C + 1, :]) * st + _dot(v[rows].T.astype(BF16), k_end[rows])
        st_ref[h] = st
        o = jnp.concatenate(outs, axis=0)
        y = o * lax.rsqrt(jnp.mean(o * o, axis=-1, keepdims=True) + EPS) * ng_ref[...]
        o_ref[:, cv] = (y * _silu(r_ref[:, cv])).astype(o_ref.dtype)


def _chunk_masks(tb, chunk):
    idx = np.arange(tb)
    same = (idx[:, None] // chunk) == (idx[None, :] // chunk)
    return jnp.asarray(same & (idx[:, None] >= idx[None, :]), BF16), jnp.asarray(same, BF16)


def gla_mixer(proj3, wa_pad, ba, norm_g, tb=256):
    bsz, T, _ = proj3.shape
    tb = min(tb, T)
    tri, same = _chunk_masks(tb, GLA_CHUNK)

    def col(off, w):
        return pl.BlockSpec((None, tb, w), lambda b, t, o=off // w: (b, t, o))

    def full(shape):
        return pl.BlockSpec(shape, lambda b, t: (0,) * len(shape))

    return pl.pallas_call(
        functools.partial(_gla_kernel, tb=tb),
        grid=(bsz, T // tb),
        in_specs=[col(OFF_GQ, GLA_KW), col(OFF_GK, GLA_KW), col(OFF_GV, GLA_VW), col(OFF_GR, GLA_VW),
                  col(OFF_SMALL, LANES),
                  full((LANES, GLA_KW)), full((1, GLA_KW)), full((1, GLA_DV)), full((tb, tb)), full((tb, tb))],
        out_specs=pl.BlockSpec((None, tb, GLA_VW), lambda b, t: (b, t, 0)),
        out_shape=jax.ShapeDtypeStruct((bsz, T, GLA_VW), BF16),
        scratch_shapes=[pltpu.VMEM((GLA_HEADS, GLA_DV, GLA_DK), F32)],
        compiler_params=_cparams(("parallel", "arbitrary")),
        name="gla",
    )(proj3, proj3, proj3, proj3, proj3, wa_pad, ba.reshape(1, GLA_KW), norm_g.reshape(1, GLA_DV), tri, same)


def _ssd_kernel(z_ref, x_ref, bm_ref, cm_ref, sm_ref, cwx_ref, cwb_ref, cwc_ref, cbx_ref, cbb_ref, cbc_ref,
                dtb_ref, alog_ref, dskip_ref, ng_ref, tri_ref, exp_ref, o_ref,
                st_ref, extx_ref, extb_ref, extc_ref, xa_ref, ba_ref, ca_ref, dt_ref, a_ref, *, tb, nchunk):
    L = SSM_CHUNK
    GW = SSM_HPG * SSM_HEADDIM
    NS = SSM_STATE
    first = pl.program_id(1) == 0

    @pl.when(first)
    def _():
        st_ref[...] = jnp.zeros_like(st_ref)

    def conv_silu(src_ref, ext_ref, w_ref, b_ref, dst_ref):
        @pl.when(first)
        def _():
            ext_ref[0:8, :] = jnp.zeros((8, ext_ref.shape[1]), F32)

        @pl.when(jnp.logical_not(first))
        def _():
            ext_ref[0:8, :] = ext_ref[tb:tb + 8, :]

        ext_ref[8:8 + tb, :] = src_ref[...]
        acc = b_ref[...] + w_ref[SSM_CONV - 1:SSM_CONV, :] * ext_ref[8:8 + tb, :]
        for j in range(1, SSM_CONV):
            acc = acc + w_ref[SSM_CONV - 1 - j:SSM_CONV - j, :] * ext_ref[8 - j:8 - j + tb, :]
        dst_ref[...] = _silu(acc)

    conv_silu(x_ref, extx_ref, cwx_ref, cbx_ref, xa_ref)
    conv_silu(bm_ref, extb_ref, cwb_ref, cbb_ref, ba_ref)
    conv_silu(cm_ref, extc_ref, cwc_ref, cbc_ref, ca_ref)

    v = sm_ref[...] + dtb_ref[...]
    dt = jnp.maximum(v, 0.0) + jnp.log1p(jnp.exp(-jnp.abs(v)))
    dt_ref[...] = dt
    a_ref[...] = dt * (-jnp.exp(alog_ref[...]))

    tri = tri_ref[...]
    row = lax.broadcasted_iota(jnp.int32, (L, GW), 0)
    lane = lax.broadcasted_iota(jnp.int32, (L, GW), 1)
    lane_in = jnp.bitwise_and(lane, SSM_HEADDIM - 1)
    eye_t = (lane_in == row).astype(F32)
    tril_t = lane_in <= row
    bd_mask = (lax.broadcasted_iota(jnp.int32, (GW, GW), 0) // L
               == lax.broadcasted_iota(jnp.int32, (GW, GW), 1) // SSM_HEADDIM).astype(F32).astype(BF16)

    states = [st_ref[g] for g in range(SSM_GROUPS)]
    cum_parts = _split3(_dot01_left(tri, a_ref[...]))
    dt_parts = _split3(dt_ref[...])
    cum_all, dt_all = [], []
    for g in range(SSM_GROUPS):
        e_g = exp_ref[:, g * GW:(g + 1) * GW]
        cum_all.append(sum(_dot(p, e_g) for p in cum_parts))
        dt_all.append(sum(_dot(p, e_g) for p in dt_parts))
    for c in range(nchunk):
        rows = slice(c * L, (c + 1) * L)
        for g in range(SSM_GROUPS):
            cs = slice(g * GW, (g + 1) * GW)
            cum_e = cum_all[g][rows]
            dt_e = dt_all[g][rows]
            cum_last = cum_e[L - 1:L, :]
            r_row = jnp.sum(cum_e * eye_t, axis=0, keepdims=True)
            decay = jnp.exp(jnp.where(tril_t, cum_e - r_row, -jnp.inf))
            x_g = xa_ref[rows, cs]
            xdt = x_g * dt_e
            b_g = ba_ref[rows, g * NS:(g + 1) * NS]
            c_g = ca_ref[rows, g * NS:(g + 1) * NS].astype(BF16)
            bb = b_g.astype(BF16)
            cb_t = _dot_nt(c_g, jnp.concatenate([bb] * SSM_HPG, axis=0))
            xdt_bd = jnp.concatenate([xdt.astype(BF16)] * SSM_HPG, axis=0) * bd_mask
            y = _dot((cb_t * decay).astype(BF16), xdt_bd)
            st = states[g]
            y = y + _dot(c_g, st.astype(BF16)) * jnp.exp(cum_e)
            dend = jnp.exp(cum_last - cum_e)
            states[g] = jnp.exp(cum_last) * st + _dot(b_g.T.astype(BF16), (dend * xdt).astype(BF16))
            y = y + x_g * dskip_ref[:, cs]
            y = y * _silu(z_ref[rows, cs])
            y = y * lax.rsqrt(jnp.mean(y * y, axis=-1, keepdims=True) + EPS) * ng_ref[:, cs]
            o_ref[rows, cs] = y.astype(o_ref.dtype)
    for g in range(SSM_GROUPS):
        st_ref[g] = states[g]


def _ssd_expand():
    e = np.zeros((LANES, SSM_INNER), np.float32)
    for h in range(SSM_HEADS):
        e[SM_DT + h, h * SSM_HEADDIM:(h + 1) * SSM_HEADDIM] = 1.0
    return jnp.asarray(e, BF16)


def _small_row(v, off):
    return jnp.zeros((1, LANES), F32).at[0, off:off + v.shape[0]].set(v.astype(F32))


def ssd_mixer(proj3, conv_w, conv_b, dt_bias, a_log, d_skip, norm_g, tb=256):
    bsz, T, _ = proj3.shape
    tb = min(tb, T)
    nchunk = tb // SSM_CHUNK
    GN = SSM_GROUPS * SSM_STATE
    expand = _ssd_expand()
    tri, _ = _chunk_masks(tb, SSM_CHUNK)
    cwx, cwb, cwc = conv_w[:, :SSM_INNER], conv_w[:, SSM_INNER:SSM_INNER + GN], conv_w[:, SSM_INNER + GN:]
    cb2 = conv_b.reshape(1, SSM_CONV_CH)
    cbx, cbb, cbc = cb2[:, :SSM_INNER], cb2[:, SSM_INNER:SSM_INNER + GN], cb2[:, SSM_INNER + GN:]
    dtb = _small_row(dt_bias, SM_DT)
    alog = _small_row(a_log, SM_DT)
    dskip = jnp.repeat(d_skip.astype(F32), SSM_HEADDIM).reshape(1, SSM_INNER)

    def col(off, w):
        return pl.BlockSpec((None, tb, w), lambda b, t, o=off // w: (b, t, o))

    def full(shape):
        return pl.BlockSpec(shape, lambda b, t: (0,) * len(shape))

    return pl.pallas_call(
        functools.partial(_ssd_kernel, tb=tb, nchunk=nchunk),
        grid=(bsz, T // tb),
        in_specs=[col(OFF_SZ, SSM_INNER), col(OFF_SX, SSM_INNER), col(OFF_SB, GN), col(OFF_SC, GN),
                  col(OFF_SMALL, LANES),
                  full((SSM_CONV, SSM_INNER)), full((SSM_CONV, GN)), full((SSM_CONV, GN)),
                  full((1, SSM_INNER)), full((1, GN)), full((1, GN)),
                  full((1, LANES)), full((1, LANES)), full((1, SSM_INNER)), full((1, SSM_INNER)),
                  full((tb, tb)), full((LANES, SSM_INNER))],
        out_specs=pl.BlockSpec((None, tb, SSM_INNER), lambda b, t: (b, t, 0)),
        out_shape=jax.ShapeDtypeStruct((bsz, T, SSM_INNER), BF16),
        scratch_shapes=[pltpu.VMEM((SSM_GROUPS, SSM_STATE, SSM_HPG * SSM_HEADDIM), F32),
                        pltpu.VMEM((tb + 8, SSM_INNER), F32), pltpu.VMEM((tb + 8, GN), F32),
                        pltpu.VMEM((tb + 8, GN), F32),
                        pltpu.VMEM((tb, SSM_INNER), F32), pltpu.VMEM((tb, GN), F32), pltpu.VMEM((tb, GN), F32),
                        pltpu.VMEM((tb, LANES), F32), pltpu.VMEM((tb, LANES), F32)],
        compiler_params=_cparams(("parallel", "arbitrary")),
        name="ssd",
    )(proj3, proj3, proj3, proj3, proj3, cwx, cwb, cwc, cbx, cbb, cbc, dtb, alog, dskip,
      norm_g.reshape(1, SSM_INNER), tri, expand)


NSA_SLOPES = tuple(float(np.float32(2.0 ** (-8.0 * (i + 1) / NSA_HEADS))) for i in range(NSA_HEADS))
LOG2E = float(np.log2(np.e))
NSA_SCALE = NSA_DH ** -0.5 * LOG2E
KV_W = 2 * NSA_KVW
NEG_HUGE = -3.0e38


def _head_cols(hh):
    return slice(hh * NSA_DH, (hh + 1) * NSA_DH)


def _nsa_compress_kernel(k_ref, v_ref, pos_ref, bd1_ref, bd2_ref, o_ref, *, n16):
    S = NSA_CMP_STRIDE
    top = jnp.zeros((n16, KV_W), F32)
    bot = jnp.zeros((n16, KV_W), F32)
    for l in range(S):
        rows = pl.ds(l, n16, stride=S)
        x = jnp.concatenate([k_ref[rows, :], v_ref[rows, :]], axis=-1)
        top += _dot((x + pos_ref[l:l + 1, :]).astype(BF16), bd1_ref[l])
        bot += _dot((x + pos_ref[S + l:S + l + 1, :]).astype(BF16), bd1_ref[S + l])
    pre = top + pltpu.roll(bot, n16 - 1, axis=0)
    out = _dot(_silu(pre).astype(BF16), bd2_ref[...])
    row = lax.broadcasted_iota(jnp.int32, (n16, KV_W), 0)
    o_ref[...] = jnp.where(row < n16 - 1, out, 0.0)


def nsa_compress(proj3, cmp_pos, cmp_w1, cmp_w2):
    bsz, T, _ = proj3.shape
    n16 = T // NSA_CMP_STRIDE
    sel = np.array([0, 0, 1, 1])
    eye = jnp.eye(4, dtype=F32)
    w1r = cmp_w1.reshape(2, NSA_CMP_LEN, NSA_DH, NSA_DH)[sel]
    bd1 = jnp.einsum('ab,alde->ladbe', eye, w1r).reshape(NSA_CMP_LEN, KV_W, KV_W).astype(BF16)
    bd2 = jnp.einsum('ab,ade->adbe', eye, cmp_w2[sel]).reshape(KV_W, KV_W).astype(BF16)
    pos = jnp.concatenate([cmp_pos[0], cmp_pos[0], cmp_pos[1], cmp_pos[1]], axis=-1)
    return pl.pallas_call(
        functools.partial(_nsa_compress_kernel, n16=n16),
        grid=(bsz,),
        in_specs=[pl.BlockSpec((None, T, NSA_KVW), lambda b: (b, 0, OFF_NKV // NSA_KVW)),
                  pl.BlockSpec((None, T, NSA_KVW), lambda b: (b, 0, OFF_NKV // NSA_KVW + 1)),
                  pl.BlockSpec((NSA_CMP_LEN, KV_W), lambda b: (0, 0)),
                  pl.BlockSpec((NSA_CMP_LEN, KV_W, KV_W), lambda b: (0, 0, 0)),
                  pl.BlockSpec((KV_W, KV_W), lambda b: (0, 0))],
        out_specs=pl.BlockSpec((None, n16, KV_W), lambda b: (b, 0, 0)),
        out_shape=jax.ShapeDtypeStruct((bsz, n16, KV_W), F32),
        compiler_params=_cparams(("parallel",)),
        name="nsa_compress",
    )(proj3, proj3, pos, bd1, bd2)


CMP_SEG_MAX = 320


def _cmp_segments(nc):
    bad = 2 * LANES
    if nc <= CMP_SEG_MAX + NSA_DH and nc != bad:
        return [(0, nc)]
    first = CMP_SEG_MAX if nc > CMP_SEG_MAX else nc - NSA_DH
    return [(0, first), (first, nc)]


def _stack_queries(q_ref, qc_ref, qa_ref, g, tq, shared=None):
    for h in range(NSA_HPG):
        hh = g * NSA_HPG + h
        qh = (q_ref[:, _head_cols(hh)] * NSA_SCALE).astype(BF16)
        qx = jnp.broadcast_to(qc_ref[hh:hh + 1, :], (tq, EXTRA_W))
        qa_ref[h * tq:(h + 1) * tq, 0:NSA_DH + EXTRA_W] = jnp.concatenate([qh, qx], axis=-1)
        if shared is not None:
            qa_ref[h * tq:(h + 1) * tq, NSA_DH + EXTRA_W:] = shared


def _nsa_cmp_topk_kernel(q_ref, kcv_ref, cpos_ref, qc_ref, sm_ref, ov_ref, o_ref, sel_ref, act_ref, qa_ref, imp_ref,
                         *, tq, n16, n_sel, n_top, cw):
    t0 = pl.program_id(1) * tq
    tpos = t0 + lax.broadcasted_iota(jnp.int32, (1, tq), 1)
    any_valid = (tpos >= NSA_CMP_LEN - 1).astype(F32)
    gates = 1.0 / (1.0 + jnp.exp(-sm_ref[...]))
    blk = lax.broadcasted_iota(jnp.int32, (n_sel, 1), 0)
    blk_f = blk.astype(F32)
    cur = tpos // NSA_SEL_BLOCK
    forced = (blk == 0) | (blk == cur) | (blk == cur - 1)
    future = blk * NSA_SEL_BLOCK > tpos
    blk_row = lax.broadcasted_iota(jnp.int32, (1, n_sel), 1)

    def attend(nc):
        segs = _cmp_segments(nc)
        bias = []
        for a, b in segs:
            cmp_end = (a + lax.broadcasted_iota(jnp.int32, (b - a, 1), 0)) * NSA_CMP_STRIDE + (NSA_CMP_LEN - 1)
            bias.append(jnp.where(cmp_end <= tpos, 0.0, -BIG))
        for g in range(NSA_GROUPS):
            kc = [jnp.concatenate([kcv_ref[a:b, g * NSA_DH:(g + 1) * NSA_DH].astype(BF16), cpos_ref[a:b, :]], axis=-1)
                  for a, b in segs]
            vo = [jnp.concatenate([kcv_ref[a:b, NSA_KVW + g * NSA_DH:NSA_KVW + (g + 1) * NSA_DH].T.astype(BF16),
                                   ov_ref[:, a:b]], axis=0) for a, b in segs]
            _stack_queries(q_ref, qc_ref, qa_ref, g, tq)
            imp = None
            for h in range(NSA_HPG):
                hh = g * NSA_HPG + h
                qa = qa_ref[h * tq:(h + 1) * tq, :]
                s = [_dot_nt(kc[i], qa) + bias[i] for i in range(len(segs))]
                m = functools.reduce(jnp.maximum, [jnp.max(x, axis=0, keepdims=True) for x in s])
                e = [jnp.exp2(x - m) for x in s]
                inv = any_valid / sum(jnp.sum(x, axis=0, keepdims=True) for x in e)
                r = sum(_dot(vo[i], e[i].astype(BF16)) for i in range(len(segs))) * inv
                imp = r[NSA_DH:] if imp is None else imp + r[NSA_DH:]
                o_ref[:, _head_cols(hh)] = gates[:, SM_NG + hh:SM_NG + hh + 1] * r[:NSA_DH].T
            imp_ref[g] = imp

    need = (t0 + tq) // NSA_CMP_STRIDE
    nchunks = n16 // cw
    for k in range(1, nchunks + 1):
        lo = (k - 1) * cw
        cond = (need > lo) & (need <= k * cw) if k < nchunks else need > lo
        pl.when(cond)(functools.partial(attend, k * cw))

    for g in range(NSA_GROUPS):
        work = jnp.where(forced, BIG, jnp.where(future, -BIG, imp_ref[g]))
        sel_t = jnp.zeros((n_sel, tq), F32)
        for _ in range(n_top):
            top = jnp.max(work, axis=0, keepdims=True)
            idx = jnp.min(jnp.where(work == top, blk_f, float(n_sel)), axis=0, keepdims=True)
            pick = blk_f == idx
            sel_t = jnp.where(pick, 1.0, sel_t)
            work = jnp.where(pick, NEG_HUGE, work)
        sel = sel_t.T
        sel_ref[:, g * n_sel:(g + 1) * n_sel] = sel.astype(sel_ref.dtype)
        union = jnp.max(sel, axis=0, keepdims=True)
        act_ref[:, g * n_sel:(g + 1) * n_sel] = jnp.where(blk_row * NSA_SEL_BLOCK < t0, union, 0.0)


def _nsa_overlap(n16, n_sel):
    n_cmp = n16 - 1
    tok = (np.arange(n_cmp) * NSA_CMP_STRIDE)[:, None] + np.arange(NSA_CMP_LEN)[None, :]
    ov = np.zeros((n16, n_sel), np.float32)
    np.add.at(ov, (np.repeat(np.arange(n_cmp), NSA_CMP_LEN), (tok // NSA_SEL_BLOCK).ravel()), 1.0 / NSA_CMP_LEN)
    return jnp.asarray(ov.T, BF16)


def nsa_cmp_topk(proj3, kcv, tq=256):
    bsz, T, _ = proj3.shape
    tq = min(tq, T)
    n16 = T // NSA_CMP_STRIDE
    n_sel = T // NSA_SEL_BLOCK
    n_top = min(NSA_TOPN, n_sel)
    cmp_end = np.arange(n16) * NSA_CMP_STRIDE + NSA_CMP_LEN - 1
    cpos = jnp.asarray(_pos_cols(cmp_end, np.ones(n16, bool)), BF16)
    return pl.pallas_call(
        functools.partial(_nsa_cmp_topk_kernel, tq=tq, n16=n16, n_sel=n_sel, n_top=n_top, cw=min(LANES // 2, n16)),
        grid=(bsz, T // tq),
        in_specs=[pl.BlockSpec((None, tq, NSA_QW), lambda b, i: (b, i, OFF_NQ // NSA_QW)),
                  pl.BlockSpec((None, n16, KV_W), lambda b, i: (b, 0, 0)),
                  pl.BlockSpec((n16, EXTRA_W), lambda b, i: (0, 0)),
                  pl.BlockSpec((NSA_HEADS, EXTRA_W), lambda b, i: (0, 0)),
                  pl.BlockSpec((None, tq, LANES), lambda b, i: (b, i, OFF_SMALL // LANES)),
                  pl.BlockSpec((n_sel, n16), lambda b, i: (0, 0))],
        out_specs=[pl.BlockSpec((None, tq, NSA_QW), lambda b, i: (b, i, 0)),
                   pl.BlockSpec((None, tq, NSA_GROUPS * n_sel), lambda b, i: (b, i, 0)),
                   pl.BlockSpec((None, None, 1, NSA_GROUPS * n_sel), lambda b, i: (b, i, 0, 0))],
        out_shape=[jax.ShapeDtypeStruct((bsz, T, NSA_QW), F32),
                   jax.ShapeDtypeStruct((bsz, T, NSA_GROUPS * n_sel), BF16),
                   jax.ShapeDtypeStruct((bsz, T // tq, 1, NSA_GROUPS * n_sel), F32)],
        scratch_shapes=[pltpu.VMEM((NSA_HPG * tq, NSA_DH + EXTRA_W), BF16),
                        pltpu.VMEM((NSA_GROUPS, n_sel, tq), F32)],
        compiler_params=_cparams(("parallel", "parallel")),
        name="nsa_cmp_topk",
    )(proj3, kcv, cpos, _nsa_query_consts(), proj3, _nsa_overlap(n16, n_sel))


def _nsa_window_kernel(q_ref, k0_ref, k1_ref, k2_ref, v0_ref, v1_ref, v2_ref, qc_ref, sm_ref, prev_ref, o_ref,
                       qa_ref, *, tq):
    nb = NSA_WINDOW // tq + 1
    i = pl.program_id(1)
    tpos = i * tq + lax.broadcasted_iota(jnp.int32, (1, tq), 1)
    kpos = (i - (nb - 1)) * tq + lax.broadcasted_iota(jnp.int32, (nb * tq, 1), 0)
    d = tpos - kpos
    valid = (d >= 0) & (d < NSA_WINDOW) & (kpos >= 0)
    hpc = NSA_HPG // WIN_CHAINS
    bias = jnp.concatenate([jnp.where(valid, 0.0, -BIG)] * hpc, axis=1)
    gates = 1.0 / (1.0 + jnp.exp(-sm_ref[...]))
    for g in range(NSA_GROUPS):
        kw = jnp.concatenate([k0_ref[g], k1_ref[g], k2_ref[g]], axis=0)
        vw = jnp.concatenate([v0_ref[g], v1_ref[g], v2_ref[g]], axis=0)
        vw_t = vw.astype(F32).T.astype(BF16)
        _stack_queries(q_ref, qc_ref, qa_ref, g, tq)
        for c in range(WIN_CHAINS):
            s = _dot_nt(kw, qa_ref[c * hpc * tq:(c + 1) * hpc * tq, :]) + bias
            e = jnp.exp2(s - jnp.max(s, axis=0, keepdims=True))
            acc = _dot(vw_t, e.astype(BF16))
            o = (acc[:NSA_DH, :] * (1.0 / acc[NSA_DH:NSA_DH + 1, :])).T
            for h in range(hpc):
                hh = g * NSA_HPG + c * hpc + h
                gate = gates[:, SM_NG + 2 * NSA_HEADS + hh:SM_NG + 2 * NSA_HEADS + hh + 1]
                o_ref[:, _head_cols(hh)] = prev_ref[:, _head_cols(hh)] + gate * o[h * tq:(h + 1) * tq]


def nsa_window(proj3, prev, tq=256):
    bsz, T, _ = proj3.shape
    tq = min(tq, T)
    assert NSA_WINDOW % tq == 0 and NSA_WINDOW // tq == 2
    tp = min(KPREP_ROWS, T)
    kc, vc = _nsa_key_consts(T, tp, onehot=False)
    kaug, vaug = nsa_kprep(proj3, kc, vc, tp, (OFF_NKV + 2 * KV_W) // KV_W)
    wk = kaug.shape[-1]

    def kvspec(back, w):
        return pl.BlockSpec((None, NSA_GROUPS, tq, w), lambda b, i, back=back: (b, 0, jnp.maximum(i - back, 0), 0))

    return pl.pallas_call(
        functools.partial(_nsa_window_kernel, tq=tq),
        grid=(bsz, T // tq),
        in_specs=[pl.BlockSpec((None, tq, NSA_QW), lambda b, i: (b, i, OFF_NQ // NSA_QW)),
                  kvspec(2, wk), kvspec(1, wk), kvspec(0, wk),
                  kvspec(2, 2 * NSA_DH), kvspec(1, 2 * NSA_DH), kvspec(0, 2 * NSA_DH),
                  pl.BlockSpec((NSA_HEADS, EXTRA_W), lambda b, i: (0, 0)),
                  pl.BlockSpec((None, tq, LANES), lambda b, i: (b, i, OFF_SMALL // LANES)),
                  pl.BlockSpec((None, tq, NSA_QW), lambda b, i: (b, i, 0))],
        out_specs=pl.BlockSpec((None, tq, NSA_QW), lambda b, i: (b, i, 0)),
        out_shape=jax.ShapeDtypeStruct((bsz, T, NSA_QW), F32),
        scratch_shapes=[pltpu.VMEM((NSA_HPG * tq, wk), BF16)],
        compiler_params=_cparams(("parallel", "parallel")),
        name="nsa_window",
    )(proj3, kaug, kaug, kaug, vaug, vaug, vaug, _nsa_query_consts(), proj3, prev)


MASK_BIG = 1e30
SEL_STEP = 8
KPREP_ROWS = 1024
NSA_CHAINS = 1
WIN_CHAINS = 2
POS_HI = 128
EXTRA_W = NSA_DH


def _bf16_pieces(x):
    x = np.float32(x)
    out = []
    for _ in range(3):
        p = np.float32(np.asarray(x, dtype=jnp.bfloat16))
        out.append(p)
        x = np.float32(x - p)
    return out


def _pos_cols(pos, real):
    c = np.zeros((len(pos), EXTRA_W), np.float32)
    for j in range(3):
        c[real, j] = (pos[real] // POS_HI) * POS_HI
        c[real, 3 + j] = pos[real] % POS_HI
    c[~real, 6] = 1.0
    return c


def _nsa_query_consts():
    qc = np.zeros((NSA_HEADS, EXTRA_W), np.float32)
    for hh in range(NSA_HEADS):
        qc[hh, 0:3] = qc[hh, 3:6] = _bf16_pieces(NSA_SLOPES[hh] * LOG2E)
        qc[hh, 6] = -MASK_BIG
    return jnp.asarray(qc, BF16)


def _nsa_key_consts(T, pad, onehot):
    n_sel = T // NSA_SEL_BLOCK
    pos = np.arange(T + pad)
    real = pos < T
    kc = _pos_cols(pos, real)
    if onehot:
        oh = np.zeros((T + pad, n_sel), np.float32)
        oh[pos[real], pos[real] // NSA_SEL_BLOCK] = 1.0
        kc = np.concatenate([kc, oh], axis=1)
    vc = np.zeros((T + pad, NSA_DH), np.float32)
    vc[real, 0] = 1.0
    return jnp.asarray(kc, BF16), jnp.asarray(vc, BF16)


def _nsa_kprep_kernel(kv_ref, kc_ref, vc_ref, ka_ref, va_ref):
    real = pl.program_id(1) < pl.num_programs(1) - 1
    kv = (kv_ref[...] * jnp.where(real, 1.0, 0.0)).astype(BF16)
    for g in range(NSA_GROUPS):
        ka_ref[g] = jnp.concatenate([kv[:, g * NSA_DH:(g + 1) * NSA_DH], kc_ref[...]], axis=-1)
        va_ref[g] = jnp.concatenate([kv[:, NSA_KVW + g * NSA_DH:NSA_KVW + (g + 1) * NSA_DH], vc_ref[...]], axis=-1)


def nsa_kprep(proj3, kc, vc, tb, cs):
    bsz, T, _ = proj3.shape
    nt = T // tb
    wk = NSA_DH + kc.shape[1]
    return pl.pallas_call(
        _nsa_kprep_kernel,
        grid=(bsz, nt + 1),
        in_specs=[pl.BlockSpec((None, tb, KV_W), lambda b, t: (b, jnp.minimum(t, nt - 1), cs)),
                  pl.BlockSpec((tb, kc.shape[1]), lambda b, t: (t, 0)),
                  pl.BlockSpec((tb, NSA_DH), lambda b, t: (t, 0))],
        out_specs=[pl.BlockSpec((None, NSA_GROUPS, tb, wk), lambda b, t: (b, 0, t, 0)),
                   pl.BlockSpec((None, NSA_GROUPS, tb, 2 * NSA_DH), lambda b, t: (b, 0, t, 0))],
        out_shape=[jax.ShapeDtypeStruct((bsz, NSA_GROUPS, T + tb, wk), BF16),
                   jax.ShapeDtypeStruct((bsz, NSA_GROUPS, T + tb, 2 * NSA_DH), BF16)],
        compiler_params=_cparams(("parallel", "parallel")),
        name="nsa_kprep",
    )(proj3, kc, vc)


def _nsa_select_kernel(ids_ref, cnt_ref, q_ref, ka_ref, va_ref, sel_ref, qc_ref, sm_ref, prev_ref, o_ref,
                       qa_ref, kt_ref, vt_ref, m_ref, acc_ref, *, tq, n_sel, lmax):
    b = pl.program_id(0)
    i = pl.program_id(1)
    B = NSA_SEL_BLOCK
    kpos = i * tq + lax.broadcasted_iota(jnp.int32, (tq, 1), 0)
    tpos = i * tq + lax.broadcasted_iota(jnp.int32, (1, tq), 1)
    causal_bias = jnp.where(kpos <= tpos, 0.0, -BIG)
    hpc = NSA_HPG // NSA_CHAINS
    chains = [slice(c * hpc * tq, (c + 1) * hpc * tq) for c in range(NSA_CHAINS)]
    causal_bias = jnp.concatenate([causal_bias] * hpc, axis=1)
    gates = 1.0 / (1.0 + jnp.exp(-sm_ref[...]))
    own = pl.ds(pl.multiple_of(i * tq, tq), tq)
    for g in range(NSA_GROUPS):
        mcols = ((sel_ref[:, g * n_sel:(g + 1) * n_sel].astype(F32) - 1.0) * MASK_BIG).astype(BF16)
        _stack_queries(q_ref, qc_ref, qa_ref, g, tq, shared=mcols)
        v_own = va_ref[g, own, :].astype(F32).T.astype(BF16)
        for rs in chains:
            s = _dot_nt(ka_ref[g, own, :], qa_ref[rs, :]) + causal_bias
            m0 = jnp.max(s, axis=0, keepdims=True)
            m_ref[:, rs] = m0
            acc_ref[:, rs] = _dot(v_own, jnp.exp2(s - m0).astype(BF16))
        slot = (b * pl.num_programs(1) + i) * NSA_GROUPS + g

        def step(st, carry):
            for u in range(SEL_STEP):
                r0 = pl.multiple_of(ids_ref[slot * lmax + st * SEL_STEP + u] * B, B)
                kt_ref[u * B:(u + 1) * B, :] = ka_ref[g, pl.ds(r0, B), :]
                vt_ref[u * B:(u + 1) * B, :] = va_ref[g, pl.ds(r0, B), :]
            v_t = vt_ref[...].astype(F32).T.astype(BF16)
            for rs in chains:
                s = _dot_nt(kt_ref[...], qa_ref[rs, :])
                m_old = m_ref[:, rs]
                m_new = jnp.maximum(m_old, jnp.max(s, axis=0, keepdims=True))
                p = jnp.exp2(s - m_new).astype(BF16)
                acc_ref[:, rs] = jnp.exp2(m_old - m_new) * acc_ref[:, rs] + _dot(v_t, p)
                m_ref[:, rs] = m_new
            return carry

        lax.fori_loop(0, (cnt_ref[slot] + SEL_STEP - 1) // SEL_STEP, step, 0)
        acc = acc_ref[...]
        o = (acc[:NSA_DH, :] * (1.0 / acc[NSA_DH:NSA_DH + 1, :])).T
        for h in range(NSA_HPG):
            hh = g * NSA_HPG + h
            gate = gates[:, SM_NG + NSA_HEADS + hh:SM_NG + NSA_HEADS + hh + 1]
            o_ref[:, _head_cols(hh)] = (prev_ref[:, _head_cols(hh)] + gate * o[h * tq:(h + 1) * tq, :]
                                        ).astype(o_ref.dtype)


def nsa_select(proj3, sel, act, prev, tq=256):
    bsz, T, _ = proj3.shape
    tq = min(tq, T)
    nt = T // tq
    n_sel = T // NSA_SEL_BLOCK
    lmax = -(-n_sel // SEL_STEP) * SEL_STEP
    tp = min(KPREP_ROWS, T)
    kc, vc = _nsa_key_consts(T, tp, onehot=True)
    qc = _nsa_query_consts()
    kaug, vaug = nsa_kprep(proj3, kc, vc, tp, (OFF_NKV + KV_W) // KV_W)
    wk = kaug.shape[-1]
    on = (act.reshape(bsz, nt, NSA_GROUPS, n_sel) > 0.0).astype(jnp.int32)
    seen = jnp.cumsum(on, axis=-1)
    ids = jnp.sum(seen[..., None, :] <= jnp.arange(lmax, dtype=jnp.int32)[:, None], axis=-1, dtype=jnp.int32)
    ids = ids.reshape(-1)
    cnt = seen[..., -1].reshape(-1)
    rows = NSA_HPG * tq
    grid_spec = pltpu.PrefetchScalarGridSpec(
        num_scalar_prefetch=2,
        grid=(bsz, nt),
        in_specs=[pl.BlockSpec((None, tq, NSA_QW), lambda b, i, *_: (b, i, OFF_NQ // NSA_QW)),
                  pl.BlockSpec((None, NSA_GROUPS, T + tp, wk), lambda b, i, *_: (b, 0, 0, 0)),
                  pl.BlockSpec((None, NSA_GROUPS, T + tp, 2 * NSA_DH), lambda b, i, *_: (b, 0, 0, 0)),
                  pl.BlockSpec((None, tq, NSA_GROUPS * n_sel), lambda b, i, *_: (b, i, 0)),
                  pl.BlockSpec((NSA_HEADS, EXTRA_W), lambda b, i, *_: (0, 0)),
                  pl.BlockSpec((None, tq, LANES), lambda b, i, *_: (b, i, OFF_SMALL // LANES)),
                  pl.BlockSpec((None, tq, NSA_QW), lambda b, i, *_: (b, i, 0))],
        out_specs=pl.BlockSpec((None, tq, NSA_QW), lambda b, i, *_: (b, i, 0)),
        scratch_shapes=[pltpu.VMEM((rows, wk), BF16),
                        pltpu.VMEM((SEL_STEP * NSA_SEL_BLOCK, wk), BF16),
                        pltpu.VMEM((SEL_STEP * NSA_SEL_BLOCK, 2 * NSA_DH), BF16),
                        pltpu.VMEM((1, rows), F32), pltpu.VMEM((2 * NSA_DH, rows), F32)])
    return pl.pallas_call(
        functools.partial(_nsa_select_kernel, tq=tq, n_sel=n_sel, lmax=lmax),
        grid_spec=grid_spec,
        out_shape=jax.ShapeDtypeStruct((bsz, T, NSA_QW), BF16),
        compiler_params=_cparams(("parallel", "arbitrary")),
        name="nsa_select",
    )(ids, cnt, proj3, kaug, vaug, sel, qc, proj3, prev)


def nsa_mixer(proj3, cmp_pos, cmp_w1, cmp_w2):
    kcv = nsa_compress(proj3, cmp_pos, cmp_w1, cmp_w2)
    o_cmp, sel, act = nsa_cmp_topk(proj3, kcv)
    o_cw = nsa_window(proj3, o_cmp)
    return nsa_select(proj3, sel, act, o_cw)


def _pack_w_in(w_in):
    w_in = w_in.astype(BF16)
    (g_q, g_k, g_v, g_r, g_a, n_q, n_kv, n_g, s_z, s_xbc, s_dt, m_g) = jnp.split(w_in, SPLIT_POINTS, axis=-1)
    pad = jnp.zeros(w_in.shape[:-1] + (D_PK - OFF_SMALL - GLA_RANK - 3 * NSA_HEADS - SSM_HEADS,), w_in.dtype)
    return jnp.concatenate([m_g, g_q, g_k, g_v, g_r, n_q, s_z, s_xbc, n_kv, g_a, n_g, s_dt, pad], axis=-1)


def kernel(x, w_in, gla_a2, gla_a_bias, gla_norm, nsa_cmp_pos, nsa_cmp_w1, nsa_cmp_w2, ssm_conv_w, ssm_conv_b,
           ssm_dt_bias, ssm_a_log, ssm_d, ssm_norm, w_branch, w_out, norm_pre_mix, norm_post_mix, norm_pre_ffn,
           norm_post_ffn, w_ffn_gate, w_ffn_up, w_ffn_down):
    bsz, T, D = x.shape
    depth = w_in.shape[0]
    n = bsz * T
    w_in_pk = _pack_w_in(w_in)
    wa_pad = jnp.zeros((depth, LANES, GLA_KW), F32).at[:, SM_GA:SM_GA + GLA_RANK].set(gla_a2).astype(BF16)
    w_branch_b = w_branch.astype(BF16)
    w_out_b = w_out.astype(BF16)
    w_gate_b = w_ffn_gate.astype(BF16)
    w_up_b = w_ffn_up.astype(BF16)
    w_down_b = w_ffn_down.astype(BF16)
    xf = x.reshape(n, D)
    for l in range(depth):
        proj = norm_matmul(xf, norm_pre_mix[l], w_in_pk[l])
        proj3 = proj.reshape(bsz, T, D_PK)
        y_gla = gla_mixer(proj3, wa_pad[l], gla_a_bias[l], gla_norm[l])
        y_nsa = nsa_mixer(proj3, nsa_cmp_pos[l], nsa_cmp_w1[l], nsa_cmp_w2[l])
        y_ssm = ssd_mixer(proj3, ssm_conv_w[l], ssm_conv_b[l], ssm_dt_bias[l], ssm_a_log[l], ssm_d[l], ssm_norm[l])
        merged = merge_branches(y_gla.reshape(n, BRANCH_W), y_nsa.reshape(n, BRANCH_W), y_ssm.reshape(n, BRANCH_W),
                                w_branch_b[l], proj)
        xf = proj_norm_residual(merged, w_out_b[l], xf, norm_post_mix[l], tm=512, tk=D)
        act = ffn_up(xf, norm_pre_ffn[l], w_gate_b[l], w_up_b[l])
        xf = proj_norm_residual(act, w_down_b[l], xf, norm_post_ffn[l], tm=1024, tk=512)
    return xf.reshape(bsz, T, D)
```

```python
import functools

import numpy as np
import jax
import jax.numpy as jnp
from jax import lax
from jax.experimental import pallas as pl
from jax.experimental.pallas import tpu as pltpu

F32 = jnp.float32
BF16 = jnp.bfloat16

D_MODEL = 2048
EPS = 1e-6
N_BRANCH = 3
BRANCH_W = 1024
GLA_HEADS, GLA_DK, GLA_DV, GLA_RANK, GLA_TAU, GLA_CHUNK = 4, 256, 256, 16, 16.0, 64
GLA_KW = GLA_HEADS * GLA_DK
GLA_VW = GLA_HEADS * GLA_DV
NSA_HEADS, NSA_GROUPS, NSA_DH = 16, 2, 64
NSA_HPG = NSA_HEADS // NSA_GROUPS
NSA_QW = NSA_HEADS * NSA_DH
NSA_KVW = NSA_GROUPS * NSA_DH
NSA_CMP_LEN, NSA_CMP_STRIDE, NSA_SEL_BLOCK, NSA_TOPN, NSA_WINDOW = 32, 16, 64, 16, 512
BIG = 1e30
SSM_HEADS, SSM_HEADDIM, SSM_GROUPS, SSM_STATE, SSM_CONV, SSM_CHUNK = 16, 64, 4, 128, 4, 64
SSM_INNER = SSM_HEADS * SSM_HEADDIM
SSM_HPG = SSM_HEADS // SSM_GROUPS
SSM_CONV_CH = SSM_INNER + 2 * SSM_GROUPS * SSM_STATE
D_FF = ((8 * D_MODEL // 3 + 255) // 256) * 256
IN_SIZES = (GLA_KW, GLA_KW, GLA_VW, GLA_VW, GLA_RANK, NSA_QW, 6 * NSA_KVW, 3 * NSA_HEADS,
            SSM_INNER, SSM_CONV_CH, SSM_HEADS, N_BRANCH * D_MODEL)
SPLIT_POINTS = tuple(int(v) for v in np.cumsum(IN_SIZES)[:-1])

LANES = 128
VMEM_LIMIT = 56 * 1024 * 1024

D_GATES = N_BRANCH * D_MODEL
OFF_GQ = 0
OFF_GK = OFF_GQ + GLA_KW
OFF_GV = OFF_GK + GLA_KW
OFF_GR = OFF_GV + GLA_VW
OFF_NQ = OFF_GR + GLA_VW
OFF_SZ = OFF_NQ + NSA_QW
OFF_SX = OFF_SZ + SSM_INNER
OFF_SB = OFF_SX + SSM_INNER
OFF_SC = OFF_SB + SSM_GROUPS * SSM_STATE
OFF_NKV = OFF_SC + SSM_GROUPS * SSM_STATE
OFF_SMALL = OFF_NKV + 6 * NSA_KVW
D_PK = OFF_SMALL + 2 * LANES
SM_GA = 0
SM_NG = 16
SM_DT = 64


def _cparams(sem):
    return pltpu.CompilerParams(dimension_semantics=sem, vmem_limit_bytes=VMEM_LIMIT)


def _split3(x):
    hi = x.astype(BF16)
    r1 = x - hi.astype(F32)
    mid = r1.astype(BF16)
    lo = (r1 - mid.astype(F32)).astype(BF16)
    return hi, mid, lo


def _dot(a, b):
    return jnp.dot(a, b, preferred_element_type=F32)


def _dot_nt(a, b):
    return lax.dot_general(a, b, (((1,), (1,)), ((), ())), preferred_element_type=F32)


def _dot01_left(m01, x):
    hi, mid, lo = _split3(x)
    return _dot(m01, hi) + _dot(m01, mid) + _dot(m01, lo)


def _dot01_right(x, m01):
    hi, mid, lo = _split3(x)
    return _dot(hi, m01) + _dot(mid, m01) + _dot(lo, m01)


def _silu(x):
    return x / (1.0 + jnp.exp(-x))


def _norm_rows(x_ref, g_ref, h_ref):
    @pl.when(pl.program_id(1) == 0)
    def _():
        x = x_ref[...]
        y = x * lax.rsqrt(jnp.mean(x * x, axis=-1, keepdims=True) + EPS)
        h_ref[...] = (y * g_ref[...]).astype(h_ref.dtype)


def _norm_mm_kernel(x_ref, g_ref, w_ref, og_ref, o_ref, h_ref, *, n_gate_tiles):
    _norm_rows(x_ref, g_ref, h_ref)
    j = pl.program_id(1)

    @pl.when(j < n_gate_tiles)
    def _():
        og_ref[...] = _dot(h_ref[...], w_ref[...]).astype(og_ref.dtype)

    @pl.when(j >= n_gate_tiles)
    def _():
        o_ref[...] = _dot(h_ref[...], w_ref[...])


def norm_matmul(x, g, w, n_gates, tm=1024, tn=1024):
    m, k = x.shape
    n = w.shape[1]
    tm = min(tm, m)
    ng = n_gates // tn
    return pl.pallas_call(
        functools.partial(_norm_mm_kernel, n_gate_tiles=ng),
        grid=(m // tm, n // tn),
        in_specs=[pl.BlockSpec((tm, k), lambda i, j: (i, 0)), pl.BlockSpec((1, k), lambda i, j: (0, 0)),
                  pl.BlockSpec((k, tn), lambda i, j: (0, j))],
        out_specs=[pl.BlockSpec((tm, tn), lambda i, j: (i, jnp.minimum(j, ng - 1))),
                   pl.BlockSpec((tm, tn), lambda i, j: (i, jnp.maximum(j - ng, 0)))],
        out_shape=[jax.ShapeDtypeStruct((m, n_gates), BF16), jax.ShapeDtypeStruct((m, n - n_gates), F32)],
        scratch_shapes=[pltpu.VMEM((tm, k), BF16)],
        compiler_params=_cparams(("parallel", "arbitrary")),
        name="in_proj",
    )(x, g.reshape(1, k), w)


def _merge_kernel(yg_ref, yn_ref, ys_ref, wg_ref, wn_ref, ws_ref, g0_ref, g1_ref, g2_ref, o_ref):
    def gate(ref):
        return 1.0 / (1.0 + jnp.exp(-ref[...].astype(F32)))

    acc = gate(g0_ref) * _dot(yg_ref[...], wg_ref[...])
    acc += gate(g1_ref) * _dot(yn_ref[...], wn_ref[...])
    acc += gate(g2_ref) * _dot(ys_ref[...], ws_ref[...])
    o_ref[...] = acc.astype(o_ref.dtype)


def merge_branches(y_gla, y_nsa, y_ssm, w_branch, gates, tm=1024, tn=512):
    m = y_gla.shape[0]
    d = w_branch.shape[-1]
    tm = min(tm, m)
    nj = d // tn
    ys = pl.BlockSpec((tm, BRANCH_W), lambda i, j: (i, 0))

    def wspec(b):
        return pl.BlockSpec((None, BRANCH_W, tn), lambda i, j, b=b: (b, 0, j))

    def gspec(b):
        return pl.BlockSpec((tm, tn), lambda i, j, b=b: (i, b * D_MODEL // tn + j))

    return pl.pallas_call(
        _merge_kernel,
        grid=(m // tm, nj),
        in_specs=[ys, ys, ys, wspec(0), wspec(1), wspec(2), gspec(0), gspec(1), gspec(2)],
        out_specs=pl.BlockSpec((tm, tn), lambda i, j: (i, j)),
        out_shape=jax.ShapeDtypeStruct((m, d), BF16),
        compiler_params=_cparams(("parallel", "parallel")),
        name="merge",
    )(y_gla, y_nsa, y_ssm, w_branch, w_branch, w_branch, gates, gates, gates)


def _proj_norm_res_kernel(a_ref, w_ref, x_ref, g_ref, o_ref, acc_ref):
    k = pl.program_id(1)

    @pl.when(k == 0)
    def _():
        acc_ref[...] = jnp.zeros_like(acc_ref)

    acc_ref[...] += _dot(a_ref[...], w_ref[...])

    @pl.when(k == pl.num_programs(1) - 1)
    def _():
        f = acc_ref[...]
        y = f * lax.rsqrt(jnp.mean(f * f, axis=-1, keepdims=True) + EPS)
        o_ref[...] = x_ref[...] + y * g_ref[...]


def proj_norm_residual(a, w, x, g, tm=1024, tk=512):
    m, kk = a.shape
    d = w.shape[1]
    tm = min(tm, m)
    return pl.pallas_call(
        _proj_norm_res_kernel,
        grid=(m // tm, kk // tk),
        in_specs=[pl.BlockSpec((tm, tk), lambda i, k: (i, k)),
                  pl.BlockSpec((tk, d), lambda i, k: (k, 0)),
                  pl.BlockSpec((tm, d), lambda i, k: (i, 0)),
                  pl.BlockSpec((1, d), lambda i, k: (0, 0))],
        out_specs=pl.BlockSpec((tm, d), lambda i, k: (i, 0)),
        out_shape=jax.ShapeDtypeStruct((m, d), F32),
        scratch_shapes=[pltpu.VMEM((tm, d), F32)],
        compiler_params=_cparams(("parallel", "arbitrary")),
        name="proj_norm_res",
    )(a, w, x, g.reshape(1, d))


def _ffn_up_kernel(x_ref, g_ref, wg_ref, wu_ref, o_ref, h_ref):
    _norm_rows(x_ref, g_ref, h_ref)
    h = h_ref[...]
    a = _dot(h, wg_ref[...])
    u = _dot(h, wu_ref[...])
    o_ref[...] = (_silu(a) * u).astype(o_ref.dtype)


def ffn_up(x, g, wg, wu, tm=1024, tn=512):
    m, k = x.shape
    n = wg.shape[1]
    tm = min(tm, m)
    return pl.pallas_call(
        _ffn_up_kernel,
        grid=(m // tm, n // tn),
        in_specs=[pl.BlockSpec((tm, k), lambda i, j: (i, 0)),
                  pl.BlockSpec((1, k), lambda i, j: (0, 0)),
                  pl.BlockSpec((k, tn), lambda i, j: (0, j)),
                  pl.BlockSpec((k, tn), lambda i, j: (0, j))],
        out_specs=pl.BlockSpec((tm, tn), lambda i, j: (i, j)),
        out_shape=jax.ShapeDtypeStruct((m, n), BF16),
        scratch_shapes=[pltpu.VMEM((tm, k), BF16)],
        compiler_params=_cparams(("parallel", "arbitrary")),
        name="ffn_up",
    )(x, g.reshape(1, k), wg, wu)


def _gla_kernel(q_ref, k_ref, v_ref, r_ref, sm_ref, wa_ref, ba_ref, ng_ref, tri_ref, same_ref, o_ref,
                st_ref, *, tb):
    C = GLA_CHUNK

    @pl.when(pl.program_id(1) == 0)
    def _():
        st_ref[...] = jnp.zeros_like(st_ref)

    pre = _dot(sm_ref[...].astype(BF16), wa_ref[...]) + ba_ref[...]
    la = (jnp.minimum(pre, 0.0) - jnp.log1p(jnp.exp(-jnp.abs(pre)))) * (1.0 / GLA_TAU)
    tri = tri_ref[...]
    same = same_ref[...]
    ri = lax.broadcasted_iota(jnp.int32, (tb, tb), 0)
    ci = lax.broadcasted_iota(jnp.int32, (tb, tb), 1)
    causal = (ri >= ci) & (ri // C == ci // C)
    for h in range(GLA_HEADS):
        ck = slice(h * GLA_DK, (h + 1) * GLA_DK)
        cv = slice(h * GLA_DV, (h + 1) * GLA_DV)
        hi, mid, lo = _split3(la[:, ck])
        bcum = _dot(tri, hi) + _dot(tri, mid) + _dot(tri, lo)
        b_last = _dot(same, hi) + _dot(same, mid) + _dot(same, lo)
        q = q_ref[:, ck] * (GLA_DK ** -0.5)
        k = k_ref[:, ck]
        v = v_ref[:, cv]
        q_dec = (q * jnp.exp(bcum)).astype(BF16)
        k_inv = (k * jnp.exp(-bcum)).astype(BF16)
        k_end = (k * jnp.exp(b_last - bcum)).astype(BF16)
        att = jnp.where(causal, _dot_nt(q_dec, k_inv), 0.0)
        o_intra = _dot(att.astype(BF16), v.astype(BF16))
        st = st_ref[h]
        outs = []
        for c in range(tb // C):
            rows = slice(c * C, (c + 1) * C)
            outs.append(o_intra[rows] + _dot_nt(q_dec[rows], st.astype(BF16)))
            st = jnp.exp(b_last[c * C:c * C + 1, :]) * st + _dot(v[rows].T.astype(BF16), k_end[rows])
        st_ref[h] = st
        o = jnp.concatenate(outs, axis=0)
        y = o * lax.rsqrt(jnp.mean(o * o, axis=-1, keepdims=True) + EPS) * ng_ref[...]
        o_ref[:, cv] = (y * _silu(r_ref[:, cv])).astype(o_ref.dtype)


def _chunk_masks(tb, chunk):
    idx = np.arange(tb)
    same = (idx[:, None] // chunk) == (idx[None, :] // chunk)
    return jnp.asarray(same & (idx[:, None] >= idx[None, :]), BF16), jnp.asarray(same, BF16)


def gla_mixer(proj3, wa_pad, ba, norm_g, tb=256):
    bsz, T, _ = proj3.shape
    tb = min(tb, T)
    tri, same = _chunk_masks(tb, GLA_CHUNK)

    def col(off, w):
        return pl.BlockSpec((None, tb, w), lambda b, t, o=off // w: (b, t, o))

    def full(shape):
        return pl.BlockSpec(shape, lambda b, t: (0,) * len(shape))

    return pl.pallas_call(
        functools.partial(_gla_kernel, tb=tb),
        grid=(bsz, T // tb),
        in_specs=[col(OFF_GQ, GLA_KW), col(OFF_GK, GLA_KW), col(OFF_GV, GLA_VW), col(OFF_GR, GLA_VW),
                  col(OFF_SMALL, LANES),
                  full((LANES, GLA_KW)), full((1, GLA_KW)), full((1, GLA_DV)), full((tb, tb)), full((tb, tb))],
        out_specs=pl.BlockSpec((None, tb, GLA_VW), lambda b, t: (b, t, 0)),
        out_shape=jax.ShapeDtypeStruct((bsz, T, GLA_VW), BF16),
        scratch_shapes=[pltpu.VMEM((GLA_HEADS, GLA_DV, GLA_DK), F32)],
        compiler_params=_cparams(("parallel", "arbitrary")),
        name="gla",
    )(proj3, proj3, proj3, proj3, proj3, wa_pad, ba.reshape(1, GLA_KW), norm_g.reshape(1, GLA_DV), tri, same)


def _ssd_kernel(z_ref, x_ref, bm_ref, cm_ref, sm_ref, cwx_ref, cwb_ref, cwc_ref, cbx_ref, cbb_ref, cbc_ref,
                dtb_ref, alog_ref, dskip_ref, ng_ref, tri_ref, exp_ref, o_ref,
                st_ref, extx_ref, extb_ref, extc_ref, xa_ref, ba_ref, ca_ref, dt_ref, a_ref, *, tb, nchunk):
    L = SSM_CHUNK
    GW = SSM_HPG * SSM_HEADDIM
    NS = SSM_STATE
    first = pl.program_id(1) == 0

    @pl.when(first)
    def _():
        st_ref[...] = jnp.zeros_like(st_ref)

    def conv_silu(src_ref, ext_ref, w_ref, b_ref, dst_ref):
        @pl.when(first)
        def _():
            ext_ref[0:8, :] = jnp.zeros((8, ext_ref.shape[1]), F32)

        @pl.when(jnp.logical_not(first))
        def _():
            ext_ref[0:8, :] = ext_ref[tb:tb + 8, :]

        ext_ref[8:8 + tb, :] = src_ref[...]
        acc = b_ref[...] + w_ref[SSM_CONV - 1:SSM_CONV, :] * ext_ref[8:8 + tb, :]
        for j in range(1, SSM_CONV):
            acc = acc + w_ref[SSM_CONV - 1 - j:SSM_CONV - j, :] * ext_ref[8 - j:8 - j + tb, :]
        dst_ref[...] = _silu(acc)

    conv_silu(x_ref, extx_ref, cwx_ref, cbx_ref, xa_ref)
    conv_silu(bm_ref, extb_ref, cwb_ref, cbb_ref, ba_ref)
    conv_silu(cm_ref, extc_ref, cwc_ref, cbc_ref, ca_ref)

    v = sm_ref[...] + dtb_ref[...]
    dt = jnp.maximum(v, 0.0) + jnp.log1p(jnp.exp(-jnp.abs(v)))
    dt_ref[...] = dt
    a_ref[...] = dt * (-jnp.exp(alog_ref[...]))

    tri = tri_ref[...]
    row = lax.broadcasted_iota(jnp.int32, (L, GW), 0)
    lane = lax.broadcasted_iota(jnp.int32, (L, GW), 1)
    lane_in = jnp.bitwise_and(lane, SSM_HEADDIM - 1)
    eye_t = (lane_in == row).astype(F32)
    tril_t = lane_in <= row
    bd_mask = (lax.broadcasted_iota(jnp.int32, (GW, GW), 0) // L
               == lax.broadcasted_iota(jnp.int32, (GW, GW), 1) // SSM_HEADDIM).astype(F32).astype(BF16)

    states = [st_ref[g] for g in range(SSM_GROUPS)]
    cum_parts = _split3(_dot01_left(tri, a_ref[...]))
    dt_parts = _split3(dt_ref[...])
    cum_all, dt_all = [], []
    for g in range(SSM_GROUPS):
        e_g = exp_ref[:, g * GW:(g + 1) * GW]
        cum_all.append(sum(_dot(p, e_g) for p in cum_parts))
        dt_all.append(sum(_dot(p, e_g) for p in dt_parts))
    for c in range(nchunk):
        rows = slice(c * L, (c + 1) * L)
        for g in range(SSM_GROUPS):
            cs = slice(g * GW, (g + 1) * GW)
            cum_e = cum_all[g][rows]
            dt_e = dt_all[g][rows]
            cum_last = cum_e[L - 1:L, :]
            r_row = jnp.sum(cum_e * eye_t, axis=0, keepdims=True)
            decay = jnp.exp(jnp.where(tril_t, cum_e - r_row, -jnp.inf))
            x_g = xa_ref[rows, cs]
            xdt = x_g * dt_e
            b_g = ba_ref[rows, g * NS:(g + 1) * NS]
            c_g = ca_ref[rows, g * NS:(g + 1) * NS].astype(BF16)
            bb = b_g.astype(BF16)
            cb_t = _dot_nt(c_g, jnp.concatenate([bb] * SSM_HPG, axis=0))
            xdt_bd = jnp.concatenate([xdt.astype(BF16)] * SSM_HPG, axis=0) * bd_mask
            y = _dot((cb_t * decay).astype(BF16), xdt_bd)
            st = states[g]
            y = y + _dot(c_g, st.astype(BF16)) * jnp.exp(cum_e)
            dend = jnp.exp(cum_last - cum_e)
            states[g] = jnp.exp(cum_last) * st + _dot(b_g.T.astype(BF16), (dend * xdt).astype(BF16))
            y = y + x_g * dskip_ref[:, cs]
            y = y * _silu(z_ref[rows, cs])
            y = y * lax.rsqrt(jnp.mean(y * y, axis=-1, keepdims=True) + EPS) * ng_ref[:, cs]
            o_ref[rows, cs] = y.astype(o_ref.dtype)
    for g in range(SSM_GROUPS):
        st_ref[g] = states[g]


def _ssd_expand():
    e = np.zeros((LANES, SSM_INNER), np.float32)
    for h in range(SSM_HEADS):
        e[SM_DT + h, h * SSM_HEADDIM:(h + 1) * SSM_HEADDIM] = 1.0
    return jnp.asarray(e, BF16)


def _small_row(v, off):
    return jnp.zeros((1, LANES), F32).at[0, off:off + v.shape[0]].set(v.astype(F32))


def ssd_mixer(proj3, conv_w, conv_b, dt_bias, a_log, d_skip, norm_g, tb=256):
    bsz, T, _ = proj3.shape
    tb = min(tb, T)
    nchunk = tb // SSM_CHUNK
    GN = SSM_GROUPS * SSM_STATE
    expand = _ssd_expand()
    tri, _ = _chunk_masks(tb, SSM_CHUNK)
    cwx, cwb, cwc = conv_w[:, :SSM_INNER], conv_w[:, SSM_INNER:SSM_INNER + GN], conv_w[:, SSM_INNER + GN:]
    cb2 = conv_b.reshape(1, SSM_CONV_CH)
    cbx, cbb, cbc = cb2[:, :SSM_INNER], cb2[:, SSM_INNER:SSM_INNER + GN], cb2[:, SSM_INNER + GN:]
    dtb = _small_row(dt_bias, SM_DT)
    alog = _small_row(a_log, SM_DT)
    dskip = jnp.repeat(d_skip.astype(F32), SSM_HEADDIM).reshape(1, SSM_INNER)

    def col(off, w):
        return pl.BlockSpec((None, tb, w), lambda b, t, o=off // w: (b, t, o))

    def full(shape):
        return pl.BlockSpec(shape, lambda b, t: (0,) * len(shape))

    return pl.pallas_call(
        functools.partial(_ssd_kernel, tb=tb, nchunk=nchunk),
        grid=(bsz, T // tb),
        in_specs=[col(OFF_SZ, SSM_INNER), col(OFF_SX, SSM_INNER), col(OFF_SB, GN), col(OFF_SC, GN),
                  col(OFF_SMALL, LANES),
                  full((SSM_CONV, SSM_INNER)), full((SSM_CONV, GN)), full((SSM_CONV, GN)),
                  full((1, SSM_INNER)), full((1, GN)), full((1, GN)),
                  full((1, LANES)), full((1, LANES)), full((1, SSM_INNER)), full((1, SSM_INNER)),
                  full((tb, tb)), full((LANES, SSM_INNER))],
        out_specs=pl.BlockSpec((None, tb, SSM_INNER), lambda b, t: (b, t, 0)),
        out_shape=jax.ShapeDtypeStruct((bsz, T, SSM_INNER), BF16),
        scratch_shapes=[pltpu.VMEM((SSM_GROUPS, SSM_STATE, SSM_HPG * SSM_HEADDIM), F32),
                        pltpu.VMEM((tb + 8, SSM_INNER), F32), pltpu.VMEM((tb + 8, GN), F32),
                        pltpu.VMEM((tb + 8, GN), F32),
                        pltpu.VMEM((tb, SSM_INNER), F32), pltpu.VMEM((tb, GN), F32), pltpu.VMEM((tb, GN), F32),
                        pltpu.VMEM((tb, LANES), F32), pltpu.VMEM((tb, LANES), F32)],
        compiler_params=_cparams(("parallel", "arbitrary")),
        name="ssd",
    )(proj3, proj3, proj3, proj3, proj3, cwx, cwb, cwc, cbx, cbb, cbc, dtb, alog, dskip,
      norm_g.reshape(1, SSM_INNER), tri, expand)


NSA_SLOPES = tuple(float(np.float32(2.0 ** (-8.0 * (i + 1) / NSA_HEADS))) for i in range(NSA_HEADS))
LOG2E = float(np.log2(np.e))
NSA_SCALE = NSA_DH ** -0.5 * LOG2E
KV_W = 2 * NSA_KVW
NEG_HUGE = -3.0e38


def _head_cols(hh):
    return slice(hh * NSA_DH, (hh + 1) * NSA_DH)


def _nsa_compress_kernel(k_ref, v_ref, pos_ref, bd1_ref, bd2_ref, o_ref, *, n16):
    S = NSA_CMP_STRIDE
    top = jnp.zeros((n16, KV_W), F32)
    bot = jnp.zeros((n16, KV_W), F32)
    for l in range(S):
        rows = pl.ds(l, n16, stride=S)
        x = jnp.concatenate([k_ref[rows, :], v_ref[rows, :]], axis=-1)
        top += _dot((x + pos_ref[l:l + 1, :]).astype(BF16), bd1_ref[l])
        bot += _dot((x + pos_ref[S + l:S + l + 1, :]).astype(BF16), bd1_ref[S + l])
    pre = top + pltpu.roll(bot, n16 - 1, axis=0)
    out = _dot(_silu(pre).astype(BF16), bd2_ref[...])
    row = lax.broadcasted_iota(jnp.int32, (n16, KV_W), 0)
    o_ref[...] = jnp.where(row < n16 - 1, out, 0.0)


def nsa_compress(proj3, cmp_pos, cmp_w1, cmp_w2):
    bsz, T, _ = proj3.shape
    n16 = T // NSA_CMP_STRIDE
    sel = np.array([0, 0, 1, 1])
    eye = jnp.eye(4, dtype=F32)
    w1r = cmp_w1.reshape(2, NSA_CMP_LEN, NSA_DH, NSA_DH)[sel]
    bd1 = jnp.einsum('ab,alde->ladbe', eye, w1r).reshape(NSA_CMP_LEN, KV_W, KV_W).astype(BF16)
    bd2 = jnp.einsum('ab,ade->adbe', eye, cmp_w2[sel]).reshape(KV_W, KV_W).astype(BF16)
    pos = jnp.concatenate([cmp_pos[0], cmp_pos[0], cmp_pos[1], cmp_pos[1]], axis=-1)
    return pl.pallas_call(
        functools.partial(_nsa_compress_kernel, n16=n16),
        grid=(bsz,),
        in_specs=[pl.BlockSpec((None, T, NSA_KVW), lambda b: (b, 0, OFF_NKV // NSA_KVW)),
                  pl.BlockSpec((None, T, NSA_KVW), lambda b: (b, 0, OFF_NKV // NSA_KVW + 1)),
                  pl.BlockSpec((NSA_CMP_LEN, KV_W), lambda b: (0, 0)),
                  pl.BlockSpec((NSA_CMP_LEN, KV_W, KV_W), lambda b: (0, 0, 0)),
                  pl.BlockSpec((KV_W, KV_W), lambda b: (0, 0))],
        out_specs=pl.BlockSpec((None, n16, KV_W), lambda b: (b, 0, 0)),
        out_shape=jax.ShapeDtypeStruct((bsz, n16, KV_W), F32),
        compiler_params=_cparams(("parallel",)),
        name="nsa_compress",
    )(proj3, proj3, pos, bd1, bd2)


CMP_SEG_MAX = 320


def _cmp_segments(nc):
    bad = 2 * LANES
    if nc <= CMP_SEG_MAX + NSA_DH and nc != bad:
        return [(0, nc)]
    first = CMP_SEG_MAX if nc > CMP_SEG_MAX else nc - NSA_DH
    return [(0, first), (first, nc)]


def _stack_queries(q_ref, qc_ref, qa_ref, g, tq, shared=None):
    for h in range(NSA_HPG):
        hh = g * NSA_HPG + h
        qh = (q_ref[:, _head_cols(hh)] * NSA_SCALE).astype(BF16)
        qx = jnp.broadcast_to(qc_ref[hh:hh + 1, :], (tq, EXTRA_W))
        qa_ref[h * tq:(h + 1) * tq, 0:NSA_DH + EXTRA_W] = jnp.concatenate([qh, qx], axis=-1)
        if shared is not None:
            qa_ref[h * tq:(h + 1) * tq, NSA_DH + EXTRA_W:] = shared


def _nsa_cmp_topk_kernel(q_ref, kcv_ref, cpos_ref, qc_ref, sm_ref, ov_ref, o_ref, sel_ref, act_ref, qa_ref, imp_ref,
                         *, tq, n16, n_sel, n_top, cw):
    t0 = pl.program_id(1) * tq
    tpos = t0 + lax.broadcasted_iota(jnp.int32, (1, tq), 1)
    any_valid = (tpos >= NSA_CMP_LEN - 1).astype(F32)
    gates = 1.0 / (1.0 + jnp.exp(-sm_ref[...]))
    blk = lax.broadcasted_iota(jnp.int32, (n_sel, 1), 0)
    blk_f = blk.astype(F32)
    cur = tpos // NSA_SEL_BLOCK
    forced = (blk == 0) | (blk == cur) | (blk == cur - 1)
    future = blk * NSA_SEL_BLOCK > tpos
    blk_row = lax.broadcasted_iota(jnp.int32, (1, n_sel), 1)

    def attend(nc):
        segs = _cmp_segments(nc)
        bias = []
        for a, b in segs:
            cmp_end = (a + lax.broadcasted_iota(jnp.int32, (b - a, 1), 0)) * NSA_CMP_STRIDE + (NSA_CMP_LEN - 1)
            bias.append(jnp.where(cmp_end <= tpos, 0.0, -BIG))
        for g in range(NSA_GROUPS):
            kc = [jnp.concatenate([kcv_ref[a:b, g * NSA_DH:(g + 1) * NSA_DH].astype(BF16), cpos_ref[a:b, :]], axis=-1)
                  for a, b in segs]
            vo = [jnp.concatenate([kcv_ref[a:b, NSA_KVW + g * NSA_DH:NSA_KVW + (g + 1) * NSA_DH].T.astype(BF16),
                                   ov_ref[:, a:b]], axis=0) for a, b in segs]
            _stack_queries(q_ref, qc_ref, qa_ref, g, tq)
            imp = None
            for h in range(NSA_HPG):
                hh = g * NSA_HPG + h
                qa = qa_ref[h * tq:(h + 1) * tq, :]
                s = [_dot_nt(kc[i], qa) + bias[i] for i in range(len(segs))]
                m = functools.reduce(jnp.maximum, [jnp.max(x, axis=0, keepdims=True) for x in s])
                e = [jnp.exp2(x - m) for x in s]
                inv = any_valid / sum(jnp.sum(x, axis=0, keepdims=True) for x in e)
                r = sum(_dot(vo[i], e[i].astype(BF16)) for i in range(len(segs))) * inv
                imp = r[NSA_DH:] if imp is None else imp + r[NSA_DH:]
                o_ref[:, _head_cols(hh)] = gates[:, SM_NG + hh:SM_NG + hh + 1] * r[:NSA_DH].T
            imp_ref[g] = imp

    need = (t0 + tq) // NSA_CMP_STRIDE
    nchunks = n16 // cw
    for k in range(1, nchunks + 1):
        lo = (k - 1) * cw
        cond = (need > lo) & (need <= k * cw) if k < nchunks else need > lo
        pl.when(cond)(functools.partial(attend, k * cw))

    for g in range(NSA_GROUPS):
        work = jnp.where(forced, BIG, jnp.where(future, -BIG, imp_ref[g]))
        sel_t = jnp.zeros((n_sel, tq), F32)
        for _ in range(n_top):
            top = jnp.max(work, axis=0, keepdims=True)
            idx = jnp.min(jnp.where(work == top, blk_f, float(n_sel)), axis=0, keepdims=True)
            pick = blk_f == idx
            sel_t = jnp.where(pick, 1.0, sel_t)
            work = jnp.where(pick, NEG_HUGE, work)
        sel = sel_t.T
        sel_ref[:, g * n_sel:(g + 1) * n_sel] = sel.astype(sel_ref.dtype)
        union = jnp.max(sel, axis=0, keepdims=True)
        act_ref[:, g * n_sel:(g + 1) * n_sel] = jnp.where(blk_row * NSA_SEL_BLOCK < t0, union, 0.0)


def _nsa_overlap(n16, n_sel):
    n_cmp = n16 - 1
    tok = (np.arange(n_cmp) * NSA_CMP_STRIDE)[:, None] + np.arange(NSA_CMP_LEN)[None, :]
    ov = np.zeros((n16, n_sel), np.float32)
    np.add.at(ov, (np.repeat(np.arange(n_cmp), NSA_CMP_LEN), (tok // NSA_SEL_BLOCK).ravel()), 1.0 / NSA_CMP_LEN)
    return jnp.asarray(ov.T, BF16)


def nsa_cmp_topk(proj3, kcv, tq=256):
    bsz, T, _ = proj3.shape
    tq = min(tq, T)
    n16 = T // NSA_CMP_STRIDE
    n_sel = T // NSA_SEL_BLOCK
    n_top = min(NSA_TOPN, n_sel)
    cmp_end = np.arange(n16) * NSA_CMP_STRIDE + NSA_CMP_LEN - 1
    cpos = jnp.asarray(_pos_cols(cmp_end, np.ones(n16, bool)), BF16)
    return pl.pallas_call(
        functools.partial(_nsa_cmp_topk_kernel, tq=tq, n16=n16, n_sel=n_sel, n_top=n_top, cw=min(LANES // 2, n16)),
        grid=(bsz, T // tq),
        in_specs=[pl.BlockSpec((None, tq, NSA_QW), lambda b, i: (b, i, OFF_NQ // NSA_QW)),
                  pl.BlockSpec((None, n16, KV_W), lambda b, i: (b, 0, 0)),
                  pl.BlockSpec((n16, EXTRA_W), lambda b, i: (0, 0)),
                  pl.BlockSpec((NSA_HEADS, EXTRA_W), lambda b, i: (0, 0)),
                  pl.BlockSpec((None, tq, LANES), lambda b, i: (b, i, OFF_SMALL // LANES)),
                  pl.BlockSpec((n_sel, n16), lambda b, i: (0, 0))],
        out_specs=[pl.BlockSpec((None, tq, NSA_QW), lambda b, i: (b, i, 0)),
                   pl.BlockSpec((None, tq, NSA_GROUPS * n_sel), lambda b, i: (b, i, 0)),
                   pl.BlockSpec((None, None, 1, NSA_GROUPS * n_sel), lambda b, i: (b, i, 0, 0))],
        out_shape=[jax.ShapeDtypeStruct((bsz, T, NSA_QW), F32),
                   jax.ShapeDtypeStruct((bsz, T, NSA_GROUPS * n_sel), BF16),
                   jax.ShapeDtypeStruct((bsz, T // tq, 1, NSA_GROUPS * n_sel), F32)],
        scratch_shapes=[pltpu.VMEM((NSA_HPG * tq, NSA_DH + EXTRA_W), BF16),
                        pltpu.VMEM((NSA_GROUPS, n_sel, tq), F32)],
        compiler_params=_cparams(("parallel", "parallel")),
        name="nsa_cmp_topk",
    )(proj3, kcv, cpos, _nsa_query_consts(), proj3, _nsa_overlap(n16, n_sel))


def _nsa_window_kernel(q_ref, k0_ref, k1_ref, k2_ref, v0_ref, v1_ref, v2_ref, qc_ref, sm_ref, prev_ref, o_ref,
                       qa_ref, *, tq):
    nb = NSA_WINDOW // tq + 1
    i = pl.program_id(1)
    tpos = i * tq + lax.broadcasted_iota(jnp.int32, (1, tq), 1)
    kpos = (i - (nb - 1)) * tq + lax.broadcasted_iota(jnp.int32, (nb * tq, 1), 0)
    d = tpos - kpos
    valid = (d >= 0) & (d < NSA_WINDOW) & (kpos >= 0)
    hpc = NSA_HPG // WIN_CHAINS
    bias = jnp.concatenate([jnp.where(valid, 0.0, -BIG)] * hpc, axis=1)
    gates = 1.0 / (1.0 + jnp.exp(-sm_ref[...]))
    for g in range(NSA_GROUPS):
        kw = jnp.concatenate([k0_ref[g], k1_ref[g], k2_ref[g]], axis=0)
        vw = jnp.concatenate([v0_ref[g], v1_ref[g], v2_ref[g]], axis=0)
        vw_t = vw.astype(F32).T.astype(BF16)
        _stack_queries(q_ref, qc_ref, qa_ref, g, tq)
        for c in range(WIN_CHAINS):
            s = _dot_nt(kw, qa_ref[c * hpc * tq:(c + 1) * hpc * tq, :]) + bias
            e = jnp.exp2(s - jnp.max(s, axis=0, keepdims=True))
            acc = _dot(vw_t, e.astype(BF16))
            o = (acc[:NSA_DH, :] * (1.0 / acc[NSA_DH:NSA_DH + 1, :])).T
            for h in range(hpc):
                hh = g * NSA_HPG + c * hpc + h
                gate = gates[:, SM_NG + 2 * NSA_HEADS + hh:SM_NG + 2 * NSA_HEADS + hh + 1]
                o_ref[:, _head_cols(hh)] = prev_ref[:, _head_cols(hh)] + gate * o[h * tq:(h + 1) * tq]


def nsa_window(proj3, prev, tq=256):
    bsz, T, _ = proj3.shape
    tq = min(tq, T)
    assert NSA_WINDOW % tq == 0 and NSA_WINDOW // tq == 2
    tp = min(KPREP_ROWS, T)
    kc, vc = _nsa_key_consts(T, tp, onehot=False)
    kaug, vaug = nsa_kprep(proj3, kc, vc, tp, (OFF_NKV + 2 * KV_W) // KV_W)
    wk = kaug.shape[-1]

    def kvspec(back, w):
        return pl.BlockSpec((None, NSA_GROUPS, tq, w), lambda b, i, back=back: (b, 0, jnp.maximum(i - back, 0), 0))

    return pl.pallas_call(
        functools.partial(_nsa_window_kernel, tq=tq),
        grid=(bsz, T // tq),
        in_specs=[pl.BlockSpec((None, tq, NSA_QW), lambda b, i: (b, i, OFF_NQ // NSA_QW)),
                  kvspec(2, wk), kvspec(1, wk), kvspec(0, wk),
                  kvspec(2, 2 * NSA_DH), kvspec(1, 2 * NSA_DH), kvspec(0, 2 * NSA_DH),
                  pl.BlockSpec((NSA_HEADS, EXTRA_W), lambda b, i: (0, 0)),
                  pl.BlockSpec((None, tq, LANES), lambda b, i: (b, i, OFF_SMALL // LANES)),
                  pl.BlockSpec((None, tq, NSA_QW), lambda b, i: (b, i, 0))],
        out_specs=pl.BlockSpec((None, tq, NSA_QW), lambda b, i: (b, i, 0)),
        out_shape=jax.ShapeDtypeStruct((bsz, T, NSA_QW), F32),
        scratch_shapes=[pltpu.VMEM((NSA_HPG * tq, wk), BF16)],
        compiler_params=_cparams(("parallel", "parallel")),
        name="nsa_window",
    )(proj3, kaug, kaug, kaug, vaug, vaug, vaug, _nsa_query_consts(), proj3, prev)


MASK_BIG = 1e30
SEL_STEP = 8
KPREP_ROWS = 1024
NSA_CHAINS = 1
WIN_CHAINS = 2
POS_HI = 128
EXTRA_W = NSA_DH


def _bf16_pieces(x):
    x = np.float32(x)
    out = []
    for _ in range(3):
        p = np.float32(np.asarray(x, dtype=jnp.bfloat16))
        out.append(p)
        x = np.float32(x - p)
    return out


def _pos_cols(pos, real):
    c = np.zeros((len(pos), EXTRA_W), np.float32)
    for j in range(3):
        c[real, j] = (pos[real] // POS_HI) * POS_HI
        c[real, 3 + j] = pos[real] % POS_HI
    c[~real, 6] = 1.0
    return c


def _nsa_query_consts():
    qc = np.zeros((NSA_HEADS, EXTRA_W), np.float32)
    for hh in range(NSA_HEADS):
        qc[hh, 0:3] = qc[hh, 3:6] = _bf16_pieces(NSA_SLOPES[hh] * LOG2E)
        qc[hh, 6] = -MASK_BIG
    return jnp.asarray(qc, BF16)


def _nsa_key_consts(T, pad, onehot):
    n_sel = T // NSA_SEL_BLOCK
    pos = np.arange(T + pad)
    real = pos < T
    kc = _pos_cols(pos, real)
    if onehot:
        oh = np.zeros((T + pad, n_sel), np.float32)
        oh[pos[real], pos[real] // NSA_SEL_BLOCK] = 1.0
        kc = np.concatenate([kc, oh], axis=1)
    vc = np.zeros((T + pad, NSA_DH), np.float32)
    vc[real, 0] = 1.0
    return jnp.asarray(kc, BF16), jnp.asarray(vc, BF16)


def _nsa_kprep_kernel(kv_ref, kc_ref, vc_ref, ka_ref, va_ref):
    real = pl.program_id(1) < pl.num_programs(1) - 1
    kv = (kv_ref[...] * jnp.where(real, 1.0, 0.0)).astype(BF16)
    for g in range(NSA_GROUPS):
        ka_ref[g] = jnp.concatenate([kv[:, g * NSA_DH:(g + 1) * NSA_DH], kc_ref[...]], axis=-1)
        va_ref[g] = jnp.concatenate([kv[:, NSA_KVW + g * NSA_DH:NSA_KVW + (g + 1) * NSA_DH], vc_ref[...]], axis=-1)


def nsa_kprep(proj3, kc, vc, tb, cs):
    bsz, T, _ = proj3.shape
    nt = T // tb
    wk = NSA_DH + kc.shape[1]
    return pl.pallas_call(
        _nsa_kprep_kernel,
        grid=(bsz, nt + 1),
        in_specs=[pl.BlockSpec((None, tb, KV_W), lambda b, t: (b, jnp.minimum(t, nt - 1), cs)),
                  pl.BlockSpec((tb, kc.shape[1]), lambda b, t: (t, 0)),
                  pl.BlockSpec((tb, NSA_DH), lambda b, t: (t, 0))],
        out_specs=[pl.BlockSpec((None, NSA_GROUPS, tb, wk), lambda b, t: (b, 0, t, 0)),
                   pl.BlockSpec((None, NSA_GROUPS, tb, 2 * NSA_DH), lambda b, t: (b, 0, t, 0))],
        out_shape=[jax.ShapeDtypeStruct((bsz, NSA_GROUPS, T + tb, wk), BF16),
                   jax.ShapeDtypeStruct((bsz, NSA_GROUPS, T + tb, 2 * NSA_DH), BF16)],
        compiler_params=_cparams(("parallel", "parallel")),
        name="nsa_kprep",
    )(proj3, kc, vc)


def _nsa_select_kernel(ids_ref, cnt_ref, q_ref, ka_ref, va_ref, sel_ref, qc_ref, sm_ref, prev_ref, o_ref,
                       qa_ref, kt_ref, vt_ref, m_ref, acc_ref, *, tq, n_sel, lmax):
    b = pl.program_id(0)
    i = pl.program_id(1)
    B = NSA_SEL_BLOCK
    kpos = i * tq + lax.broadcasted_iota(jnp.int32, (tq, 1), 0)
    tpos = i * tq + lax.broadcasted_iota(jnp.int32, (1, tq), 1)
    causal_bias = jnp.where(kpos <= tpos, 0.0, -BIG)
    hpc = NSA_HPG // NSA_CHAINS
    chains = [slice(c * hpc * tq, (c + 1) * hpc * tq) for c in range(NSA_CHAINS)]
    causal_bias = jnp.concatenate([causal_bias] * hpc, axis=1)
    gates = 1.0 / (1.0 + jnp.exp(-sm_ref[...]))
    own = pl.ds(pl.multiple_of(i * tq, tq), tq)
    for g in range(NSA_GROUPS):
        mcols = ((sel_ref[:, g * n_sel:(g + 1) * n_sel].astype(F32) - 1.0) * MASK_BIG).astype(BF16)
        _stack_queries(q_ref, qc_ref, qa_ref, g, tq, shared=mcols)
        v_own = va_ref[g, own, :].astype(F32).T.astype(BF16)
        for rs in chains:
            s = _dot_nt(ka_ref[g, own, :], qa_ref[rs, :]) + causal_bias
            m0 = jnp.max(s, axis=0, keepdims=True)
            m_ref[:, rs] = m0
            acc_ref[:, rs] = _dot(v_own, jnp.exp2(s - m0).astype(BF16))
        slot = (b * pl.num_programs(1) + i) * NSA_GROUPS + g

        def step(st, carry):
            for u in range(SEL_STEP):
                r0 = pl.multiple_of(ids_ref[slot * lmax + st * SEL_STEP + u] * B, B)
                kt_ref[u * B:(u + 1) * B, :] = ka_ref[g, pl.ds(r0, B), :]
                vt_ref[u * B:(u + 1) * B, :] = va_ref[g, pl.ds(r0, B), :]
            v_t = vt_ref[...].astype(F32).T.astype(BF16)
            for rs in chains:
                s = _dot_nt(kt_ref[...], qa_ref[rs, :])
                m_old = m_ref[:, rs]
                m_new = jnp.maximum(m_old, jnp.max(s, axis=0, keepdims=True))
                p = jnp.exp2(s - m_new).astype(BF16)
                acc_ref[:, rs] = jnp.exp2(m_old - m_new) * acc_ref[:, rs] + _dot(v_t, p)
                m_ref[:, rs] = m_new
            return carry

        lax.fori_loop(0, (cnt_ref[slot] + SEL_STEP - 1) // SEL_STEP, step, 0)
        acc = acc_ref[...]
        o = (acc[:NSA_DH, :] * (1.0 / acc[NSA_DH:NSA_DH + 1, :])).T
        for h in range(NSA_HPG):
            hh = g * NSA_HPG + h
            gate = gates[:, SM_NG + NSA_HEADS + hh:SM_NG + NSA_HEADS + hh + 1]
            o_ref[:, _head_cols(hh)] = (prev_ref[:, _head_cols(hh)] + gate * o[h * tq:(h + 1) * tq, :]
                                        ).astype(o_ref.dtype)


def nsa_select(proj3, sel, act, prev, tq=256):
    bsz, T, _ = proj3.shape
    tq = min(tq, T)
    nt = T // tq
    n_sel = T // NSA_SEL_BLOCK
    lmax = -(-n_sel // SEL_STEP) * SEL_STEP
    tp = min(KPREP_ROWS, T)
    kc, vc = _nsa_key_consts(T, tp, onehot=True)
    qc = _nsa_query_consts()
    kaug, vaug = nsa_kprep(proj3, kc, vc, tp, (OFF_NKV + KV_W) // KV_W)
    wk = kaug.shape[-1]
    on = (act.reshape(bsz, nt, NSA_GROUPS, n_sel) > 0.0).astype(jnp.int32)
    seen = jnp.cumsum(on, axis=-1)
    ids = jnp.sum(seen[..., None, :] <= jnp.arange(lmax, dtype=jnp.int32)[:, None], axis=-1, dtype=jnp.int32)
    ids = ids.reshape(-1)
    cnt = seen[..., -1].reshape(-1)
    rows = NSA_HPG * tq
    grid_spec = pltpu.PrefetchScalarGridSpec(
        num_scalar_prefetch=2,
        grid=(bsz, nt),
        in_specs=[pl.BlockSpec((None, tq, NSA_QW), lambda b, i, *_: (b, i, OFF_NQ // NSA_QW)),
                  pl.BlockSpec((None, NSA_GROUPS, T + tp, wk), lambda b, i, *_: (b, 0, 0, 0)),
                  pl.BlockSpec((None, NSA_GROUPS, T + tp, 2 * NSA_DH), lambda b, i, *_: (b, 0, 0, 0)),
                  pl.BlockSpec((None, tq, NSA_GROUPS * n_sel), lambda b, i, *_: (b, i, 0)),
                  pl.BlockSpec((NSA_HEADS, EXTRA_W), lambda b, i, *_: (0, 0)),
                  pl.BlockSpec((None, tq, LANES), lambda b, i, *_: (b, i, OFF_SMALL // LANES)),
                  pl.BlockSpec((None, tq, NSA_QW), lambda b, i, *_: (b, i, 0))],
        out_specs=pl.BlockSpec((None, tq, NSA_QW), lambda b, i, *_: (b, i, 0)),
        scratch_shapes=[pltpu.VMEM((rows, wk), BF16),
                        pltpu.VMEM((SEL_STEP * NSA_SEL_BLOCK, wk), BF16),
                        pltpu.VMEM((SEL_STEP * NSA_SEL_BLOCK, 2 * NSA_DH), BF16),
                        pltpu.VMEM((1, rows), F32), pltpu.VMEM((2 * NSA_DH, rows), F32)])
    return pl.pallas_call(
        functools.partial(_nsa_select_kernel, tq=tq, n_sel=n_sel, lmax=lmax),
        grid_spec=grid_spec,
        out_shape=jax.ShapeDtypeStruct((bsz, T, NSA_QW), BF16),
        compiler_params=_cparams(("parallel", "arbitrary")),
        name="nsa_select",
    )(ids, cnt, proj3, kaug, vaug, sel, qc, proj3, prev)


def nsa_mixer(proj3, cmp_pos, cmp_w1, cmp_w2):
    kcv = nsa_compress(proj3, cmp_pos, cmp_w1, cmp_w2)
    o_cmp, sel, act = nsa_cmp_topk(proj3, kcv)
    o_cw = nsa_window(proj3, o_cmp)
    return nsa_select(proj3, sel, act, o_cw)


def _pack_w_in(w_in):
    w_in = w_in.astype(BF16)
    (g_q, g_k, g_v, g_r, g_a, n_q, n_kv, n_g, s_z, s_xbc, s_dt, m_g) = jnp.split(w_in, SPLIT_POINTS, axis=-1)
    pad = jnp.zeros(w_in.shape[:-1] + (D_PK - OFF_SMALL - GLA_RANK - 3 * NSA_HEADS - SSM_HEADS,), w_in.dtype)
    return jnp.concatenate([m_g, g_q, g_k, g_v, g_r, n_q, s_z, s_xbc, n_kv, g_a, n_g, s_dt, pad], axis=-1)


def kernel(x, w_in, gla_a2, gla_a_bias, gla_norm, nsa_cmp_pos, nsa_cmp_w1, nsa_cmp_w2, ssm_conv_w, ssm_conv_b,
           ssm_dt_bias, ssm_a_log, ssm_d, ssm_norm, w_branch, w_out, norm_pre_mix, norm_post_mix, norm_pre_ffn,
           norm_post_ffn, w_ffn_gate, w_ffn_up, w_ffn_down):
    bsz, T, D = x.shape
    depth = w_in.shape[0]
    n = bsz * T
    w_in_pk = _pack_w_in(w_in)
    wa_pad = jnp.zeros((depth, LANES, GLA_KW), F32).at[:, SM_GA:SM_GA + GLA_RANK].set(gla_a2).astype(BF16)
    w_branch_b = w_branch.astype(BF16)
    w_out_b = w_out.astype(BF16)
    w_gate_b = w_ffn_gate.astype(BF16)
    w_up_b = w_ffn_up.astype(BF16)
    w_down_b = w_ffn_down.astype(BF16)
    xf = x.reshape(n, D)
    for l in range(depth):
        gates, proj = norm_matmul(xf, norm_pre_mix[l], w_in_pk[l], D_GATES)
        proj3 = proj.reshape(bsz, T, D_PK)
        y_gla = gla_mixer(proj3, wa_pad[l], gla_a_bias[l], gla_norm[l])
        y_nsa = nsa_mixer(proj3, nsa_cmp_pos[l], nsa_cmp_w1[l], nsa_cmp_w2[l])
        y_ssm = ssd_mixer(proj3, ssm_conv_w[l], ssm_conv_b[l], ssm_dt_bias[l], ssm_a_log[l], ssm_d[l], ssm_norm[l])
        merged = merge_branches(y_gla.reshape(n, BRANCH_W), y_nsa.reshape(n, BRANCH_W), y_ssm.reshape(n, BRANCH_W),
                                w_branch_b[l], gates)
        xf = proj_norm_residual(merged, w_out_b[l], xf, norm_post_mix[l], tm=512, tk=D)
        act = ffn_up(xf, norm_pre_ffn[l], w_gate_b[l], w_up_b[l])
        xf = proj_norm_residual(act, w_down_b[l], xf, norm_post_ffn[l], tm=1024, tk=512)
    return xf.reshape(bsz, T, D)
```
